```python
import jax, jax.numpy as jnp
from jax import lax
import numpy as np

D_MODEL = 1024
BATCH = 4
SEQ = 4096
DEPTH = 1

HEAD_DIM = 64
N_ATTN_HEADS = 16
ATTN_WIDTH = N_ATTN_HEADS * HEAD_DIM
ROPE_DIM = HEAD_DIM // 4
ROPE_THETA = 500000.0
DILATED_PATTERNS = ((128, 1), (512, 4), (2048, 16))

D_INNER = 1024
SSM_HEAD_DIM = 64
N_SSM_HEADS = D_INNER // SSM_HEAD_DIM
N_SSM_GROUPS = 4
HEADS_PER_GROUP = N_SSM_HEADS // N_SSM_GROUPS
D_STATE = 128
SSM_CONV = 3
CHUNK = 128
XBC_WIDTH = D_INNER + 2 * N_SSM_GROUPS * D_STATE

MIX_WIDTH = ATTN_WIDTH + D_INNER
IN_WIDTH = 3 * ATTN_WIDTH + D_INNER + XBC_WIDTH + 2 * N_SSM_HEADS

D_FF = 2816
FFN_CONV = 3
EPS = 1e-6

kernel_name = "hymba_dilated_ssd_convffn_encoder"


def rmsnorm(x, w):
    xf = x.astype(jnp.float32)
    y = xf * lax.rsqrt(jnp.mean(xf * xf, axis=-1, keepdims=True) + EPS)
    return (y * w.astype(jnp.float32)).astype(x.dtype)


def dwconv_centered(x, w, b):
    k = w.shape[1]
    rhs = w.T[:, None, :].astype(x.dtype)
    y = lax.conv_general_dilated(x, rhs, window_strides=(1,), padding=[(k // 2, k // 2)],
                                 dimension_numbers=("NWC", "WIO", "NWC"),
                                 feature_group_count=x.shape[-1])
    return y + b.astype(x.dtype)


def partial_rope(t, pos):
    half = ROPE_DIM // 2
    inv_freq = jnp.power(ROPE_THETA, -jnp.arange(half, dtype=jnp.float32) * 2.0 / ROPE_DIM)
    ang = pos[:, None] * inv_freq[None, :]
    cos = jnp.cos(ang)[None, :, None, :]
    sin = jnp.sin(ang)[None, :, None, :]
    tf = t.astype(jnp.float32)
    x1 = tf[..., :half]
    x2 = tf[..., half:ROPE_DIM]
    out = jnp.concatenate([x1 * cos - x2 * sin, x2 * cos + x1 * sin, tf[..., ROPE_DIM:]], axis=-1)
    return out.astype(t.dtype)


def band_attention(q, k, v, half):
    n, length, h, dh = q.shape
    blk = half
    nb = -(-length // blk)
    lp = nb * blk
    qb = jnp.pad(q, [(0, 0), (0, lp - length), (0, 0), (0, 0)]).reshape(n, nb, blk, h, dh)
    pad_kv = [(0, 0), (blk, lp - length + blk), (0, 0), (0, 0)]
    kb = jnp.pad(k, pad_kv).reshape(n, nb + 2, blk, h, dh)
    vb = jnp.pad(v, pad_kv).reshape(n, nb + 2, blk, h, dh)
    kw = jnp.concatenate([kb[:, :-2], kb[:, 1:-1], kb[:, 2:]], axis=2)
    vw = jnp.concatenate([vb[:, :-2], vb[:, 1:-1], vb[:, 2:]], axis=2)
    s = jnp.einsum("nbqhd,nbkhd->nbhqk", qb, kw, preferred_element_type=jnp.float32) * (dh ** -0.5)
    qpos = jnp.arange(nb)[:, None] * blk + jnp.arange(blk)[None, :]
    kpos = jnp.arange(nb)[:, None] * blk - blk + jnp.arange(3 * blk)[None, :]
    valid = ((jnp.abs(kpos[:, None, :] - qpos[:, :, None]) <= half)
             & (kpos >= 0)[:, None, :] & (kpos < length)[:, None, :])
    s = jnp.where(valid[None, :, None], s, -jnp.inf)
    m = jnp.max(s, axis=-1, keepdims=True)
    p = jnp.exp(s - m)
    den = jnp.sum(p, axis=-1)
    o = jnp.einsum("nbhqk,nbkhd->nbqhd", p, vw.astype(jnp.float32))
    o = o / jnp.transpose(den, (0, 1, 3, 2))[..., None]
    lse = jnp.transpose(m[..., 0] + jnp.log(den), (0, 1, 3, 2))
    o = o.reshape(n, lp, h, dh)[:, :length]
    lse = lse.reshape(n, lp, h)[:, :length]
    return o, lse


def dilated_attention(q, k, v):
    b, s, h, dh = q.shape
    outs, lses = [], []
    for window, dil in DILATED_PATTERNS:
        half = (window // 2) // dil
        length = s // dil

        def gather(t):
            t = t.reshape(b, length, dil, h, dh).transpose(0, 2, 1, 3, 4)
            return t.reshape(b * dil, length, h, dh)

        o, lse = band_attention(gather(q), gather(k), gather(v), half)
        o = o.reshape(b, dil, length, h, dh).transpose(0, 2, 1, 3, 4).reshape(b, s, h, dh)
        lse = lse.reshape(b, dil, length, h).transpose(0, 2, 1, 3).reshape(b, s, h)
        outs.append(o)
        lses.append(lse)
    wts = jax.nn.softmax(jnp.stack(lses, axis=0), axis=0)[..., None]
    out = jnp.sum(wts * jnp.stack(outs, axis=0), axis=0)
    return out.astype(q.dtype)


def segsum(a):
    t = a.shape[-1]
    a_rep = jnp.broadcast_to(a[..., :, None], a.shape + (t,))
    a_rep = jnp.where(jnp.tril(jnp.ones((t, t), dtype=bool), -1), a_rep, 0.0)
    cs = jnp.cumsum(a_rep, axis=-2)
    return jnp.where(jnp.tril(jnp.ones((t, t), dtype=bool)), cs, -jnp.inf)


def ssd_scan(x, dt, a_head, bm, cm):
    b, l, g, r, p = x.shape
    c = l // CHUNK
    xc = (x * dt[..., None]).reshape(b, c, CHUNK, g, r, p)
    a = (dt * a_head).reshape(b, c, CHUNK, g, r).transpose(0, 3, 4, 1, 2)
    bc = bm.reshape(b, c, CHUNK, g, -1)
    cc = cm.reshape(b, c, CHUNK, g, -1)
    a_cs = jnp.cumsum(a, axis=-1)
    lmat = jnp.exp(segsum(a))
    cb = jnp.einsum("bclgn,bcsgn->bcgls", cc, bc)
    y_diag = jnp.einsum("bcgls,bgrcls,bcsgrp->bclgrp", cb, lmat, xc)
    decay_states = jnp.exp(a_cs[..., -1:] - a_cs)
    states = jnp.einsum("bclgn,bgrcl,bclgrp->bcgrpn", bc, decay_states, xc)
    states = jnp.concatenate([jnp.zeros_like(states[:, :1]), states], axis=1)
    chunk_tot = jnp.pad(a_cs[..., -1], [(0, 0), (0, 0), (0, 0), (1, 0)])
    decay_chunk = jnp.exp(segsum(chunk_tot))
    states = jnp.einsum("bgrzc,bcgrpn->bzgrpn", decay_chunk, states)[:, :-1]
    y_off = jnp.einsum("bclgn,bcgrpn,bgrcl->bclgrp", cc, states, jnp.exp(a_cs))
    return (y_diag + y_off).reshape(b, l, g, r, p)


def ssm_mixer(z, xbc, dt_f_raw, dt_b_raw, conv_w, conv_b, a_log_f, a_log_b,
              dt_bias_f, dt_bias_b, d_skip, norm_w):
    b, l, _ = z.shape
    xbc = jax.nn.silu(dwconv_centered(xbc, conv_w, conv_b))
    gn = N_SSM_GROUPS * D_STATE
    xs = xbc[..., :D_INNER].astype(jnp.float32).reshape(b, l, N_SSM_GROUPS, HEADS_PER_GROUP, SSM_HEAD_DIM)
    bm = xbc[..., D_INNER:D_INNER + gn].astype(jnp.float32).reshape(b, l, N_SSM_GROUPS, D_STATE)
    cm = xbc[..., D_INNER + gn:].astype(jnp.float32).reshape(b, l, N_SSM_GROUPS, D_STATE)

    def direction(xs_, bm_, cm_, dt_raw, a_log, dt_bias):
        dt = jax.nn.softplus(dt_raw.astype(jnp.float32) + dt_bias.astype(jnp.float32))
        dt = dt.reshape(b, l, N_SSM_GROUPS, HEADS_PER_GROUP)
        a_head = -jnp.exp(a_log.astype(jnp.float32)).reshape(N_SSM_GROUPS, HEADS_PER_GROUP)
        return ssd_scan(xs_, dt, a_head, bm_, cm_)

    flip = lambda t: jnp.flip(t, axis=1)
    y_f = direction(xs, bm, cm, dt_f_raw, a_log_f, dt_bias_f)
    y_b = flip(direction(flip(xs), flip(bm), flip(cm), flip(dt_b_raw), a_log_b, dt_bias_b))
    d = d_skip.astype(jnp.float32).reshape(N_SSM_GROUPS, HEADS_PER_GROUP)[..., None]
    y = (y_f + y_b + d * xs).reshape(b, l, D_INNER)
    gated = (y * jax.nn.silu(z.astype(jnp.float32))).reshape(b, l, N_SSM_GROUPS, D_INNER // N_SSM_GROUPS)
    gated = gated * lax.rsqrt(jnp.mean(gated * gated, axis=-1, keepdims=True) + EPS)
    return (gated.reshape(b, l, D_INNER) * norm_w.astype(jnp.float32)).astype(z.dtype)


def conv_gated_mlp(h, w_up, conv_w, conv_b, w_down):
    u = dwconv_centered(h @ w_up, conv_w, conv_b)
    gate = u[..., :D_FF]
    up = u[..., D_FF:]
    return (jax.nn.silu(gate) * up) @ w_down


def setup_inputs(seed: int = 0) -> dict:
    key = jax.random.key(seed)
    ks = jax.random.split(key, 20)
    f32 = jnp.float32

    def gain(k, shape):
        return 1.0 + 0.02 * jax.random.normal(k, shape, f32)

    def dt_bias(k):
        dt = jnp.exp(jax.random.uniform(k, (DEPTH, N_SSM_HEADS), f32, np.log(1e-3), np.log(1e-1)))
        return dt + jnp.log(-jnp.expm1(-dt))

    return {
        "x": jax.random.normal(ks[0], (BATCH, SEQ, D_MODEL), f32),
        "norm1_w": gain(ks[1], (DEPTH, D_MODEL)),
        "w_in": jax.random.normal(ks[2], (DEPTH, D_MODEL, IN_WIDTH), f32) * D_MODEL ** -0.5,
        "ssm_conv_w": jax.random.normal(ks[3], (DEPTH, XBC_WIDTH, SSM_CONV), f32) * SSM_CONV ** -0.5,
        "ssm_conv_b": 0.02 * jax.random.normal(ks[4], (DEPTH, XBC_WIDTH), f32),
        "a_log_f": jnp.log(jax.random.uniform(ks[5], (DEPTH, N_SSM_HEADS), f32, 1.0, 16.0)),
        "a_log_b": jnp.log(jax.random.uniform(ks[6], (DEPTH, N_SSM_HEADS), f32, 1.0, 16.0)),
        "dt_bias_f": dt_bias(ks[7]),
        "dt_bias_b": dt_bias(ks[8]),
        "d_skip": gain(ks[9], (DEPTH, N_SSM_HEADS)),
        "ssm_norm_w": gain(ks[10], (DEPTH, D_INNER)),
        "w_out": jax.random.normal(ks[11], (DEPTH, MIX_WIDTH, D_MODEL), f32) * MIX_WIDTH ** -0.5,
        "norm2_w": gain(ks[12], (DEPTH, D_MODEL)),
        "w_up": jax.random.normal(ks[13], (DEPTH, D_MODEL, 2 * D_FF), f32) * D_MODEL ** -0.5,
        "ffn_conv_w": jax.random.normal(ks[14], (DEPTH, 2 * D_FF, FFN_CONV), f32) * FFN_CONV ** -0.5,
        "ffn_conv_b": 0.02 * jax.random.normal(ks[15], (DEPTH, 2 * D_FF), f32),
        "w_down": jax.random.normal(ks[16], (DEPTH, D_FF, D_MODEL), f32) * D_FF ** -0.5,
        "final_norm_w": gain(ks[17], (D_MODEL,)),
    }


def reference(x, norm1_w, w_in, ssm_conv_w, ssm_conv_b, a_log_f, a_log_b, dt_bias_f, dt_bias_b,
              d_skip, ssm_norm_w, w_out, norm2_w, w_up, ffn_conv_w, ffn_conv_b, w_down, final_norm_w):
    b, s, _ = x.shape
    pos = jnp.arange(s, dtype=jnp.float32)
    sizes = [ATTN_WIDTH, ATTN_WIDTH, ATTN_WIDTH, D_INNER, XBC_WIDTH, N_SSM_HEADS, N_SSM_HEADS]
    splits = [int(v) for v in np.cumsum(sizes)[:-1]]
    for layer in range(DEPTH):
        h = rmsnorm(x, norm1_w[layer])
        proj = h @ w_in[layer]
        q, k, v, z, xbc, dt_f, dt_b = jnp.split(proj, splits, axis=-1)
        q = partial_rope(q.reshape(b, s, N_ATTN_HEADS, HEAD_DIM), pos)
        k = partial_rope(k.reshape(b, s, N_ATTN_HEADS, HEAD_DIM), pos)
        v = v.reshape(b, s, N_ATTN_HEADS, HEAD_DIM)
        attn = dilated_attention(q, k, v).reshape(b, s, ATTN_WIDTH)
        ssm = ssm_mixer(z, xbc, dt_f, dt_b, ssm_conv_w[layer], ssm_conv_b[layer], a_log_f[layer],
                        a_log_b[layer], dt_bias_f[layer], dt_bias_b[layer], d_skip[layer], ssm_norm_w[layer])
        x = x + jnp.concatenate([attn, ssm], axis=-1) @ w_out[layer]
        h = rmsnorm(x, norm2_w[layer])
        x = x + conv_gated_mlp(h, w_up[layer], ffn_conv_w[layer], ffn_conv_b[layer], w_down[layer])
    return rmsnorm(x, final_norm_w)
```

```python
import functools

import numpy as np
import jax
import jax.numpy as jnp
from jax import lax
from jax.experimental import pallas as pl
from jax.experimental.pallas import tpu as pltpu

F32 = jnp.float32
BF16 = jnp.bfloat16

D_MODEL = 1024
HEAD_DIM = 64
N_ATTN_HEADS = 16
ATTN_WIDTH = N_ATTN_HEADS * HEAD_DIM
ROPE_DIM = HEAD_DIM // 4
ROPE_THETA = 500000.0
DILATIONS = (1, 4, 16)
BAND_HALF = 64

D_INNER = 1024
N_SSM_HEADS = 16
N_SSM_GROUPS = 4
D_STATE = 128
CHUNK = 128
XBC_WIDTH = D_INNER + 2 * N_SSM_GROUPS * D_STATE
D_FF = 2816
EPS = 1e-6

LANES = 128
BF16_ROWS = 16
VMEM_LIMIT = 56 * 1024 * 1024
NEG_BIG = -1e30


def _params(*sem):
    return pltpu.CompilerParams(dimension_semantics=sem, vmem_limit_bytes=VMEM_LIMIT)


def _const_spec(shape):
    return pl.BlockSpec(shape, lambda *_: (0,) * len(shape))


def _rms(x, w):
    return x * lax.rsqrt(jnp.mean(x * x, axis=-1, keepdims=True) + EPS) * w


def _silu(y):
    return y / (1.0 + jnp.exp(-y))


def _inproj_kernel(x_ref, nw_ref, wq_ref, wk_ref, wv_ref, wz_ref, wx_ref, wdt_ref,
                   rc_ref, rs1_ref, rs2_ref,
                   q_ref, k_ref, v_ref, z_ref, xbc_ref, dt_ref):
    h = _rms(x_ref[...], nw_ref[...]).astype(BF16)
    rc, rs1, rs2 = rc_ref[...], rs1_ref[...], rs2_ref[...]

    def rope_store(w_ref, o_ref):
        t = jnp.dot(h, w_ref[...], preferred_element_type=F32)
        for cb in range(ATTN_WIDTH // LANES):
            blk = t[:, cb * LANES:(cb + 1) * LANES]
            lo = pltpu.roll(blk, ROPE_DIM // 2, 1)
            hi = pltpu.roll(blk, LANES - ROPE_DIM // 2, 1)
            o_ref[:, cb * LANES:(cb + 1) * LANES] = (blk * rc + lo * rs1 + hi * rs2).astype(BF16)

    rope_store(wq_ref, q_ref)
    rope_store(wk_ref, k_ref)
    v_ref[...] = jnp.dot(h, wv_ref[...], preferred_element_type=F32).astype(BF16)
    z_ref[...] = jnp.dot(h, wz_ref[...], preferred_element_type=F32).astype(BF16)
    xbc_ref[...] = jnp.dot(h, wx_ref[...], preferred_element_type=F32).astype(BF16)
    dt_ref[...] = jnp.dot(h, wdt_ref[...], preferred_element_type=F32)


def _rope_tables(seq):
    half = ROPE_DIM // 2
    inv_freq = jnp.power(ROPE_THETA, -jnp.arange(half, dtype=F32) * 2.0 / ROPE_DIM)
    ang = jnp.arange(seq, dtype=F32)[:, None] * inv_freq[None, :]
    cos, sin = jnp.cos(ang), jnp.sin(ang)
    one = jnp.ones((seq, HEAD_DIM - ROPE_DIM), F32)
    zero8 = jnp.zeros((seq, half), F32)
    zero = jnp.zeros((seq, HEAD_DIM - ROPE_DIM), F32)
    rc = jnp.concatenate([cos, cos, one], axis=1)
    rs1 = jnp.concatenate([zero8, sin, zero], axis=1)
    rs2 = jnp.concatenate([-sin, zero8, zero], axis=1)
    rep = LANES // HEAD_DIM
    return tuple(jnp.tile(t, (1, rep)) for t in (rc, rs1, rs2))


def _in_projection(x2d, norm_w, w_in, seq, tm=256):
    t_rows = x2d.shape[0]
    a = ATTN_WIDTH
    wq = (w_in[:, :a] * (HEAD_DIM ** -0.5)).astype(BF16)
    wk = w_in[:, a:2 * a].astype(BF16)
    wv = w_in[:, 2 * a:3 * a].astype(BF16)
    wz = w_in[:, 3 * a:3 * a + D_INNER].astype(BF16)
    o = 3 * a + D_INNER
    wx = w_in[:, o:o + XBC_WIDTH].astype(BF16)
    wdt = jnp.pad(w_in[:, o + XBC_WIDTH:], ((0, 0), (0, LANES - 2 * N_SSM_HEADS))).astype(BF16)
    rc, rs1, rs2 = _rope_tables(seq)
    nseq = seq // tm
    row = lambda width: pl.BlockSpec((tm, width), lambda i: (i, 0))
    tab = pl.BlockSpec((tm, LANES), lambda i: (i % nseq, 0))
    outs = pl.pallas_call(
        _inproj_kernel,
        grid=(t_rows // tm,),
        in_specs=[row(D_MODEL), _const_spec((1, D_MODEL)),
                  _const_spec((D_MODEL, a)), _const_spec((D_MODEL, a)), _const_spec((D_MODEL, a)),
                  _const_spec((D_MODEL, D_INNER)), _const_spec((D_MODEL, XBC_WIDTH)),
                  _const_spec((D_MODEL, LANES)), tab, tab, tab],
        out_specs=[row(a), row(a), row(a), row(D_INNER), row(XBC_WIDTH), row(LANES)],
        out_shape=[jax.ShapeDtypeStruct((t_rows, a), BF16)] * 3
        + [jax.ShapeDtypeStruct((t_rows, D_INNER), BF16),
           jax.ShapeDtypeStruct((t_rows, XBC_WIDTH), BF16),
           jax.ShapeDtypeStruct((t_rows, LANES), F32)],
        compiler_params=_params("parallel"),
        name="in_projection",
    )(x2d, norm_w.reshape(1, D_MODEL), wq, wk, wv, wz, wx, wdt, rc, rs1, rs2)
    return outs


ATT_TQ = 128
ATT_TK = ATT_TQ + 2 * BAND_HALF
ATT_HEADS_PER_STEP = 8


def _attn_kernel(*refs, length, n_sub, first, last):
    if first:
        q_ref, k_ref, v_ref, o_ref, st_ref = refs
        op_ref = sp_ref = None
    elif last:
        q_ref, k_ref, v_ref, op_ref, sp_ref, o_ref = refs
        st_ref = None
    else:
        q_ref, k_ref, v_ref, op_ref, sp_ref, o_ref, st_ref = refs
    tq, tk = ATT_TQ, ATT_TK
    n_pairs = ATT_HEADS_PER_STEP // 2
    qi = pl.program_id(3)
    lane = lax.broadcasted_iota(jnp.int32, (tq, LANES), 1)
    even = lane < HEAD_DIM
    stat_is_max = (lane & 15) < 8
    delta = (lax.broadcasted_iota(jnp.int32, (tq, tk), 1)
             - lax.broadcasted_iota(jnp.int32, (tq, tk), 0))
    nt = (((1,), (1,)), ((), ()))

    def sub_block(sb, carry):
        r0 = pl.multiple_of(sb * tq, tq)
        q0 = (qi * n_sub + sb) * tq
        ws = pl.multiple_of(jnp.clip(q0 - BAND_HALF, 0, length - tk), BAND_HALF)
        bias = jnp.where(jnp.abs(delta + (ws - q0)) <= BAND_HALF, 0.0, NEG_BIG)
        stats = jnp.zeros((tq, LANES), F32)
        for hp in range(n_pairs):
            cols = slice(hp * LANES, (hp + 1) * LANES)
            q2 = q_ref[0, pl.ds(r0, tq), cols]
            k2 = k_ref[0, pl.ds(ws, tk), cols]
            v2 = v_ref[0, pl.ds(ws, tk), cols]
            zero = jnp.zeros_like(q2)
            halves = []
            for qh in (jnp.where(even, q2, zero), jnp.where(even, zero, q2)):
                s = lax.dot_general(qh, k2, nt, preferred_element_type=F32) + bias
                m = jnp.max(s, axis=-1, keepdims=True)
                p = jnp.exp(s - m)
                l = jnp.sum(p, axis=-1, keepdims=True)
                pv = jnp.dot(p.astype(BF16), v2, preferred_element_type=F32)
                halves.append((m, l, pv))
            (me, le, pve), (mo, lo, pvo) = halves
            m = jnp.where(even, me, mo)
            l = jnp.where(even, le, lo)
            acc = jnp.where(even, pve, pvo)
            if not first:
                sp = sp_ref[0, pl.ds(r0, tq), :]
                be, bo = 16 * hp, HEAD_DIM + 16 * hp
                m_prev = jnp.where(even, sp[:, be:be + 1], sp[:, bo:bo + 1])
                l_prev = jnp.where(even, sp[:, be + 8:be + 9], sp[:, bo + 8:bo + 9])
                acc_prev = op_ref[0, pl.ds(r0, tq), cols].astype(F32)
                m_new = jnp.maximum(m_prev, m)
                a_prev = jnp.exp(m_prev - m_new)
                a_cur = jnp.exp(m - m_new)
                acc = acc_prev * a_prev + acc * a_cur
                l = l_prev * a_prev + l * a_cur
                m = m_new
            if last:
                o_ref[0, pl.ds(r0, tq), cols] = (acc / l).astype(BF16)
            else:
                o_ref[0, pl.ds(r0, tq), cols] = acc.astype(BF16)
                in_zone = ((lane & (HEAD_DIM - 1)) >> 4) == hp
                stats = jnp.where(in_zone, jnp.where(stat_is_max, m, l), stats)
        if not last:
            st_ref[0, pl.ds(r0, tq), :] = stats
        return carry

    lax.fori_loop(0, n_sub, sub_block, 0)


def _attention_pattern(q, k, v, o_prev, st_prev, dil, first, last):
    b, s, width = q.shape
    length = s // dil
    qs = min(length, 512)
    n_sub = qs // ATT_TQ
    hw = ATT_HEADS_PER_STEP * HEAD_DIM
    n_hg = width // hw
    view = lambda t: t.reshape(b, length, dil * t.shape[-1])
    blk = pl.BlockSpec((1, qs, hw), lambda bi, r, g, qi: (bi, qi, r * n_hg + g))
    seq = pl.BlockSpec((1, length, hw), lambda bi, r, g, qi: (bi, 0, r * n_hg + g))
    stat = pl.BlockSpec((1, qs, LANES), lambda bi, r, g, qi: (bi, qi, r * n_hg + g))
    in_specs, args = [blk, seq, seq], [view(q), view(k), view(v)]
    if not first:
        in_specs += [blk, stat]
        args += [view(o_prev), view(st_prev)]
    out_specs = [blk]
    out_shape = [jax.ShapeDtypeStruct((b, length, dil * width), BF16)]
    if not last:
        out_specs.append(stat)
        out_shape.append(jax.ShapeDtypeStruct((b, length, dil * n_hg * LANES), F32))
    outs = pl.pallas_call(
        functools.partial(_attn_kernel, length=length, n_sub=n_sub, first=first, last=last),
        grid=(b, dil, n_hg, length // qs),
        in_specs=in_specs, out_specs=out_specs, out_shape=out_shape,
        compiler_params=_params("parallel", "parallel", "parallel", "arbitrary"),
        name=f"attention_dil{dil}",
    )(*args)
    o = outs[0].reshape(b, s, width)
    st = None if last else outs[1].reshape(b, s, n_hg * LANES)
    return o, st


def _dilated_attention(q, k, v):
    o = st = None
    for i, dil in enumerate(DILATIONS):
        o, st = _attention_pattern(q, k, v, o, st, dil, i == 0, i == len(DILATIONS) - 1)
    return o


def _split_dot(v, mat, passes):
    out = None
    r = v
    for i in range(passes):
        piece = r.astype(BF16)
        term = jnp.dot(piece, mat, preferred_element_type=F32)
        out = term if out is None else out + term
        if i + 1 < passes:
            r = r - piece.astype(F32)
    return out


def _ssm_kernel(xc_ref, xp_ref, xn_ref, dt_ref, z_ref, cw_ref, cb_ref, alog_ref, dtb_ref,
                dsk_ref, nw_ref, ef_ref, eb_ref,
                y_ref, hb_ref, hrun_ref, *, n_chunks):
    L = CHUNK
    step = pl.program_id(1)
    is_bwd = step < n_chunks
    c = jnp.where(is_bwd, n_chunks - 1 - step, step - n_chunks)

    @pl.when((step == 0) | (step == n_chunks))
    def _():
        hrun_ref[...] = jnp.zeros_like(hrun_ref)

    row = lax.broadcasted_iota(jnp.int32, (L, 1), 0)
    has_prev = (c > 0).astype(F32)
    has_next = (c < n_chunks - 1).astype(F32)

    def conv_silu(lo, hi):
        xc = xc_ref[0, :, lo:hi].astype(F32)
        prev_row = xp_ref[0, :, lo:hi].astype(F32)[BF16_ROWS - 1:BF16_ROWS] * has_prev
        next_row = xn_ref[0, :, lo:hi].astype(F32)[0:1] * has_next
        x_m1 = jnp.where(row == 0, prev_row, pltpu.roll(xc, 1, 0))
        x_p1 = jnp.where(row == L - 1, next_row, pltpu.roll(xc, L - 1, 0))
        y = (x_m1 * cw_ref[0:1, lo:hi] + xc * cw_ref[1:2, lo:hi] + x_p1 * cw_ref[2:3, lo:hi]
             + cb_ref[:, lo:hi])
        return _silu(y)

    x_dt = dt_ref[0] + dtb_ref[...]
    dt = jnp.maximum(x_dt, 0.0) + jnp.log1p(jnp.exp(-jnp.abs(x_dt)))
    a = dt * (-jnp.exp(alog_ref[...]))
    li = lax.broadcasted_iota(jnp.int32, (L, L), 0)
    si = lax.broadcasted_iota(jnp.int32, (L, L), 1)
    tri = (si <= li).astype(BF16)
    cs = _split_dot_lhs_const(tri, a)
    ecs = cs - a
    tot = cs[L - 1:L, :]
    ldt = jnp.log(dt)

    xs = conv_silu(0, D_INNER)
    bm = conv_silu(D_INNER, D_INNER + N_SSM_GROUPS * D_STATE)
    xs_b = xs.astype(BF16)
    bts = [bm[:, g * D_STATE:(g + 1) * D_STATE].T.astype(BF16) for g in range(N_SSM_GROUPS)]

    def state_update(weights, e_ref):
        xw = (_split_dot(weights, e_ref[...], 2) * xs).astype(BF16)
        decay = _split_dot(jnp.broadcast_to(jnp.exp(tot), (8, LANES)), e_ref[...], 3)[0:1]
        gw = D_INNER // N_SSM_GROUPS
        for g in range(N_SSM_GROUPS):
            cols = slice(g * gw, (g + 1) * gw)
            s_g = jnp.dot(bts[g], xw[:, cols], preferred_element_type=F32)
            hrun_ref[:, cols] = hrun_ref[:, cols] * decay[:, cols] + s_g

    @pl.when(is_bwd)
    def _():
        hb_ref[c] = hrun_ref[...].astype(BF16)
        state_update(jnp.exp(ecs + ldt), eb_ref)

    @pl.when(jnp.logical_not(is_bwd))
    def _():
        cm = conv_silu(D_INNER + N_SSM_GROUPS * D_STATE, XBC_WIDTH)
        hf_in = hrun_ref[...].astype(BF16)
        hb_in = hb_ref[c]
        scale_f = _split_dot(jnp.exp(cs), ef_ref[...], 2)
        scale_b = _split_dot(jnp.exp(tot - ecs), eb_ref[...], 2)
        row_f = (cs - ldt).T
        row_b = (ecs + ldt).T
        lane = lax.broadcasted_iota(jnp.int32, (L, LANES), 1)
        even = lane < HEAD_DIM
        lower = si <= li
        upper = si >= li
        gw = D_INNER // N_SSM_GROUPS
        hpg = N_SSM_HEADS // N_SSM_GROUPS
        y_groups = []
        for g in range(N_SSM_GROUPS):
            cols = slice(g * gw, (g + 1) * gw)
            cg = cm[:, g * D_STATE:(g + 1) * D_STATE].astype(BF16)
            bg = bm[:, g * D_STATE:(g + 1) * D_STATE].astype(BF16)
            gmat = lax.dot_general(cg, bg, (((1,), (1,)), ((), ())), preferred_element_type=F32)
            y_off = (scale_f[:, cols] * jnp.dot(cg, hf_in[:, cols], preferred_element_type=F32)
                     + scale_b[:, cols] * jnp.dot(cg, hb_in[:, cols], preferred_element_type=F32))
            pairs = []
            for pr in range(hpg // 2):
                h0 = g * hpg + 2 * pr
                xs_pair = xs_b[:, h0 * HEAD_DIM:(h0 + 2) * HEAD_DIM]
                res = []
                for h in (h0, h0 + 1):
                    hb = N_SSM_HEADS + h
                    wf = jnp.where(lower, jnp.exp(cs[:, h:h + 1] - row_f[h:h + 1, :]), 0.0)
                    wb = jnp.where(upper, jnp.exp(row_b[hb:hb + 1, :] - ecs[:, hb:hb + 1]), 0.0)
                    w = (gmat * (wf + wb)).astype(BF16)
                    res.append(jnp.dot(w, xs_pair, preferred_element_type=F32))
                pairs.append(jnp.where(even, res[0], res[1]))
            y_groups.append(jnp.concatenate(pairs, axis=1) + y_off)
        y = jnp.concatenate(y_groups, axis=1) + dsk_ref[...] * xs
        zf = z_ref[0].astype(F32)
        gated = y * _silu(zf)
        outs = []
        for g in range(N_SSM_GROUPS):
            gg = gated[:, g * gw:(g + 1) * gw]
            outs.append(gg * lax.rsqrt(jnp.mean(gg * gg, axis=-1, keepdims=True) + EPS))
        y_ref[0] = (jnp.concatenate(outs, axis=1) * nw_ref[...]).astype(BF16)
        state_update(jnp.exp(tot - cs + ldt), ef_ref)


def _split_dot_lhs_const(mat, v):
    out = None
    r = v
    for i in range(3):
        piece = r.astype(BF16)
        term = jnp.dot(mat, piece, preferred_element_type=F32)
        out = term if out is None else out + term
        if i < 2:
            r = r - piece.astype(F32)
    return out


def _head_expanders():
    col_head = np.arange(D_INNER) // HEAD_DIM
    rows = np.arange(LANES)[:, None]
    ef = (rows == col_head[None, :]).astype(np.float32)
    eb = (rows == col_head[None, :] + N_SSM_HEADS).astype(np.float32)
    return jnp.asarray(ef, BF16), jnp.asarray(eb, BF16)


def _ssm_mixer(z, xbc, dt_raw, conv_w, conv_b, a_log_f, a_log_b, dt_bias_f, dt_bias_b, d_skip, norm_w):
    b, s, _ = z.shape
    n = s // CHUNK
    pad = LANES - 2 * N_SSM_HEADS
    alog = jnp.pad(jnp.concatenate([a_log_f, a_log_b]), (0, pad)).reshape(1, LANES)
    dtb = jnp.pad(jnp.concatenate([dt_bias_f, dt_bias_b]), (0, pad)).reshape(1, LANES)
    dsk = jnp.repeat(d_skip, HEAD_DIM).reshape(1, D_INNER)
    ef, eb = _head_expanders()
    hpc = CHUNK // BF16_ROWS
    last_halo = s // BF16_ROWS - 1

    def chunk_of(st):
        return jnp.where(st < n, n - 1 - st, st - n)

    return pl.pallas_call(
        functools.partial(_ssm_kernel, n_chunks=n),
        grid=(b, 2 * n),
        in_specs=[
            pl.BlockSpec((1, CHUNK, XBC_WIDTH), lambda bi, st: (bi, chunk_of(st), 0)),
            pl.BlockSpec((1, BF16_ROWS, XBC_WIDTH),
                         lambda bi, st: (bi, jnp.maximum(chunk_of(st) * hpc - 1, 0), 0)),
            pl.BlockSpec((1, BF16_ROWS, XBC_WIDTH),
                         lambda bi, st: (bi, jnp.minimum((chunk_of(st) + 1) * hpc, last_halo), 0)),
            pl.BlockSpec((1, CHUNK, LANES), lambda bi, st: (bi, chunk_of(st), 0)),
            pl.BlockSpec((1, CHUNK, D_INNER), lambda bi, st: (bi, jnp.maximum(st - n, 0), 0)),
            _const_spec((3, XBC_WIDTH)), _const_spec((1, XBC_WIDTH)),
            _const_spec((1, LANES)), _const_spec((1, LANES)),
            _const_spec((1, D_INNER)), _const_spec((1, D_INNER)),
            _const_spec((LANES, D_INNER)), _const_spec((LANES, D_INNER)),
        ],
        out_specs=pl.BlockSpec((1, CHUNK, D_INNER), lambda bi, st: (bi, jnp.maximum(st - n, 0), 0)),
        out_shape=jax.ShapeDtypeStruct((b, s, D_INNER), BF16),
        scratch_shapes=[pltpu.VMEM((n, D_STATE, D_INNER), BF16), pltpu.VMEM((D_STATE, D_INNER), F32)],
        compiler_params=_params("arbitrary", "arbitrary"),
        name="ssm_mixer",
    )(xbc, xbc, xbc, dt_raw, z, conv_w.T, conv_b.reshape(1, XBC_WIDTH), alog, dtb, dsk,
      norm_w.reshape(1, D_INNER), ef, eb)


def _outproj_kernel(x_ref, a_ref, s_ref, wa_ref, ws_ref, nw_ref, x1_ref, h_ref):
    x1 = (x_ref[...] + jnp.dot(a_ref[...], wa_ref[...], preferred_element_type=F32)
          + jnp.dot(s_ref[...], ws_ref[...], preferred_element_type=F32))
    x1_ref[...] = x1
    h_ref[...] = _rms(x1, nw_ref[...]).astype(BF16)


def _out_projection(x2d, attn, ssm, w_out, norm_w, tm=512):
    t_rows = x2d.shape[0]
    row = pl.BlockSpec((tm, D_MODEL), lambda i: (i, 0))
    wa = w_out[:ATTN_WIDTH].astype(BF16)
    ws = w_out[ATTN_WIDTH:].astype(BF16)
    return pl.pallas_call(
        _outproj_kernel,
        grid=(t_rows // tm,),
        in_specs=[row, row, row, _const_spec((ATTN_WIDTH, D_MODEL)), _const_spec((D_INNER, D_MODEL)),
                  _const_spec((1, D_MODEL))],
        out_specs=[row, row],
        out_shape=[jax.ShapeDtypeStruct((t_rows, D_MODEL), F32),
                   jax.ShapeDtypeStruct((t_rows, D_MODEL), BF16)],
        compiler_params=_params("parallel"),
        name="out_projection",
    )(x2d, attn, ssm, wa, ws, norm_w.reshape(1, D_MODEL))


FFN_UP_COLS = 1408
FFN_DOWN_COLS = 256


def _ffn_up_kernel(h_ref, w_ref, u_ref):
    h = h_ref[...]
    for j in range(2 * D_FF // FFN_UP_COLS):
        cols = slice(j * FFN_UP_COLS, (j + 1) * FFN_UP_COLS)
        u_ref[:, cols] = jnp.dot(h, w_ref[:, cols], preferred_element_type=F32).astype(BF16)


def _ffn_up(h, w_up, tm=512):
    t_rows = h.shape[0]
    return pl.pallas_call(
        _ffn_up_kernel,
        grid=(t_rows // tm,),
        in_specs=[pl.BlockSpec((tm, D_MODEL), lambda i: (i, 0)), _const_spec((D_MODEL, 2 * D_FF))],
        out_specs=pl.BlockSpec((tm, 2 * D_FF), lambda i: (i, 0)),
        out_shape=jax.ShapeDtypeStruct((t_rows, 2 * D_FF), BF16),
        compiler_params=_params("parallel"),
        name="ffn_up",
    )(h, w_up.astype(BF16))


def _ffn_down_kernel(uc_ref, up_ref, un_ref, x1_ref, cw_ref, cb_ref, wd_ref, nw_ref, o_ref, *, blocks_per_seq):
    tm = uc_ref.shape[0]
    i = pl.program_id(0)
    pos = i % blocks_per_seq
    has_prev = (pos > 0).astype(F32)
    has_next = (pos < blocks_per_seq - 1).astype(F32)
    row = lax.broadcasted_iota(jnp.int32, (tm, 1), 0)

    def conv(lo):
        hi = lo + FFN_DOWN_COLS
        uc = uc_ref[:, lo:hi].astype(F32)
        prev_row = up_ref[:, lo:hi].astype(F32)[BF16_ROWS - 1:BF16_ROWS] * has_prev
        next_row = un_ref[:, lo:hi].astype(F32)[0:1] * has_next
        u_m1 = jnp.where(row == 0, prev_row, pltpu.roll(uc, 1, 0))
        u_p1 = jnp.where(row == tm - 1, next_row, pltpu.roll(uc, tm - 1, 0))
        return (u_m1 * cw_ref[0:1, lo:hi] + uc * cw_ref[1:2, lo:hi] + u_p1 * cw_ref[2:3, lo:hi]
                + cb_ref[:, lo:hi])

    acc = x1_ref[...]
    for j in range(D_FF // FFN_DOWN_COLS):
        lo = j * FFN_DOWN_COLS
        act = (_silu(conv(lo)) * conv(D_FF + lo)).astype(BF16)
        acc = acc + jnp.dot(act, wd_ref[lo:lo + FFN_DOWN_COLS, :], preferred_element_type=F32)
    o_ref[...] = _rms(acc, nw_ref[...])


def _ffn_down(u, x1, conv_w, conv_b, w_down, norm_w, seq, tm=256):
    t_rows = u.shape[0]
    hpb = tm // BF16_ROWS
    last_halo = t_rows // BF16_ROWS - 1
    width = 2 * D_FF
    return pl.pallas_call(
        functools.partial(_ffn_down_kernel, blocks_per_seq=seq // tm),
        grid=(t_rows // tm,),
        in_specs=[
            pl.BlockSpec((tm, width), lambda i: (i, 0)),
            pl.BlockSpec((BF16_ROWS, width), lambda i: (jnp.maximum(i * hpb - 1, 0), 0)),
            pl.BlockSpec((BF16_ROWS, width), lambda i: (jnp.minimum((i + 1) * hpb, last_halo), 0)),
            pl.BlockSpec((tm, D_MODEL), lambda i: (i, 0)),
            _const_spec((3, width)), _const_spec((1, width)),
            _const_spec((D_FF, D_MODEL)), _const_spec((1, D_MODEL)),
        ],
        out_specs=pl.BlockSpec((tm, D_MODEL), lambda i: (i, 0)),
        out_shape=jax.ShapeDtypeStruct((t_rows, D_MODEL), F32),
        compiler_params=_params("parallel"),
        name="ffn_down",
    )(u, u, u, x1, conv_w.T, conv_b.reshape(1, width), w_down.astype(BF16), norm_w.reshape(1, D_MODEL))


def kernel(x, norm1_w, w_in, ssm_conv_w, ssm_conv_b, a_log_f, a_log_b, dt_bias_f, dt_bias_b, d_skip,
           ssm_norm_w, w_out, norm2_w, w_up, ffn_conv_w, ffn_conv_b, w_down, final_norm_w):
    b, s, d = x.shape
    depth = w_in.shape[0]
    x2d = x.reshape(b * s, d)
    for layer in range(depth):
        q, k, v, z, xbc, dt_raw = _in_projection(x2d, norm1_w[layer], w_in[layer], s)
        sh = lambda t: t.reshape(b, s, t.shape[-1])
        attn = _dilated_attention(sh(q), sh(k), sh(v))
        ssm = _ssm_mixer(sh(z), sh(xbc), sh(dt_raw), ssm_conv_w[layer], ssm_conv_b[layer],
                         a_log_f[layer], a_log_b[layer], dt_bias_f[layer], dt_bias_b[layer],
                         d_skip[layer], ssm_norm_w[layer])
        x1, h2 = _out_projection(x2d, attn.reshape(b * s, -1), ssm.reshape(b * s, -1),
                                 w_out[layer], norm2_w[layer])
        u = _ffn_up(h2, w_up[layer])
        assert depth == 1
        x2d = _ffn_down(u, x1, ffn_conv_w[layer], ffn_conv_b[layer], w_down[layer], final_norm_w, s)
    return x2d.reshape(b, s, d)
```

```python
import functools

import numpy as np
import jax
import jax.numpy as jnp
from jax import lax
from jax.experimental import pallas as pl
from jax.experimental.pallas import tpu as pltpu

F32 = jnp.float32
BF16 = jnp.bfloat16

D_MODEL = 1024
HEAD_DIM = 64
N_ATTN_HEADS = 16
ATTN_WIDTH = N_ATTN_HEADS * HEAD_DIM
ROPE_DIM = HEAD_DIM // 4
ROPE_THETA = 500000.0
DILATIONS = (1, 4, 16)
BAND_HALF = 64
CARRY_SPLIT = 4

D_INNER = 1024
N_SSM_HEADS = 16
N_SSM_GROUPS = 4
D_STATE = 128
CHUNK = 128
XBC_WIDTH = D_INNER + 2 * N_SSM_GROUPS * D_STATE
D_FF = 2816
EPS = 1e-6

LANES = 128
BF16_ROWS = 16
VMEM_LIMIT = 56 * 1024 * 1024
NEG_BIG = -1e30
LOG2E = 1.4426950408889634


def _params(*sem):
    return pltpu.CompilerParams(dimension_semantics=sem, vmem_limit_bytes=VMEM_LIMIT)


def _const_spec(shape):
    return pl.BlockSpec(shape, lambda *_: (0,) * len(shape))


def _rms(x, w):
    return x * lax.rsqrt(jnp.mean(x * x, axis=-1, keepdims=True) + EPS) * w


def _silu(y):
    h = 0.5 * y
    return h + h * jnp.tanh(h)


def _centered_conv3(xc, prev_row, next_row, w, b):
    r = xc.shape[0]
    y = pltpu.roll(xc, 1, 0) * w[0:1] + xc * w[1:2] + pltpu.roll(xc, r - 1, 0) * w[2:3] + b
    sub = lax.broadcasted_iota(jnp.int32, (8, 1), 0)
    top = y[0:8] + jnp.where(sub == 0, (prev_row - xc[r - 1:r]) * w[0:1], 0.0)
    bot = y[r - 8:r] + jnp.where(sub == 7, (next_row - xc[0:1]) * w[2:3], 0.0)
    return jnp.concatenate([top, y[8:r - 8], bot], axis=0)


def _inproj_kernel(x_ref, nw_ref, wq_ref, wk_ref, wv_ref, wz_ref, wx_ref, wdt_ref,
                   rc_ref, rs1_ref, rs2_ref,
                   q1_ref, k1_ref, v1_ref, q4_ref, k4_ref, v4_ref, q16_ref, k16_ref, v16_ref,
                   z_ref, xbc_ref, dt_ref, slab_ref, slab2_ref):
    tm = x_ref.shape[0]
    h = _rms(x_ref[...], nw_ref[...]).astype(BF16)
    rc, rs1, rs2 = rc_ref[...], rs1_ref[...], rs2_ref[...]
    n_cb = ATTN_WIDTH // LANES

    def emit(idx, w_ref, out_refs, rope):
        t = jnp.dot(h, w_ref[...], preferred_element_type=F32)
        for cb in range(n_cb):
            cols = slice(cb * LANES, (cb + 1) * LANES)
            blk = t[:, cols]
            if rope:
                lo = pltpu.roll(blk, ROPE_DIM // 2, 1)
                hi = pltpu.roll(blk, LANES - ROPE_DIM // 2, 1)
                blk = blk * rc + lo * rs1 + hi * rs2
            slab = slab_ref.at[idx * n_cb + cb]
            slab2 = slab2_ref.at[idx * n_cb + cb]
            slab[...] = blk
            out_refs[0][:, cols] = blk.astype(BF16)
            n4 = tm // CARRY_SPLIT
            n16 = n4 // CARRY_SPLIT
            for r4 in range(CARRY_SPLIT):
                p4 = slab[pl.ds(r4, n4, stride=CARRY_SPLIT), :]
                out_refs[1][0, r4, :, cols] = p4.astype(BF16)
                slab2[r4 * n4:(r4 + 1) * n4, :] = p4
                for j in range(CARRY_SPLIT):
                    out_refs[2][0, CARRY_SPLIT * j + r4, :, cols] = (
                        slab2[pl.ds(r4 * n4 + j, n16, stride=CARRY_SPLIT), :].astype(BF16))

    emit(0, wq_ref, (q1_ref, q4_ref, q16_ref), True)
    emit(1, wk_ref, (k1_ref, k4_ref, k16_ref), True)
    emit(2, wv_ref, (v1_ref, v4_ref, v16_ref), False)
    z_ref[...] = jnp.dot(h, wz_ref[...], preferred_element_type=F32).astype(BF16)
    xbc_ref[...] = jnp.dot(h, wx_ref[...], preferred_element_type=F32).astype(BF16)
    dt_ref[...] = jnp.dot(h, wdt_ref[...], preferred_element_type=F32)


def _rope_tables(seq):
    half = ROPE_DIM // 2
    inv_freq = jnp.power(ROPE_THETA, -jnp.arange(half, dtype=F32) * 2.0 / ROPE_DIM)
    ang = jnp.arange(seq, dtype=F32)[:, None] * inv_freq[None, :]
    cos, sin = jnp.cos(ang), jnp.sin(ang)
    one = jnp.ones((seq, HEAD_DIM - ROPE_DIM), F32)
    zero8 = jnp.zeros((seq, half), F32)
    zero = jnp.zeros((seq, HEAD_DIM - ROPE_DIM), F32)
    rc = jnp.concatenate([cos, cos, one], axis=1)
    rs1 = jnp.concatenate([zero8, sin, zero], axis=1)
    rs2 = jnp.concatenate([-sin, zero8, zero], axis=1)
    rep = LANES // HEAD_DIM
    return tuple(jnp.tile(t, (1, rep)) for t in (rc, rs1, rs2))


def _in_projection(x2d, norm_w, w_in, batch, seq, tm=256):
    t_rows = x2d.shape[0]
    a = ATTN_WIDTH
    wq = (w_in[:, :a] * (HEAD_DIM ** -0.5 * LOG2E)).astype(BF16)
    wk = w_in[:, a:2 * a].astype(BF16)
    wv = w_in[:, 2 * a:3 * a].astype(BF16)
    wz = w_in[:, 3 * a:3 * a + D_INNER].astype(BF16)
    o = 3 * a + D_INNER
    wx = w_in[:, o:o + XBC_WIDTH].astype(BF16)
    wdt = jnp.pad(w_in[:, o + XBC_WIDTH:], ((0, 0), (0, LANES - 2 * N_SSM_HEADS))).astype(BF16)
    rc, rs1, rs2 = _rope_tables(seq)
    nseq = seq // tm
    row = lambda width: pl.BlockSpec((tm, width), lambda i: (i, 0))
    tab = pl.BlockSpec((tm, LANES), lambda i: (i % nseq, 0))
    plane = lambda dil: pl.BlockSpec((1, dil, tm // dil, a), lambda i: (i // nseq, 0, i % nseq, 0))
    plane_shape = lambda dil: jax.ShapeDtypeStruct((batch, dil, seq // dil, a), BF16)
    d4, d16 = DILATIONS[1:]
    return pl.pallas_call(
        _inproj_kernel,
        grid=(t_rows // tm,),
        in_specs=[row(D_MODEL), _const_spec((1, D_MODEL)),
                  _const_spec((D_MODEL, a)), _const_spec((D_MODEL, a)), _const_spec((D_MODEL, a)),
                  _const_spec((D_MODEL, D_INNER)), _const_spec((D_MODEL, XBC_WIDTH)),
                  _const_spec((D_MODEL, LANES)), tab, tab, tab],
        out_specs=[row(a)] * 3 + [plane(d4)] * 3 + [plane(d16)] * 3
        + [row(D_INNER), row(XBC_WIDTH), row(LANES)],
        out_shape=[jax.ShapeDtypeStruct((t_rows, a), BF16)] * 3 + [plane_shape(d4)] * 3
        + [plane_shape(d16)] * 3
        + [jax.ShapeDtypeStruct((t_rows, D_INNER), BF16),
           jax.ShapeDtypeStruct((t_rows, XBC_WIDTH), BF16),
           jax.ShapeDtypeStruct((t_rows, LANES), F32)],
        scratch_shapes=[pltpu.VMEM((3 * a // LANES, tm, LANES), F32)] * 2,
        compiler_params=_params("parallel"),
        name="in_projection",
    )(x2d, norm_w.reshape(1, D_MODEL), wq, wk, wv, wz, wx, wdt, rc, rs1, rs2)


ATT_TQ = 128
ATT_TK = ATT_TQ + 2 * BAND_HALF
ATT_HEADS_PER_STEP = 8
ATT_QUERIES_PER_STEP = 512


def _attn_kernel(*refs, length, n_sub, first, last):
    q_ref, k_ref, v_ref = refs[:3]
    if first:
        o_ref, st_ref, s_scr, slab_ref = refs[3:]
    elif last:
        op_ref, sp_ref, o_ref, s_scr = refs[3:]
    else:
        op_ref, sp_ref, o_ref, st_ref, s_scr, slab_ref = refs[3:]
    tq, tk = ATT_TQ, ATT_TK
    n_pairs = ATT_HEADS_PER_STEP // 2
    qi = pl.program_id(3)
    lane = lax.broadcasted_iota(jnp.int32, (tq, LANES), 1)
    even = lane < HEAD_DIM
    stat_is_max = (lane & 15) < 8
    delta = (lax.broadcasted_iota(jnp.int32, (tq, tk), 1)
             - lax.broadcasted_iota(jnp.int32, (tq, tk), 0))
    nt = (((1,), (1,)), ((), ()))
    ones = jnp.ones((tk, LANES), BF16)
    sub_rows = tq // CARRY_SPLIT

    def window_start(sb):
        q0 = (qi * n_sub + sb) * tq
        return q0, pl.multiple_of(jnp.clip(q0 - BAND_HALF, 0, length - tk), BAND_HALF)

    row_max = []
    for sb in range(n_sub):
        q0, ws = window_start(sb)
        bias = jnp.where(jnp.abs(delta + (ws - q0)) <= BAND_HALF, 0.0, NEG_BIG)
        bias2 = jnp.concatenate([bias, bias], axis=0)
        for hp in range(n_pairs):
            cols = slice(hp * LANES, (hp + 1) * LANES)
            q2 = q_ref[0, 0, sb * tq:(sb + 1) * tq, cols]
            zero = jnp.zeros_like(q2)
            qq = jnp.concatenate([jnp.where(even, q2, zero), jnp.where(even, zero, q2)], axis=0)
            k2 = k_ref[0, 0, pl.ds(ws, tk), cols]
            s = lax.dot_general(qq, k2, nt, preferred_element_type=F32) + bias2
            s_scr[sb * n_pairs + hp] = s
            row_max.append(jnp.max(s, axis=-1, keepdims=True))

    for sb in range(n_sub):
        _, ws = window_start(sb)
        rows = slice(sb * tq, (sb + 1) * tq)
        out_rows = slice(sb * sub_rows, (sb + 1) * sub_rows)
        stats = jnp.zeros((tq, LANES), F32)
        for hp in range(n_pairs):
            u = sb * n_pairs + hp
            cols = slice(hp * LANES, (hp + 1) * LANES)
            v_ext = jnp.concatenate([v_ref[0, 0, pl.ds(ws, tk), cols], ones], axis=1)
            p = jnp.exp2(s_scr[u] - row_max[u]).astype(BF16)
            pv = jnp.dot(p, v_ext, preferred_element_type=F32)
            acc = jnp.where(even, pv[:tq, :LANES], pv[tq:, :LANES])
            l = jnp.where(even, pv[:tq, LANES:], pv[tq:, LANES:])
            m = jnp.where(even, row_max[u][:tq], row_max[u][tq:])
            if not first:
                sp = sp_ref[0, 0, rows, :]
                be, bo = 16 * hp, HEAD_DIM + 16 * hp
                m_prev = jnp.where(even, sp[:, be:be + 1], sp[:, bo:bo + 1])
                l_prev = jnp.where(even, sp[:, be + 8:be + 9], sp[:, bo + 8:bo + 9])
                acc_prev = op_ref[0, 0, rows, cols].astype(F32)
                m_new = jnp.maximum(m_prev, m)
                a_prev = jnp.exp2(m_prev - m_new)
                a_cur = jnp.exp2(m - m_new)
                acc = acc_prev * a_prev + acc * a_cur
                l = l_prev * a_prev + l * a_cur
                m = m_new
            if last:
                o_ref[0, 0, rows, cols] = (acc / l).astype(BF16)
            else:
                slab_ref[u] = acc
                for j in range(CARRY_SPLIT):
                    o_ref[0, j, 0, out_rows, cols] = (
                        slab_ref[u, pl.ds(j, sub_rows, stride=CARRY_SPLIT), :].astype(BF16))
                in_zone = ((lane & (HEAD_DIM - 1)) >> 4) == hp
                stats = jnp.where(in_zone, jnp.where(stat_is_max, m, l), stats)
        if not last:
            su = n_sub * n_pairs + sb
            slab_ref[su] = stats
            for j in range(CARRY_SPLIT):
                st_ref[0, j, 0, out_rows, :] = slab_ref[su, pl.ds(j, sub_rows, stride=CARRY_SPLIT), :]


def _attention_pattern(q, k, v, o_prev, st_prev, first, last):
    b, dil, length, width = q.shape
    qs = min(length, ATT_QUERIES_PER_STEP)
    n_sub = qs // ATT_TQ
    hw = ATT_HEADS_PER_STEP * HEAD_DIM
    n_hg = width // hw
    n_units = n_sub * (ATT_HEADS_PER_STEP // 2)
    blk = pl.BlockSpec((1, 1, qs, hw), lambda bi, r, g, qi: (bi, r, qi, g))
    seq = pl.BlockSpec((1, 1, length, hw), lambda bi, r, g, qi: (bi, r, 0, g))
    stat = pl.BlockSpec((1, 1, qs, LANES), lambda bi, r, g, qi: (bi, r, qi, g))
    in_specs, args = [blk, seq, seq], [q, k, v]
    if not first:
        in_specs += [blk, stat]
        args += [o_prev, st_prev]
    scratch = [pltpu.VMEM((n_units, 2 * ATT_TQ, ATT_TK), F32)]
    if last:
        out_specs = [blk]
        out_shape = [jax.ShapeDtypeStruct((b, dil, length, width), BF16)]
    else:
        cs = CARRY_SPLIT
        carry = lambda w: pl.BlockSpec((1, cs, 1, qs // cs, w), lambda bi, r, g, qi: (bi, 0, r, qi, g))
        out_specs = [carry(hw), carry(LANES)]
        out_shape = [jax.ShapeDtypeStruct((b, cs, dil, length // cs, width), BF16),
                     jax.ShapeDtypeStruct((b, cs, dil, length // cs, n_hg * LANES), F32)]
        scratch.append(pltpu.VMEM((n_units + n_sub, ATT_TQ, LANES), F32))
    outs = pl.pallas_call(
        functools.partial(_attn_kernel, length=length, n_sub=n_sub, first=first, last=last),
        grid=(b, dil, n_hg, length // qs),
        in_specs=in_specs, out_specs=out_specs, out_shape=out_shape,
        scratch_shapes=scratch,
        compiler_params=_params("parallel", "parallel", "parallel", "arbitrary"),
        name=f"attention_dil{dil}",
    )(*args)
    if last:
        return outs[0], None
    nxt = lambda t: t.reshape(b, CARRY_SPLIT * dil, length // CARRY_SPLIT, t.shape[-1])
    return nxt(outs[0]), nxt(outs[1])


def _dilated_attention(qkv_planes):
    o = st = None
    for i, (q, k, v) in enumerate(qkv_planes):
        o, st = _attention_pattern(q, k, v, o, st, i == 0, i == len(qkv_planes) - 1)
    return o


def _split_dot(v, mat, passes):
    out = None
    r = v
    for i in range(passes):
        piece = r.astype(BF16)
        term = jnp.dot(piece, mat, preferred_element_type=F32)
        out = term if out is None else out + term
        if i + 1 < passes:
            r = r - piece.astype(F32)
    return out


def _split_dot_lhs_const(mat, v):
    out = None
    r = v
    for i in range(3):
        piece = r.astype(BF16)
        term = jnp.dot(mat, piece, preferred_element_type=F32)
        out = term if out is None else out + term
        if i < 2:
            r = r - piece.astype(F32)
    return out


def _ssm_kernel(xc_ref, xp_ref, xn_ref, dt_ref, z_ref, cw_ref, cb_ref, alog_ref, dtb_ref,
                dsk_ref, nw_ref, ef_ref, eb_ref,
                y_ref, hb_ref, xs_c, bm_c, bt_c, hrun_ref, *, n_chunks):
    L = CHUNK
    step = pl.program_id(1)
    is_bwd = step < n_chunks
    c = jnp.where(is_bwd, n_chunks - 1 - step, step - n_chunks)

    @pl.when((step == 0) | (step == n_chunks))
    def _():
        hrun_ref[...] = jnp.zeros_like(hrun_ref)

    has_prev = (c > 0).astype(F32)
    has_next = (c < n_chunks - 1).astype(F32)
    n_bc = N_SSM_GROUPS * D_STATE
    gw = D_INNER // N_SSM_GROUPS

    def conv_silu(lo, hi):
        prev_row = xp_ref[0, :, lo:hi].astype(F32)[BF16_ROWS - 1:BF16_ROWS] * has_prev
        next_row = xn_ref[0, :, lo:hi].astype(F32)[0:1] * has_next
        return _silu(_centered_conv3(xc_ref[0, :, lo:hi].astype(F32), prev_row, next_row,
                                     cw_ref[:, lo:hi], cb_ref[:, lo:hi]))

    x_dt = dt_ref[0] + dtb_ref[...]
    dt = jnp.maximum(x_dt, 0.0) + jnp.log1p(jnp.exp(-jnp.abs(x_dt)))
    a = dt * (-jnp.exp(alog_ref[...]) * LOG2E)
    li = lax.broadcasted_iota(jnp.int32, (L, L), 0)
    si = lax.broadcasted_iota(jnp.int32, (L, L), 1)
    tri = (si <= li).astype(BF16)
    cs = _split_dot_lhs_const(tri, a)
    ecs = cs - a
    tot = cs[L - 1:L, :]
    ldt = jnp.log(dt) * LOG2E

    def state_update(xs, bts, weights, e_ref):
        xw = (_split_dot(weights, e_ref[...], 2) * xs).astype(BF16)
        decay = _split_dot(jnp.broadcast_to(jnp.exp2(tot), (8, LANES)), e_ref[...], 3)[0:1]
        for g in range(N_SSM_GROUPS):
            cols = slice(g * gw, (g + 1) * gw)
            s_g = jnp.dot(bts[g], xw[:, cols], preferred_element_type=F32)
            hrun_ref[:, cols] = hrun_ref[:, cols] * decay[:, cols] + s_g

    @pl.when(is_bwd)
    def _():
        hb_ref[c] = hrun_ref[...].astype(BF16)
        xs = conv_silu(0, D_INNER)
        bm = conv_silu(D_INNER, D_INNER + n_bc)
        bts = [bm[:, g * D_STATE:(g + 1) * D_STATE].T.astype(BF16) for g in range(N_SSM_GROUPS)]
        xs_c[c] = xs.astype(BF16)
        bm_c[c] = bm.astype(BF16)
        bt_c[c] = jnp.concatenate(bts, axis=1)
        state_update(xs, bts, jnp.exp2(ecs + ldt), eb_ref)

    @pl.when(jnp.logical_not(is_bwd))
    def _():
        xs_b = xs_c[c]
        xs = xs_b.astype(F32)
        bts = [bt_c[c, :, g * D_STATE:(g + 1) * D_STATE] for g in range(N_SSM_GROUPS)]
        cm = conv_silu(D_INNER + n_bc, XBC_WIDTH)
        hf_in = hrun_ref[...].astype(BF16)
        hb_in = hb_ref[c]
        scale_f = _split_dot(jnp.exp2(cs), ef_ref[...], 2)
        scale_b = _split_dot(jnp.exp2(tot - ecs), eb_ref[...], 2)
        row_f = (cs - ldt).T
        row_b = (ecs + ldt).T
        dt_t = dt.T
        lane = lax.broadcasted_iota(jnp.int32, (L, LANES), 1)
        even = lane < HEAD_DIM
        below = si < li
        diag = si == li
        hpg = N_SSM_HEADS // N_SSM_GROUPS
        y_groups = []
        for g in range(N_SSM_GROUPS):
            cols = slice(g * gw, (g + 1) * gw)
            cg = cm[:, g * D_STATE:(g + 1) * D_STATE].astype(BF16)
            bg = bm_c[c, :, g * D_STATE:(g + 1) * D_STATE]
            gmat = lax.dot_general(cg, bg, (((1,), (1,)), ((), ())), preferred_element_type=F32)
            y_off = (scale_f[:, cols] * jnp.dot(cg, hf_in[:, cols], preferred_element_type=F32)
                     + scale_b[:, cols] * jnp.dot(cg, hb_in[:, cols], preferred_element_type=F32))
            pairs = []
            for pr in range(hpg // 2):
                h0 = g * hpg + 2 * pr
                xs_pair = xs_b[:, h0 * HEAD_DIM:(h0 + 2) * HEAD_DIM]
                res = []
                for h in (h0, h0 + 1):
                    hb = N_SSM_HEADS + h
                    sel = jnp.where(below, cs[:, h:h + 1] - row_f[h:h + 1, :],
                                    row_b[hb:hb + 1, :] - ecs[:, hb:hb + 1])
                    e = jnp.exp2(sel) + jnp.where(diag, dt_t[h:h + 1, :], 0.0)
                    w = (gmat * e).astype(BF16)
                    res.append(jnp.dot(w, xs_pair, preferred_element_type=F32))
                pairs.append(jnp.where(even, res[0], res[1]))
            y_groups.append(jnp.concatenate(pairs, axis=1) + y_off)
        y = jnp.concatenate(y_groups, axis=1) + dsk_ref[...] * xs
        zf = z_ref[0].astype(F32)
        gated = y * _silu(zf)
        outs = []
        for g in range(N_SSM_GROUPS):
            gg = gated[:, g * gw:(g + 1) * gw]
            outs.append(gg * lax.rsqrt(jnp.mean(gg * gg, axis=-1, keepdims=True) + EPS))
        y_ref[0] = (jnp.concatenate(outs, axis=1) * nw_ref[...]).astype(BF16)
        state_update(xs, bts, jnp.exp2(tot - cs + ldt), ef_ref)


def _head_expanders():
    col_head = np.arange(D_INNER) // HEAD_DIM
    rows = np.arange(LANES)[:, None]
    ef = (rows == col_head[None, :]).astype(np.float32)
    eb = (rows == col_head[None, :] + N_SSM_HEADS).astype(np.float32)
    return jnp.asarray(ef, BF16), jnp.asarray(eb, BF16)


def _ssm_mixer(z, xbc, dt_raw, conv_w, conv_b, a_log_f, a_log_b, dt_bias_f, dt_bias_b, d_skip, norm_w):
    b, s, _ = z.shape
    n = s // CHUNK
    pad = LANES - 2 * N_SSM_HEADS
    alog = jnp.pad(jnp.concatenate([a_log_f, a_log_b]), (0, pad)).reshape(1, LANES)
    dtb = jnp.pad(jnp.concatenate([dt_bias_f, dt_bias_b]), (0, pad)).reshape(1, LANES)
    dsk = jnp.repeat(d_skip, HEAD_DIM).reshape(1, D_INNER)
    ef, eb = _head_expanders()
    hpc = CHUNK // BF16_ROWS
    last_halo = s // BF16_ROWS - 1

    def chunk_of(st):
        return jnp.where(st < n, n - 1 - st, st - n)

    return pl.pallas_call(
        functools.partial(_ssm_kernel, n_chunks=n),
        grid=(b, 2 * n),
        in_specs=[
            pl.BlockSpec((1, CHUNK, XBC_WIDTH), lambda bi, st: (bi, chunk_of(st), 0)),
            pl.BlockSpec((1, BF16_ROWS, XBC_WIDTH),
                         lambda bi, st: (bi, jnp.maximum(chunk_of(st) * hpc - 1, 0), 0)),
            pl.BlockSpec((1, BF16_ROWS, XBC_WIDTH),
                         lambda bi, st: (bi, jnp.minimum((chunk_of(st) + 1) * hpc, last_halo), 0)),
            pl.BlockSpec((1, CHUNK, LANES), lambda bi, st: (bi, chunk_of(st), 0)),
            pl.BlockSpec((1, CHUNK, D_INNER), lambda bi, st: (bi, jnp.maximum(st - n, 0), 0)),
            _const_spec((3, XBC_WIDTH)), _const_spec((1, XBC_WIDTH)),
            _const_spec((1, LANES)), _const_spec((1, LANES)),
            _const_spec((1, D_INNER)), _const_spec((1, D_INNER)),
            _const_spec((LANES, D_INNER)), _const_spec((LANES, D_INNER)),
        ],
        out_specs=pl.BlockSpec((1, CHUNK, D_INNER), lambda bi, st: (bi, jnp.maximum(st - n, 0), 0)),
        out_shape=jax.ShapeDtypeStruct((b, s, D_INNER), BF16),
        scratch_shapes=[pltpu.VMEM((n, D_STATE, D_INNER), BF16),
                        pltpu.VMEM((n, CHUNK, D_INNER), BF16),
                        pltpu.VMEM((n, CHUNK, N_SSM_GROUPS * D_STATE), BF16),
                        pltpu.VMEM((n, D_STATE, N_SSM_GROUPS * CHUNK), BF16),
                        pltpu.VMEM((D_STATE, D_INNER), F32)],
        compiler_params=_params("arbitrary", "arbitrary"),
        name="ssm_mixer",
    )(xbc, xbc, xbc, dt_raw, z, conv_w.T, conv_b.reshape(1, XBC_WIDTH), alog, dtb, dsk,
      norm_w.reshape(1, D_INNER), ef, eb)


def _outproj_kernel(x_ref, a_ref, s_ref, wa_ref, ws_ref, nw_ref, x1_ref, h_ref, slab_ref):
    tm = x_ref.shape[0]
    dil = a_ref.shape[1]
    n_cb = ATTN_WIDTH // LANES
    for r in range(dil):
        blk = a_ref[0, r].astype(F32)
        for cb in range(n_cb):
            slab_ref[cb, pl.ds(r, tm // dil, stride=dil), :] = blk[:, cb * LANES:(cb + 1) * LANES]
    attn = jnp.concatenate([slab_ref[cb] for cb in range(n_cb)], axis=1).astype(BF16)
    x1 = (x_ref[...] + jnp.dot(attn, wa_ref[...], preferred_element_type=F32)
          + jnp.dot(s_ref[...], ws_ref[...], preferred_element_type=F32))
    x1_ref[...] = x1
    h_ref[...] = _rms(x1, nw_ref[...]).astype(BF16)


def _out_projection(x2d, attn_planes, ssm, w_out, norm_w, seq, tm=512):
    t_rows = x2d.shape[0]
    dil = attn_planes.shape[1]
    nseq = seq // tm
    row = pl.BlockSpec((tm, D_MODEL), lambda i: (i, 0))
    planes = pl.BlockSpec((1, dil, tm // dil, ATTN_WIDTH), lambda i: (i // nseq, 0, i % nseq, 0))
    wa = w_out[:ATTN_WIDTH].astype(BF16)
    ws = w_out[ATTN_WIDTH:].astype(BF16)
    return pl.pallas_call(
        _outproj_kernel,
        grid=(t_rows // tm,),
        in_specs=[row, planes, row, _const_spec((ATTN_WIDTH, D_MODEL)), _const_spec((D_INNER, D_MODEL)),
                  _const_spec((1, D_MODEL))],
        out_specs=[row, row],
        out_shape=[jax.ShapeDtypeStruct((t_rows, D_MODEL), F32),
                   jax.ShapeDtypeStruct((t_rows, D_MODEL), BF16)],
        scratch_shapes=[pltpu.VMEM((ATTN_WIDTH // LANES, tm, LANES), F32)],
        compiler_params=_params("parallel"),
        name="out_projection",
    )(x2d, attn_planes, ssm, wa, ws, norm_w.reshape(1, D_MODEL))


FFN_UP_COLS = 1408
FFN_DOWN_COLS = 256


def _ffn_up_kernel(h_ref, w_ref, u_ref):
    h = h_ref[...]
    for j in range(2 * D_FF // FFN_UP_COLS):
        cols = slice(j * FFN_UP_COLS, (j + 1) * FFN_UP_COLS)
        u_ref[:, cols] = jnp.dot(h, w_ref[:, cols], preferred_element_type=F32).astype(BF16)


def _ffn_up(h, w_up, tm=512):
    t_rows = h.shape[0]
    return pl.pallas_call(
        _ffn_up_kernel,
        grid=(t_rows // tm,),
        in_specs=[pl.BlockSpec((tm, D_MODEL), lambda i: (i, 0)), _const_spec((D_MODEL, 2 * D_FF))],
        out_specs=pl.BlockSpec((tm, 2 * D_FF), lambda i: (i, 0)),
        out_shape=jax.ShapeDtypeStruct((t_rows, 2 * D_FF), BF16),
        compiler_params=_params("parallel"),
        name="ffn_up",
    )(h, w_up.astype(BF16))


def _ffn_down_kernel(uc_ref, up_ref, un_ref, x1_ref, cw_ref, cb_ref, wd_ref, nw_ref, o_ref, *, blocks_per_seq):
    tm = uc_ref.shape[0]
    i = pl.program_id(0)
    pos = i % blocks_per_seq
    has_prev = (pos > 0).astype(F32)
    has_next = (pos < blocks_per_seq - 1).astype(F32)

    def conv(lo):
        hi = lo + FFN_DOWN_COLS
        prev_row = up_ref[:, lo:hi].astype(F32)[BF16_ROWS - 1:BF16_ROWS] * has_prev
        next_row = un_ref[:, lo:hi].astype(F32)[0:1] * has_next
        return _centered_conv3(uc_ref[:, lo:hi].astype(F32), prev_row, next_row,
                               cw_ref[:, lo:hi], cb_ref[:, lo:hi])

    acc = x1_ref[...]
    for j in range(D_FF // FFN_DOWN_COLS):
        lo = j * FFN_DOWN_COLS
        act = (_silu(conv(lo)) * conv(D_FF + lo)).astype(BF16)
        acc = acc + jnp.dot(act, wd_ref[lo:lo + FFN_DOWN_COLS, :], preferred_element_type=F32)
    o_ref[...] = _rms(acc, nw_ref[...])


def _ffn_down(u, x1, conv_w, conv_b, w_down, norm_w, seq, tm=256):
    t_rows = u.shape[0]
    hpb = tm // BF16_ROWS
    last_halo = t_rows // BF16_ROWS - 1
    width = 2 * D_FF
    return pl.pallas_call(
        functools.partial(_ffn_down_kernel, blocks_per_seq=seq // tm),
        grid=(t_rows // tm,),
        in_specs=[
            pl.BlockSpec((tm, width), lambda i: (i, 0)),
            pl.BlockSpec((BF16_ROWS, width), lambda i: (jnp.maximum(i * hpb - 1, 0), 0)),
            pl.BlockSpec((BF16_ROWS, width), lambda i: (jnp.minimum((i + 1) * hpb, last_halo), 0)),
            pl.BlockSpec((tm, D_MODEL), lambda i: (i, 0)),
            _const_spec((3, width)), _const_spec((1, width)),
            _const_spec((D_FF, D_MODEL)), _const_spec((1, D_MODEL)),
        ],
        out_specs=pl.BlockSpec((tm, D_MODEL), lambda i: (i, 0)),
        out_shape=jax.ShapeDtypeStruct((t_rows, D_MODEL), F32),
        compiler_params=_params("parallel"),
        name="ffn_down",
    )(u, u, u, x1, conv_w.T, conv_b.reshape(1, width), w_down.astype(BF16), norm_w.reshape(1, D_MODEL))


def kernel(x, norm1_w, w_in, ssm_conv_w, ssm_conv_b, a_log_f, a_log_b, dt_bias_f, dt_bias_b, d_skip,
           ssm_norm_w, w_out, norm2_w, w_up, ffn_conv_w, ffn_conv_b, w_down, final_norm_w):
    b, s, d = x.shape
    depth = w_in.shape[0]
    x2d = x.reshape(b * s, d)
    for layer in range(depth):
        (q1, k1, v1, q4, k4, v4, q16, k16, v16, z, xbc, dt_raw) = _in_projection(
            x2d, norm1_w[layer], w_in[layer], b, s)
        nat = lambda t: t.reshape(b, 1, s, t.shape[-1])
        attn = _dilated_attention([(nat(q1), nat(k1), nat(v1)), (q4, k4, v4), (q16, k16, v16)])
        sh = lambda t: t.reshape(b, s, t.shape[-1])
        ssm = _ssm_mixer(sh(z), sh(xbc), sh(dt_raw), ssm_conv_w[layer], ssm_conv_b[layer],
                         a_log_f[layer], a_log_b[layer], dt_bias_f[layer], dt_bias_b[layer],
                         d_skip[layer], ssm_norm_w[layer])
        x1, h2 = _out_projection(x2d, attn, ssm.reshape(b * s, -1), w_out[layer], norm2_w[layer], s)
        u = _ffn_up(h2, w_up[layer])
        assert depth == 1
        x2d = _ffn_down(u, x1, ffn_conv_w[layer], ffn_conv_b[layer], w_down[layer], final_norm_w, s)
    return x2d.reshape(b, s, d)
```

```python
import functools

import numpy as np
import jax
import jax.numpy as jnp
from jax import lax
from jax.experimental import pallas as pl
from jax.experimental.pallas import tpu as pltpu

F32 = jnp.float32
BF16 = jnp.bfloat16

D_MODEL = 1024
HEAD_DIM = 64
N_ATTN_HEADS = 16
ATTN_WIDTH = N_ATTN_HEADS * HEAD_DIM
ROPE_DIM = HEAD_DIM // 4
ROPE_THETA = 500000.0
DILATIONS = (1, 4, 16)
BAND_HALF = 64
CARRY_SPLIT = 4

D_INNER = 1024
N_SSM_HEADS = 16
N_SSM_GROUPS = 4
D_STATE = 128
CHUNK = 128
XBC_WIDTH = D_INNER + 2 * N_SSM_GROUPS * D_STATE
D_FF = 2816
EPS = 1e-6

LANES = 128
BF16_ROWS = 16
VMEM_LIMIT = 56 * 1024 * 1024
NEG_BIG = -1e30
LOG2E = 1.4426950408889634


def _params(*sem):
    return pltpu.CompilerParams(dimension_semantics=sem, vmem_limit_bytes=VMEM_LIMIT)


def _const_spec(shape):
    return pl.BlockSpec(shape, lambda *_: (0,) * len(shape))


def _rms(x, w):
    return x * lax.rsqrt(jnp.mean(x * x, axis=-1, keepdims=True) + EPS) * w


def _silu(y):
    h = 0.5 * y
    return h + h * jnp.tanh(h)


def _conv3_rows(u, w, b, halo, rows):
    r = u.shape[0]
    y = pltpu.roll(u, 1, 0) * w[0:1] + u * w[1:2] + pltpu.roll(u, r - 1, 0) * w[2:3] + b
    return y[halo:halo + rows]


def _inproj_kernel(x_ref, nw_ref, wq_ref, wk_ref, wv_ref, wz_ref, rc_ref, rs1_ref, rs2_ref,
                   q1_ref, k1_ref, v1_ref, q4_ref, k4_ref, v4_ref, q16_ref, k16_ref, v16_ref,
                   z_ref, slab_ref, slab2_ref):
    tm = x_ref.shape[0]
    h = _rms(x_ref[...], nw_ref[...]).astype(BF16)
    rc, rs1, rs2 = rc_ref[...], rs1_ref[...], rs2_ref[...]
    n_cb = ATTN_WIDTH // LANES

    def emit(idx, w_ref, out_refs, rope):
        t = jnp.dot(h, w_ref[...], preferred_element_type=F32)
        for cb in range(n_cb):
            cols = slice(cb * LANES, (cb + 1) * LANES)
            blk = t[:, cols]
            if rope:
                lo = pltpu.roll(blk, ROPE_DIM // 2, 1)
                hi = pltpu.roll(blk, LANES - ROPE_DIM // 2, 1)
                blk = blk * rc + lo * rs1 + hi * rs2
            slab = slab_ref.at[idx * n_cb + cb]
            slab2 = slab2_ref.at[idx * n_cb + cb]
            slab[...] = blk
            out_refs[0][:, cols] = blk.astype(BF16)
            n4 = tm // CARRY_SPLIT
            n16 = n4 // CARRY_SPLIT
            for r4 in range(CARRY_SPLIT):
                p4 = slab[pl.ds(r4, n4, stride=CARRY_SPLIT), :]
                out_refs[1][0, r4, :, cols] = p4.astype(BF16)
                slab2[r4 * n4:(r4 + 1) * n4, :] = p4
                for j in range(CARRY_SPLIT):
                    out_refs[2][0, CARRY_SPLIT * j + r4, :, cols] = (
                        slab2[pl.ds(r4 * n4 + j, n16, stride=CARRY_SPLIT), :].astype(BF16))

    emit(0, wq_ref, (q1_ref, q4_ref, q16_ref), True)
    emit(1, wk_ref, (k1_ref, k4_ref, k16_ref), True)
    emit(2, wv_ref, (v1_ref, v4_ref, v16_ref), False)
    z_ref[...] = jnp.dot(h, wz_ref[...], preferred_element_type=F32).astype(BF16)


def _rope_tables(seq):
    half = ROPE_DIM // 2
    inv_freq = jnp.power(ROPE_THETA, -jnp.arange(half, dtype=F32) * 2.0 / ROPE_DIM)
    ang = jnp.arange(seq, dtype=F32)[:, None] * inv_freq[None, :]
    cos, sin = jnp.cos(ang), jnp.sin(ang)
    one = jnp.ones((seq, HEAD_DIM - ROPE_DIM), F32)
    zero8 = jnp.zeros((seq, half), F32)
    zero = jnp.zeros((seq, HEAD_DIM - ROPE_DIM), F32)
    rc = jnp.concatenate([cos, cos, one], axis=1)
    rs1 = jnp.concatenate([zero8, sin, zero], axis=1)
    rs2 = jnp.concatenate([-sin, zero8, zero], axis=1)
    rep = LANES // HEAD_DIM
    return tuple(jnp.tile(t, (1, rep)) for t in (rc, rs1, rs2))


def _in_projection(x2d, norm_w, w_in, batch, seq, tm=256):
    t_rows = x2d.shape[0]
    a = ATTN_WIDTH
    wq = (w_in[:, :a] * (HEAD_DIM ** -0.5 * LOG2E)).astype(BF16)
    wk = w_in[:, a:2 * a].astype(BF16)
    wv = w_in[:, 2 * a:3 * a].astype(BF16)
    wz = w_in[:, 3 * a:3 * a + D_INNER].astype(BF16)
    rc, rs1, rs2 = _rope_tables(seq)
    nseq = seq // tm
    row = lambda width: pl.BlockSpec((tm, width), lambda i: (i, 0))
    tab = pl.BlockSpec((tm, LANES), lambda i: (i % nseq, 0))
    plane = lambda dil: pl.BlockSpec((1, dil, tm // dil, a), lambda i: (i // nseq, 0, i % nseq, 0))
    plane_shape = lambda dil: jax.ShapeDtypeStruct((batch, dil, seq // dil, a), BF16)
    d4, d16 = DILATIONS[1:]
    return pl.pallas_call(
        _inproj_kernel,
        grid=(t_rows // tm,),
        in_specs=[row(D_MODEL), _const_spec((1, D_MODEL)),
                  _const_spec((D_MODEL, a)), _const_spec((D_MODEL, a)), _const_spec((D_MODEL, a)),
                  _const_spec((D_MODEL, D_INNER)), tab, tab, tab],
        out_specs=[row(a)] * 3 + [plane(d4)] * 3 + [plane(d16)] * 3 + [row(D_INNER)],
        out_shape=[jax.ShapeDtypeStruct((t_rows, a), BF16)] * 3 + [plane_shape(d4)] * 3
        + [plane_shape(d16)] * 3 + [jax.ShapeDtypeStruct((t_rows, D_INNER), BF16)],
        scratch_shapes=[pltpu.VMEM((3 * a // LANES, tm, LANES), F32)] * 2,
        compiler_params=_params("parallel"),
        name="in_projection",
    )(x2d, norm_w.reshape(1, D_MODEL), wq, wk, wv, wz, rc, rs1, rs2)


SSM_HALO = BF16_ROWS
SSM_COLS = 256


def _ssm_proj_kernel(xc_ref, xp_ref, xn_ref, nw_ref, wx_ref, wdt_ref, cw_ref, cb_ref, alog_ref, dtb_ref,
                     xa_ref, dt_ref, ldt_ref, cs_ref, lhs_scr, u_scr, *, blocks_per_seq):
    tm = xc_ref.shape[0]
    pos = pl.program_id(0) % blocks_per_seq
    nw = nw_ref[...]
    rows = tm + 2 * SSM_HALO
    h = _rms(xc_ref[...], nw).astype(BF16)
    zero_halo = jnp.zeros((SSM_HALO, D_MODEL), BF16)
    lhs_scr[0:SSM_HALO, :] = jnp.where(pos > 0, _rms(xp_ref[...], nw).astype(BF16), zero_halo)
    lhs_scr[SSM_HALO:SSM_HALO + tm, :] = h
    lhs_scr[SSM_HALO + tm:rows, :] = jnp.where(pos < blocks_per_seq - 1,
                                               _rms(xn_ref[...], nw).astype(BF16), zero_halo)
    n_chunks = XBC_WIDTH // SSM_COLS

    def chunk_cols(j):
        return pl.ds(pl.multiple_of(j * SSM_COLS, LANES), SSM_COLS)

    def project(j, slot):
        u_scr[slot] = jnp.dot(lhs_scr[...], wx_ref[:, chunk_cols(j)], preferred_element_type=F32)

    def finish(j, slot):
        y = _conv3_rows(u_scr[slot], cw_ref[:, chunk_cols(j)], cb_ref[:, chunk_cols(j)], SSM_HALO, tm)
        xa_ref[:, chunk_cols(j)] = _silu(y).astype(BF16)

    assert n_chunks % 2 == 0
    project(0, 0)

    def body(i, carry):
        j = 2 * i
        project(j + 1, 1)
        finish(j, 0)
        project(j + 2, 0)
        finish(j + 1, 1)
        return carry

    lax.fori_loop(0, n_chunks // 2 - 1, body, 0)
    project(n_chunks - 1, 1)
    finish(n_chunks - 2, 0)
    finish(n_chunks - 1, 1)

    x_dt = jnp.dot(h, wdt_ref[...], preferred_element_type=F32) + dtb_ref[...]
    dt = jnp.maximum(x_dt, 0.0) + jnp.log1p(jnp.exp(-jnp.abs(x_dt)))
    a = dt * (-jnp.exp(alog_ref[...]) * LOG2E)
    tri = (lax.broadcasted_iota(jnp.int32, (CHUNK, CHUNK), 1)
           <= lax.broadcasted_iota(jnp.int32, (CHUNK, CHUNK), 0)).astype(BF16)
    dt_ref[...] = dt
    ldt_ref[...] = jnp.log(dt) * LOG2E
    for ch in range(tm // CHUNK):
        rows_c = slice(ch * CHUNK, (ch + 1) * CHUNK)
        cs_ref[rows_c, :] = _split_dot_lhs_const(tri, a[rows_c])


def _ssm_projection(x2d, norm_w, w_in, conv_w, conv_b, a_log_f, a_log_b, dt_bias_f, dt_bias_b,
                    seq, tm=512):
    t_rows = x2d.shape[0]
    pad = LANES - 2 * N_SSM_HEADS
    alog = jnp.pad(jnp.concatenate([a_log_f, a_log_b]), (0, pad)).reshape(1, LANES)
    dtb = jnp.pad(jnp.concatenate([dt_bias_f, dt_bias_b]), (0, pad)).reshape(1, LANES)
    o = 3 * ATTN_WIDTH + D_INNER
    wx = w_in[:, o:o + XBC_WIDTH].astype(BF16)
    wdt = jnp.pad(w_in[:, o + XBC_WIDTH:], ((0, 0), (0, pad))).astype(BF16)
    hpb = tm // SSM_HALO
    last_halo = t_rows // SSM_HALO - 1
    row = lambda width: pl.BlockSpec((tm, width), lambda i: (i, 0))
    return pl.pallas_call(
        functools.partial(_ssm_proj_kernel, blocks_per_seq=seq // tm),
        grid=(t_rows // tm,),
        in_specs=[row(D_MODEL),
                  pl.BlockSpec((SSM_HALO, D_MODEL), lambda i: (jnp.maximum(i * hpb - 1, 0), 0)),
                  pl.BlockSpec((SSM_HALO, D_MODEL), lambda i: (jnp.minimum((i + 1) * hpb, last_halo), 0)),
                  _const_spec((1, D_MODEL)), _const_spec((D_MODEL, XBC_WIDTH)),
                  _const_spec((D_MODEL, LANES)),
                  _const_spec((3, XBC_WIDTH)), _const_spec((1, XBC_WIDTH)),
                  _const_spec((1, LANES)), _const_spec((1, LANES))],
        out_specs=[row(XBC_WIDTH), row(LANES), row(LANES), row(LANES)],
        out_shape=[jax.ShapeDtypeStruct((t_rows, XBC_WIDTH), BF16)]
        + [jax.ShapeDtypeStruct((t_rows, LANES), F32)] * 3,
        scratch_shapes=[pltpu.VMEM((tm + 2 * SSM_HALO, D_MODEL), BF16),
                        pltpu.VMEM((2, tm + 2 * SSM_HALO, SSM_COLS), F32)],
        compiler_params=_params("parallel"),
        name="ssm_projection",
    )(x2d, x2d, x2d, norm_w.reshape(1, D_MODEL), wx, wdt, conv_w.T, conv_b.reshape(1, XBC_WIDTH),
      alog, dtb)


ATT_TQ = 128
ATT_TK = ATT_TQ + 2 * BAND_HALF
ATT_HEADS_PER_STEP = 8
ATT_QUERIES_PER_STEP = 512


def _attn_kernel(*refs, length, n_sub, first, last):
    q_ref, k_ref, v_ref = refs[:3]
    if first:
        o_ref, st_ref, s_scr, slab_ref = refs[3:]
    elif last:
        op_ref, sp_ref, o_ref, s_scr = refs[3:]
    else:
        op_ref, sp_ref, o_ref, st_ref, s_scr, slab_ref = refs[3:]
    tq, tk = ATT_TQ, ATT_TK
    n_pairs = ATT_HEADS_PER_STEP // 2
    qi = pl.program_id(3)
    lane = lax.broadcasted_iota(jnp.int32, (tq, LANES), 1)
    even = lane < HEAD_DIM
    stat_is_max = (lane & 15) < 8
    delta = (lax.broadcasted_iota(jnp.int32, (tq, tk), 1)
             - lax.broadcasted_iota(jnp.int32, (tq, tk), 0))
    nt = (((1,), (1,)), ((), ()))
    ones = jnp.ones((tk, LANES), BF16)
    sub_rows = tq // CARRY_SPLIT

    def window_start(sb):
        q0 = (qi * n_sub + sb) * tq
        return q0, pl.multiple_of(jnp.clip(q0 - BAND_HALF, 0, length - tk), BAND_HALF)

    row_max = []
    for sb in range(n_sub):
        q0, ws = window_start(sb)
        bias = jnp.where(jnp.abs(delta + (ws - q0)) <= BAND_HALF, 0.0, NEG_BIG)
        bias2 = jnp.concatenate([bias, bias], axis=0)
        for hp in range(n_pairs):
            cols = slice(hp * LANES, (hp + 1) * LANES)
            q2 = q_ref[0, 0, sb * tq:(sb + 1) * tq, cols]
            zero = jnp.zeros_like(q2)
            qq = jnp.concatenate([jnp.where(even, q2, zero), jnp.where(even, zero, q2)], axis=0)
            k2 = k_ref[0, 0, pl.ds(ws, tk), cols]
            s = lax.dot_general(qq, k2, nt, preferred_element_type=F32) + bias2
            s_scr[sb * n_pairs + hp] = s
            row_max.append(jnp.max(s, axis=-1, keepdims=True))

    for sb in range(n_sub):
        _, ws = window_start(sb)
        rows = slice(sb * tq, (sb + 1) * tq)
        out_rows = slice(sb * sub_rows, (sb + 1) * sub_rows)
        stats = jnp.zeros((tq, LANES), F32)
        for hp in range(n_pairs):
            u = sb * n_pairs + hp
            cols = slice(hp * LANES, (hp + 1) * LANES)
            v_ext = jnp.concatenate([v_ref[0, 0, pl.ds(ws, tk), cols], ones], axis=1)
            p = jnp.exp2(s_scr[u] - row_max[u]).astype(BF16)
            pv = jnp.dot(p, v_ext, preferred_element_type=F32)
            acc = jnp.where(even, pv[:tq, :LANES], pv[tq:, :LANES])
            l = jnp.where(even, pv[:tq, LANES:], pv[tq:, LANES:])
            m = jnp.where(even, row_max[u][:tq], row_max[u][tq:])
            if not first:
                sp = sp_ref[0, 0, rows, :]
                be, bo = 16 * hp, HEAD_DIM + 16 * hp
                m_prev = jnp.where(even, sp[:, be:be + 1], sp[:, bo:bo + 1])
                l_prev = jnp.where(even, sp[:, be + 8:be + 9], sp[:, bo + 8:bo + 9])
                acc_prev = op_ref[0, 0, rows, cols].astype(F32)
                m_new = jnp.maximum(m_prev, m)
                a_prev = jnp.exp2(m_prev - m_new)
                a_cur = jnp.exp2(m - m_new)
                acc = acc_prev * a_prev + acc * a_cur
                l = l_prev * a_prev + l * a_cur
                m = m_new
            if last:
                o_ref[0, 0, rows, cols] = (acc / l).astype(BF16)
            else:
                slab_ref[u] = acc
                for j in range(CARRY_SPLIT):
                    o_ref[0, j, 0, out_rows, cols] = (
                        slab_ref[u, pl.ds(j, sub_rows, stride=CARRY_SPLIT), :].astype(BF16))
                in_zone = ((lane & (HEAD_DIM - 1)) >> 4) == hp
                stats = jnp.where(in_zone, jnp.where(stat_is_max, m, l), stats)
        if not last:
            su = n_sub * n_pairs + sb
            slab_ref[su] = stats
            for j in range(CARRY_SPLIT):
                st_ref[0, j, 0, out_rows, :] = slab_ref[su, pl.ds(j, sub_rows, stride=CARRY_SPLIT), :]


def _attention_pattern(q, k, v, o_prev, st_prev, first, last):
    b, dil, length, width = q.shape
    qs = min(length, ATT_QUERIES_PER_STEP)
    n_sub = qs // ATT_TQ
    hw = ATT_HEADS_PER_STEP * HEAD_DIM
    n_hg = width // hw
    n_units = n_sub * (ATT_HEADS_PER_STEP // 2)
    blk = pl.BlockSpec((1, 1, qs, hw), lambda bi, r, g, qi: (bi, r, qi, g))
    seq = pl.BlockSpec((1, 1, length, hw), lambda bi, r, g, qi: (bi, r, 0, g))
    stat = pl.BlockSpec((1, 1, qs, LANES), lambda bi, r, g, qi: (bi, r, qi, g))
    in_specs, args = [blk, seq, seq], [q, k, v]
    if not first:
        in_specs += [blk, stat]
        args += [o_prev, st_prev]
    scratch = [pltpu.VMEM((n_units, 2 * ATT_TQ, ATT_TK), F32)]
    if last:
        out_specs = [blk]
        out_shape = [jax.ShapeDtypeStruct((b, dil, length, width), BF16)]
    else:
        cs = CARRY_SPLIT
        carry = lambda w: pl.BlockSpec((1, cs, 1, qs // cs, w), lambda bi, r, g, qi: (bi, 0, r, qi, g))
        out_specs = [carry(hw), carry(LANES)]
        out_shape = [jax.ShapeDtypeStruct((b, cs, dil, length // cs, width), BF16),
                     jax.ShapeDtypeStruct((b, cs, dil, length // cs, n_hg * LANES), F32)]
        scratch.append(pltpu.VMEM((n_units + n_sub, ATT_TQ, LANES), F32))
    outs = pl.pallas_call(
        functools.partial(_attn_kernel, length=length, n_sub=n_sub, first=first, last=last),
        grid=(b, dil, n_hg, length // qs),
        in_specs=in_specs, out_specs=out_specs, out_shape=out_shape,
        scratch_shapes=scratch,
        compiler_params=_params("parallel", "parallel", "parallel", "arbitrary"),
        name=f"attention_dil{dil}",
    )(*args)
    if last:
        return outs[0], None
    nxt = lambda t: t.reshape(b, CARRY_SPLIT * dil, length // CARRY_SPLIT, t.shape[-1])
    return nxt(outs[0]), nxt(outs[1])


def _dilated_attention(qkv_planes):
    o = st = None
    for i, (q, k, v) in enumerate(qkv_planes):
        o, st = _attention_pattern(q, k, v, o, st, i == 0, i == len(qkv_planes) - 1)
    return o


def _split_dot(v, mat, passes):
    out = None
    r = v
    for i in range(passes):
        piece = r.astype(BF16)
        term = jnp.dot(piece, mat, preferred_element_type=F32)
        out = term if out is None else out + term
        if i + 1 < passes:
            r = r - piece.astype(F32)
    return out


def _split_dot_lhs_const(mat, v):
    out = None
    r = v
    for i in range(3):
        piece = r.astype(BF16)
        term = jnp.dot(mat, piece, preferred_element_type=F32)
        out = term if out is None else out + term
        if i < 2:
            r = r - piece.astype(F32)
    return out


def _ssm_kernel(xa_ref, dt_ref, ldt_ref, cs_ref, z_ref, alog_ref, dsk_ref, nw_ref, ef_ref, eb_ref,
                y_ref, hb_ref, bt_c, hrun_ref, *, n_chunks):
    L = CHUNK
    step = pl.program_id(1)
    is_bwd = step < n_chunks
    c = jnp.where(is_bwd, n_chunks - 1 - step, step - n_chunks)

    @pl.when((step == 0) | (step == n_chunks))
    def _():
        hrun_ref[...] = jnp.zeros_like(hrun_ref)

    n_bc = N_SSM_GROUPS * D_STATE
    gw = D_INNER // N_SSM_GROUPS

    dt = dt_ref[0]
    ldt = ldt_ref[0]
    cs = cs_ref[0]
    ecs = cs - dt * (-jnp.exp(alog_ref[...]) * LOG2E)
    tot = cs[L - 1:L, :]
    li = lax.broadcasted_iota(jnp.int32, (L, L), 0)
    si = lax.broadcasted_iota(jnp.int32, (L, L), 1)

    def state_update(xs, bts, weights, e_ref):
        xw = (_split_dot(weights, e_ref[...], 1) * xs).astype(BF16)
        decay = _split_dot(jnp.broadcast_to(jnp.exp2(tot), (8, LANES)), e_ref[...], 3)[0:1]
        for g in range(N_SSM_GROUPS):
            cols = slice(g * gw, (g + 1) * gw)
            s_g = jnp.dot(bts[g], xw[:, cols], preferred_element_type=F32)
            hrun_ref[:, cols] = hrun_ref[:, cols] * decay[:, cols] + s_g

    @pl.when(is_bwd)
    def _():
        hb_ref[c] = hrun_ref[...].astype(BF16)
        xs = xa_ref[0, :, 0:D_INNER].astype(F32)
        bm = xa_ref[0, :, D_INNER:D_INNER + n_bc].astype(F32)
        bts = [bm[:, g * D_STATE:(g + 1) * D_STATE].T.astype(BF16) for g in range(N_SSM_GROUPS)]
        bt_c[c] = jnp.concatenate(bts, axis=1)
        state_update(xs, bts, jnp.exp2(ecs + ldt), eb_ref)

    @pl.when(jnp.logical_not(is_bwd))
    def _():
        xs_b = xa_ref[0, :, 0:D_INNER]
        xs = xs_b.astype(F32)
        bts = [bt_c[c, :, g * D_STATE:(g + 1) * D_STATE] for g in range(N_SSM_GROUPS)]
        hf_in = hrun_ref[...].astype(BF16)
        hb_in = hb_ref[c]
        scale_f = _split_dot(jnp.exp2(cs), ef_ref[...], 1)
        scale_b = _split_dot(jnp.exp2(tot - ecs), eb_ref[...], 1)
        row_f = (cs - ldt).T
        row_b = (ecs + ldt).T
        dt_t = dt.T
        lane = lax.broadcasted_iota(jnp.int32, (L, LANES), 1)
        even = lane < HEAD_DIM
        below = si < li
        diag = si == li
        hpg = N_SSM_HEADS // N_SSM_GROUPS
        y_groups = []
        for g in range(N_SSM_GROUPS):
            cols = slice(g * gw, (g + 1) * gw)
            cg = xa_ref[0, :, D_INNER + n_bc + g * D_STATE:D_INNER + n_bc + (g + 1) * D_STATE]
            bg = xa_ref[0, :, D_INNER + g * D_STATE:D_INNER + (g + 1) * D_STATE]
            gmat = lax.dot_general(cg, bg, (((1,), (1,)), ((), ())), preferred_element_type=F32)
            y_off = (scale_f[:, cols] * jnp.dot(cg, hf_in[:, cols], preferred_element_type=F32)
                     + scale_b[:, cols] * jnp.dot(cg, hb_in[:, cols], preferred_element_type=F32))
            pairs = []
            for pr in range(hpg // 2):
                h0 = g * hpg + 2 * pr
                xs_pair = xs_b[:, h0 * HEAD_DIM:(h0 + 2) * HEAD_DIM]
                res = []
                for h in (h0, h0 + 1):
                    hb = N_SSM_HEADS + h
                    sel = jnp.where(below, cs[:, h:h + 1] - row_f[h:h + 1, :],
                                    row_b[hb:hb + 1, :] - ecs[:, hb:hb + 1])
                    e = jnp.exp2(sel) + jnp.where(diag, dt_t[h:h + 1, :], 0.0)
                    w = (gmat * e).astype(BF16)
                    res.append(jnp.dot(w, xs_pair, preferred_element_type=F32))
                pairs.append(jnp.where(even, res[0], res[1]))
            y_groups.append(jnp.concatenate(pairs, axis=1) + y_off)
        y = jnp.concatenate(y_groups, axis=1) + dsk_ref[...] * xs
        zf = z_ref[0].astype(F32)
        gated = y * _silu(zf)
        outs = []
        for g in range(N_SSM_GROUPS):
            gg = gated[:, g * gw:(g + 1) * gw]
            outs.append(gg * lax.rsqrt(jnp.mean(gg * gg, axis=-1, keepdims=True) + EPS))
        y_ref[0] = (jnp.concatenate(outs, axis=1) * nw_ref[...]).astype(BF16)
        state_update(xs, bts, jnp.exp2(tot - cs + ldt), ef_ref)


def _head_expanders():
    col_head = np.arange(D_INNER) // HEAD_DIM
    rows = np.arange(LANES)[:, None]
    ef = (rows == col_head[None, :]).astype(np.float32)
    eb = (rows == col_head[None, :] + N_SSM_HEADS).astype(np.float32)
    return jnp.asarray(ef, BF16), jnp.asarray(eb, BF16)


def _ssm_mixer(z, xa, dt, ldt, cs, a_log_f, a_log_b, d_skip, norm_w):
    b, s, _ = z.shape
    n = s // CHUNK
    pad = LANES - 2 * N_SSM_HEADS
    alog = jnp.pad(jnp.concatenate([a_log_f, a_log_b]), (0, pad)).reshape(1, LANES)
    dsk = jnp.repeat(d_skip, HEAD_DIM).reshape(1, D_INNER)
    ef, eb = _head_expanders()

    def chunk_of(st):
        return jnp.where(st < n, n - 1 - st, st - n)

    per_chunk = pl.BlockSpec((1, CHUNK, LANES), lambda bi, st: (bi, chunk_of(st), 0))
    fwd_only = pl.BlockSpec((1, CHUNK, D_INNER), lambda bi, st: (bi, jnp.maximum(st - n, 0), 0))
    return pl.pallas_call(
        functools.partial(_ssm_kernel, n_chunks=n),
        grid=(b, 2 * n),
        in_specs=[
            pl.BlockSpec((1, CHUNK, XBC_WIDTH), lambda bi, st: (bi, chunk_of(st), 0)),
            per_chunk, per_chunk, per_chunk, fwd_only,
            _const_spec((1, LANES)), _const_spec((1, D_INNER)), _const_spec((1, D_INNER)),
            _const_spec((LANES, D_INNER)), _const_spec((LANES, D_INNER)),
        ],
        out_specs=fwd_only,
        out_shape=jax.ShapeDtypeStruct((b, s, D_INNER), BF16),
        scratch_shapes=[pltpu.VMEM((n, D_STATE, D_INNER), BF16),
                        pltpu.VMEM((n, D_STATE, N_SSM_GROUPS * CHUNK), BF16),
                        pltpu.VMEM((D_STATE, D_INNER), F32)],
        compiler_params=_params("arbitrary", "arbitrary"),
        name="ssm_mixer",
    )(xa, dt, ldt, cs, z, alog, dsk, norm_w.reshape(1, D_INNER), ef, eb)


def _outproj_kernel(x_ref, a_ref, s_ref, wa_ref, ws_ref, nw_ref, x1_ref, h_ref, slab_ref):
    tm = x_ref.shape[0]
    dil = a_ref.shape[1]
    n_cb = ATTN_WIDTH // LANES
    for r in range(dil):
        blk = a_ref[0, r].astype(F32)
        for cb in range(n_cb):
            slab_ref[cb, pl.ds(r, tm // dil, stride=dil), :] = blk[:, cb * LANES:(cb + 1) * LANES]
    attn = jnp.concatenate([slab_ref[cb] for cb in range(n_cb)], axis=1).astype(BF16)
    x1 = (x_ref[...] + jnp.dot(attn, wa_ref[...], preferred_element_type=F32)
          + jnp.dot(s_ref[...], ws_ref[...], preferred_element_type=F32))
    x1_ref[...] = x1
    h_ref[...] = _rms(x1, nw_ref[...]).astype(BF16)


def _out_projection(x2d, attn_planes, ssm, w_out, norm_w, seq, tm=512):
    t_rows = x2d.shape[0]
    dil = attn_planes.shape[1]
    nseq = seq // tm
    row = pl.BlockSpec((tm, D_MODEL), lambda i: (i, 0))
    planes = pl.BlockSpec((1, dil, tm // dil, ATTN_WIDTH), lambda i: (i // nseq, 0, i % nseq, 0))
    wa = w_out[:ATTN_WIDTH].astype(BF16)
    ws = w_out[ATTN_WIDTH:].astype(BF16)
    return pl.pallas_call(
        _outproj_kernel,
        grid=(t_rows // tm,),
        in_specs=[row, planes, row, _const_spec((ATTN_WIDTH, D_MODEL)), _const_spec((D_INNER, D_MODEL)),
                  _const_spec((1, D_MODEL))],
        out_specs=[row, row],
        out_shape=[jax.ShapeDtypeStruct((t_rows, D_MODEL), F32),
                   jax.ShapeDtypeStruct((t_rows, D_MODEL), BF16)],
        scratch_shapes=[pltpu.VMEM((ATTN_WIDTH // LANES, tm, LANES), F32)],
        compiler_params=_params("parallel"),
        name="out_projection",
    )(x2d, attn_planes, ssm, wa, ws, norm_w.reshape(1, D_MODEL))


FFN_COLS = 256
FFN_HALO = BF16_ROWS


def _ffn_up_kernel(hc_ref, hp_ref, hn_ref, w_ref, cw_ref, cb_ref, act_ref, lhs_scr, u_scr, *,
                   blocks_per_seq):
    tm = hc_ref.shape[0]
    pos = pl.program_id(0) % blocks_per_seq
    rows = tm + 2 * FFN_HALO
    lhs_scr[0:FFN_HALO, :] = jnp.where(pos > 0, hp_ref[...], jnp.zeros_like(hp_ref))
    lhs_scr[FFN_HALO:FFN_HALO + tm, :] = hc_ref[...]
    lhs_scr[FFN_HALO + tm:rows, :] = jnp.where(pos < blocks_per_seq - 1, hn_ref[...],
                                               jnp.zeros_like(hn_ref))

    n_chunks = D_FF // FFN_COLS

    def chunk_cols(j, half):
        return pl.ds(pl.multiple_of(j * FFN_COLS + half * D_FF, LANES), FFN_COLS)

    def project(j, slot):
        for half in range(2):
            u_scr[2 * slot + half] = jnp.dot(lhs_scr[...], w_ref[:, chunk_cols(j, half)],
                                             preferred_element_type=F32)

    def conv(j, slot, half):
        return _conv3_rows(u_scr[2 * slot + half], cw_ref[:, chunk_cols(j, half)],
                           cb_ref[:, chunk_cols(j, half)], FFN_HALO, tm)

    def finish(j, slot):
        act_ref[:, chunk_cols(j, 0)] = (_silu(conv(j, slot, 0)) * conv(j, slot, 1)).astype(BF16)

    assert n_chunks % 2 == 1
    project(0, 0)

    def body(i, carry):
        j = 2 * i
        project(j + 1, 1)
        finish(j, 0)
        project(j + 2, 0)
        finish(j + 1, 1)
        return carry

    lax.fori_loop(0, n_chunks // 2, body, 0)
    finish(n_chunks - 1, 0)


def _ffn_up(h, w_up, conv_w, conv_b, seq, tm=512):
    t_rows = h.shape[0]
    hpb = tm // FFN_HALO
    last_halo = t_rows // FFN_HALO - 1
    width = 2 * D_FF
    return pl.pallas_call(
        functools.partial(_ffn_up_kernel, blocks_per_seq=seq // tm),
        grid=(t_rows // tm,),
        in_specs=[
            pl.BlockSpec((tm, D_MODEL), lambda i: (i, 0)),
            pl.BlockSpec((FFN_HALO, D_MODEL), lambda i: (jnp.maximum(i * hpb - 1, 0), 0)),
            pl.BlockSpec((FFN_HALO, D_MODEL), lambda i: (jnp.minimum((i + 1) * hpb, last_halo), 0)),
            _const_spec((D_MODEL, width)), _const_spec((3, width)), _const_spec((1, width)),
        ],
        out_specs=pl.BlockSpec((tm, D_FF), lambda i: (i, 0)),
        out_shape=jax.ShapeDtypeStruct((t_rows, D_FF), BF16),
        scratch_shapes=[pltpu.VMEM((tm + 2 * FFN_HALO, D_MODEL), BF16),
                        pltpu.VMEM((4, tm + 2 * FFN_HALO, FFN_COLS), F32)],
        compiler_params=_params("parallel"),
        name="ffn_up",
    )(h, h, h, w_up.astype(BF16), conv_w.T, conv_b.reshape(1, width))


def _ffn_down_kernel(a_ref, x1_ref, wd_ref, nw_ref, o_ref):
    acc = x1_ref[...] + jnp.dot(a_ref[...], wd_ref[...], preferred_element_type=F32)
    o_ref[...] = _rms(acc, nw_ref[...])


def _ffn_down(act, x1, w_down, norm_w, tm=512):
    t_rows = act.shape[0]
    return pl.pallas_call(
        _ffn_down_kernel,
        grid=(t_rows // tm,),
        in_specs=[pl.BlockSpec((tm, D_FF), lambda i: (i, 0)), pl.BlockSpec((tm, D_MODEL), lambda i: (i, 0)),
                  _const_spec((D_FF, D_MODEL)), _const_spec((1, D_MODEL))],
        out_specs=pl.BlockSpec((tm, D_MODEL), lambda i: (i, 0)),
        out_shape=jax.ShapeDtypeStruct((t_rows, D_MODEL), F32),
        compiler_params=_params("parallel"),
        name="ffn_down",
    )(act, x1, w_down.astype(BF16), norm_w.reshape(1, D_MODEL))


def kernel(x, norm1_w, w_in, ssm_conv_w, ssm_conv_b, a_log_f, a_log_b, dt_bias_f, dt_bias_b, d_skip,
           ssm_norm_w, w_out, norm2_w, w_up, ffn_conv_w, ffn_conv_b, w_down, final_norm_w):
    b, s, d = x.shape
    depth = w_in.shape[0]
    x2d = x.reshape(b * s, d)
    for layer in range(depth):
        (q1, k1, v1, q4, k4, v4, q16, k16, v16, z) = _in_projection(
            x2d, norm1_w[layer], w_in[layer], b, s)
        xa, dt, ldt, cs = _ssm_projection(
            x2d, norm1_w[layer], w_in[layer], ssm_conv_w[layer], ssm_conv_b[layer],
            a_log_f[layer], a_log_b[layer], dt_bias_f[layer], dt_bias_b[layer], s)
        nat = lambda t: t.reshape(b, 1, s, t.shape[-1])
        attn = _dilated_attention([(nat(q1), nat(k1), nat(v1)), (q4, k4, v4), (q16, k16, v16)])
        sh = lambda t: t.reshape(b, s, t.shape[-1])
        ssm = _ssm_mixer(sh(z), sh(xa), sh(dt), sh(ldt), sh(cs), a_log_f[layer], a_log_b[layer],
                         d_skip[layer], ssm_norm_w[layer])
        x1, h2 = _out_projection(x2d, attn, ssm.reshape(b * s, -1), w_out[layer], norm2_w[layer], s)
        act = _ffn_up(h2, w_up[layer], ffn_conv_w[layer], ffn_conv_b[layer], s)
        assert depth == 1
        x2d = _ffn_down(act, x1, w_down[layer], final_norm_w)
    return x2d.reshape(b, s, d)
```

```python
import functools

import numpy as np
import jax
import jax.numpy as jnp
from jax import lax
from jax.experimental import pallas as pl
from jax.experimental.pallas import tpu as pltpu

F32 = jnp.float32
BF16 = jnp.bfloat16

D_MODEL = 1024
HEAD_DIM = 64
N_ATTN_HEADS = 16
ATTN_WIDTH = N_ATTN_HEADS * HEAD_DIM
ROPE_DIM = HEAD_DIM // 4
ROPE_THETA = 500000.0
DILATIONS = (1, 4, 16)
BAND_HALF = 64
CARRY_SPLIT = 4

D_INNER = 1024
N_SSM_HEADS = 16
N_SSM_GROUPS = 4
D_STATE = 128
CHUNK = 128
XBC_WIDTH = D_INNER + 2 * N_SSM_GROUPS * D_STATE
D_FF = 2816
EPS = 1e-6

LANES = 128
BF16_ROWS = 16
VMEM_LIMIT = 56 * 1024 * 1024
NEG_BIG = -1e30
LOG2E = 1.4426950408889634


def _params(*sem):
    return pltpu.CompilerParams(dimension_semantics=sem, vmem_limit_bytes=VMEM_LIMIT)


def _const_spec(shape):
    return pl.BlockSpec(shape, lambda *_: (0,) * len(shape))


def _rms(x, w):
    return x * lax.rsqrt(jnp.mean(x * x, axis=-1, keepdims=True) + EPS) * w


def _silu(y):
    h = 0.5 * y
    return h + h * jnp.tanh(h)


def _conv3_rows(u, w, b, halo, rows):
    r = u.shape[0]
    y = pltpu.roll(u, 1, 0) * w[0:1] + u * w[1:2] + pltpu.roll(u, r - 1, 0) * w[2:3] + b
    return y[halo:halo + rows]


def _software_pipeline(n_chunks, project, finish, depth, slots, rolled):
    for k in range(min(depth, n_chunks)):
        project(k, k % slots)
    n_rolled = max(n_chunks - depth, 0) // slots if rolled else 0

    def body(i, carry):
        k0 = i * slots
        for s in range(slots):
            project(k0 + s + depth, (s + depth) % slots)
            finish(k0 + s, s)
        return carry

    if n_rolled:
        lax.fori_loop(0, n_rolled, body, 0)
    for k in range(n_rolled * slots, n_chunks):
        if k + depth < n_chunks:
            project(k + depth, (k + depth) % slots)
        finish(k, k % slots)


def _inproj_kernel(x_ref, nw_ref, wq_ref, wk_ref, wv_ref, wz_ref, rc_ref, rs1_ref, rs2_ref,
                   q1_ref, k1_ref, v1_ref, q4_ref, k4_ref, v4_ref, q16_ref, k16_ref, v16_ref,
                   z_ref, slab_ref, slab2_ref):
    tm = x_ref.shape[0]
    h = _rms(x_ref[...], nw_ref[...]).astype(BF16)
    rc, rs1, rs2 = rc_ref[...], rs1_ref[...], rs2_ref[...]
    n_cb = ATTN_WIDTH // LANES

    def emit(idx, w_ref, out_refs, rope):
        t = jnp.dot(h, w_ref[...], preferred_element_type=F32)
        for cb in range(n_cb):
            cols = slice(cb * LANES, (cb + 1) * LANES)
            blk = t[:, cols]
            if rope:
                lo = pltpu.roll(blk, ROPE_DIM // 2, 1)
                hi = pltpu.roll(blk, LANES - ROPE_DIM // 2, 1)
                blk = blk * rc + lo * rs1 + hi * rs2
            slab = slab_ref.at[idx * n_cb + cb]
            slab2 = slab2_ref.at[idx * n_cb + cb]
            slab[...] = blk
            out_refs[0][:, cols] = blk.astype(BF16)
            n4 = tm // CARRY_SPLIT
            n16 = n4 // CARRY_SPLIT
            for r4 in range(CARRY_SPLIT):
                p4 = slab[pl.ds(r4, n4, stride=CARRY_SPLIT), :]
                out_refs[1][0, r4, :, cols] = p4.astype(BF16)
                slab2[r4 * n4:(r4 + 1) * n4, :] = p4
                for j in range(CARRY_SPLIT):
                    out_refs[2][0, CARRY_SPLIT * j + r4, :, cols] = (
                        slab2[pl.ds(r4 * n4 + j, n16, stride=CARRY_SPLIT), :].astype(BF16))

    emit(0, wq_ref, (q1_ref, q4_ref, q16_ref), True)
    emit(1, wk_ref, (k1_ref, k4_ref, k16_ref), True)
    emit(2, wv_ref, (v1_ref, v4_ref, v16_ref), False)
    z_ref[...] = jnp.dot(h, wz_ref[...], preferred_element_type=F32).astype(BF16)


def _rope_tables(seq):
    half = ROPE_DIM // 2
    inv_freq = jnp.power(ROPE_THETA, -jnp.arange(half, dtype=F32) * 2.0 / ROPE_DIM)
    ang = jnp.arange(seq, dtype=F32)[:, None] * inv_freq[None, :]
    cos, sin = jnp.cos(ang), jnp.sin(ang)
    one = jnp.ones((seq, HEAD_DIM - ROPE_DIM), F32)
    zero8 = jnp.zeros((seq, half), F32)
    zero = jnp.zeros((seq, HEAD_DIM - ROPE_DIM), F32)
    rc = jnp.concatenate([cos, cos, one], axis=1)
    rs1 = jnp.concatenate([zero8, sin, zero], axis=1)
    rs2 = jnp.concatenate([-sin, zero8, zero], axis=1)
    rep = LANES // HEAD_DIM
    return tuple(jnp.tile(t, (1, rep)) for t in (rc, rs1, rs2))


def _in_projection(x2d, norm_w, w_in, batch, seq, tm=256):
    t_rows = x2d.shape[0]
    a = ATTN_WIDTH
    wq = (w_in[:, :a] * (HEAD_DIM ** -0.5 * LOG2E)).astype(BF16)
    wk = w_in[:, a:2 * a].astype(BF16)
    wv = w_in[:, 2 * a:3 * a].astype(BF16)
    wz = w_in[:, 3 * a:3 * a + D_INNER].astype(BF16)
    rc, rs1, rs2 = _rope_tables(seq)
    nseq = seq // tm
    row = lambda width: pl.BlockSpec((tm, width), lambda i: (i, 0))
    tab = pl.BlockSpec((tm, LANES), lambda i: (i % nseq, 0))
    plane = lambda dil: pl.BlockSpec((1, dil, tm // dil, a), lambda i: (i // nseq, 0, i % nseq, 0))
    plane_shape = lambda dil: jax.ShapeDtypeStruct((batch, dil, seq // dil, a), BF16)
    d4, d16 = DILATIONS[1:]
    return pl.pallas_call(
        _inproj_kernel,
        grid=(t_rows // tm,),
        in_specs=[row(D_MODEL), _const_spec((1, D_MODEL)),
                  _const_spec((D_MODEL, a)), _const_spec((D_MODEL, a)), _const_spec((D_MODEL, a)),
                  _const_spec((D_MODEL, D_INNER)), tab, tab, tab],
        out_specs=[row(a)] * 3 + [plane(d4)] * 3 + [plane(d16)] * 3 + [row(D_INNER)],
        out_shape=[jax.ShapeDtypeStruct((t_rows, a), BF16)] * 3 + [plane_shape(d4)] * 3
        + [plane_shape(d16)] * 3 + [jax.ShapeDtypeStruct((t_rows, D_INNER), BF16)],
        scratch_shapes=[pltpu.VMEM((3 * a // LANES, tm, LANES), F32)] * 2,
        compiler_params=_params("parallel"),
        name="in_projection",
    )(x2d, norm_w.reshape(1, D_MODEL), wq, wk, wv, wz, rc, rs1, rs2)


SSM_HALO = BF16_ROWS
SSM_COLS = 256
SSM_SLOTS = 4


def _ssm_proj_kernel(xc_ref, xp_ref, xn_ref, nw_ref, wx_ref, wdt_ref, cw_ref, cb_ref, alog_ref, dtb_ref,
                     xa_ref, dt_ref, ldt_ref, cs_ref, lhs_scr, u_scr, *, blocks_per_seq):
    tm = xc_ref.shape[0]
    pos = pl.program_id(0) % blocks_per_seq
    nw = nw_ref[...]
    rows = tm + 2 * SSM_HALO
    h = _rms(xc_ref[...], nw).astype(BF16)
    zero_halo = jnp.zeros((SSM_HALO, D_MODEL), BF16)
    lhs_scr[0:SSM_HALO, :] = jnp.where(pos > 0, _rms(xp_ref[...], nw).astype(BF16), zero_halo)
    lhs_scr[SSM_HALO:SSM_HALO + tm, :] = h
    lhs_scr[SSM_HALO + tm:rows, :] = jnp.where(pos < blocks_per_seq - 1,
                                               _rms(xn_ref[...], nw).astype(BF16), zero_halo)
    n_chunks = XBC_WIDTH // SSM_COLS

    def chunk_cols(j):
        return pl.ds(pl.multiple_of(j * SSM_COLS, LANES), SSM_COLS)

    def project(j, slot):
        u_scr[slot] = jnp.dot(lhs_scr[...], wx_ref[:, chunk_cols(j)], preferred_element_type=F32)

    def finish(j, slot):
        y = _conv3_rows(u_scr[slot], cw_ref[:, chunk_cols(j)], cb_ref[:, chunk_cols(j)], SSM_HALO, tm)
        xa_ref[:, chunk_cols(j)] = _silu(y).astype(BF16)

    _software_pipeline(n_chunks, project, finish, depth=2, slots=SSM_SLOTS, rolled=False)

    x_dt = jnp.dot(h, wdt_ref[...], preferred_element_type=F32) + dtb_ref[...]
    dt = jnp.maximum(x_dt, 0.0) + jnp.log1p(jnp.exp(-jnp.abs(x_dt)))
    a = dt * (-jnp.exp(alog_ref[...]) * LOG2E)
    tri = (lax.broadcasted_iota(jnp.int32, (CHUNK, CHUNK), 1)
           <= lax.broadcasted_iota(jnp.int32, (CHUNK, CHUNK), 0)).astype(BF16)
    dt_ref[...] = dt
    ldt_ref[...] = jnp.log(dt) * LOG2E
    for ch in range(tm // CHUNK):
        rows_c = slice(ch * CHUNK, (ch + 1) * CHUNK)
        cs_ref[rows_c, :] = _split_dot_lhs_const(tri, a[rows_c])


def _ssm_projection(x2d, norm_w, w_in, conv_w, conv_b, a_log_f, a_log_b, dt_bias_f, dt_bias_b,
                    seq, tm=512):
    t_rows = x2d.shape[0]
    pad = LANES - 2 * N_SSM_HEADS
    alog = jnp.pad(jnp.concatenate([a_log_f, a_log_b]), (0, pad)).reshape(1, LANES)
    dtb = jnp.pad(jnp.concatenate([dt_bias_f, dt_bias_b]), (0, pad)).reshape(1, LANES)
    o = 3 * ATTN_WIDTH + D_INNER
    wx = w_in[:, o:o + XBC_WIDTH].astype(BF16)
    wdt = jnp.pad(w_in[:, o + XBC_WIDTH:], ((0, 0), (0, pad))).astype(BF16)
    hpb = tm // SSM_HALO
    last_halo = t_rows // SSM_HALO - 1
    row = lambda width: pl.BlockSpec((tm, width), lambda i: (i, 0))
    return pl.pallas_call(
        functools.partial(_ssm_proj_kernel, blocks_per_seq=seq // tm),
        grid=(t_rows // tm,),
        in_specs=[row(D_MODEL),
                  pl.BlockSpec((SSM_HALO, D_MODEL), lambda i: (jnp.maximum(i * hpb - 1, 0), 0)),
                  pl.BlockSpec((SSM_HALO, D_MODEL), lambda i: (jnp.minimum((i + 1) * hpb, last_halo), 0)),
                  _const_spec((1, D_MODEL)), _const_spec((D_MODEL, XBC_WIDTH)),
                  _const_spec((D_MODEL, LANES)),
                  _const_spec((3, XBC_WIDTH)), _const_spec((1, XBC_WIDTH)),
                  _const_spec((1, LANES)), _const_spec((1, LANES))],
        out_specs=[row(XBC_WIDTH), row(LANES), row(LANES), row(LANES)],
        out_shape=[jax.ShapeDtypeStruct((t_rows, XBC_WIDTH), BF16)]
        + [jax.ShapeDtypeStruct((t_rows, LANES), F32)] * 3,
        scratch_shapes=[pltpu.VMEM((tm + 2 * SSM_HALO, D_MODEL), BF16),
                        pltpu.VMEM((SSM_SLOTS, tm + 2 * SSM_HALO, SSM_COLS), F32)],
        compiler_params=_params("parallel"),
        name="ssm_projection",
    )(x2d, x2d, x2d, norm_w.reshape(1, D_MODEL), wx, wdt, conv_w.T, conv_b.reshape(1, XBC_WIDTH),
      alog, dtb)


ATT_TQ = 128
ATT_TK = ATT_TQ + 2 * BAND_HALF
ATT_UNITS_PER_STEP = 16
ATT_STAT_PAIRS = 4
ATT_QUERIES_PER_STEP = 512


def _attn_kernel(*refs, length, n_sub, n_pairs, first, last):
    q_ref, k_ref, v_ref = refs[:3]
    if first:
        o_ref, st_ref, s_scr, slab_ref = refs[3:]
    elif last:
        op_ref, sp_ref, o_ref, s_scr = refs[3:]
    else:
        op_ref, sp_ref, o_ref, st_ref, s_scr, slab_ref = refs[3:]
    tq, tk = ATT_TQ, ATT_TK
    qi = pl.program_id(3)
    lane = lax.broadcasted_iota(jnp.int32, (tq, LANES), 1)
    even = lane < HEAD_DIM
    stat_is_max = (lane & 15) < 8
    delta = (lax.broadcasted_iota(jnp.int32, (tq, tk), 1)
             - lax.broadcasted_iota(jnp.int32, (tq, tk), 0))
    nt = (((1,), (1,)), ((), ()))
    ones = jnp.ones((tk, LANES), BF16)
    sub_rows = tq // CARRY_SPLIT

    def window_start(sb):
        q0 = (qi * n_sub + sb) * tq
        return q0, pl.multiple_of(jnp.clip(q0 - BAND_HALF, 0, length - tk), BAND_HALF)

    row_max = []
    for sb in range(n_sub):
        q0, ws = window_start(sb)
        bias = jnp.where(jnp.abs(delta + (ws - q0)) <= BAND_HALF, 0.0, NEG_BIG)
        bias2 = jnp.concatenate([bias, bias], axis=0)
        for hp in range(n_pairs):
            cols = slice(hp * LANES, (hp + 1) * LANES)
            q2 = q_ref[0, 0, sb * tq:(sb + 1) * tq, cols]
            zero = jnp.zeros_like(q2)
            qq = jnp.concatenate([jnp.where(even, q2, zero), jnp.where(even, zero, q2)], axis=0)
            k2 = k_ref[0, 0, pl.ds(ws, tk), cols]
            s = lax.dot_general(qq, k2, nt, preferred_element_type=F32) + bias2
            s_scr[sb * n_pairs + hp] = s
            row_max.append(jnp.max(s, axis=-1, keepdims=True))

    for sb in range(n_sub):
        _, ws = window_start(sb)
        rows = slice(sb * tq, (sb + 1) * tq)
        out_rows = slice(sb * sub_rows, (sb + 1) * sub_rows)
        stats = jnp.zeros((tq, LANES), F32)
        for hp in range(n_pairs):
            u = sb * n_pairs + hp
            cols = slice(hp * LANES, (hp + 1) * LANES)
            v_ext = jnp.concatenate([v_ref[0, 0, pl.ds(ws, tk), cols], ones], axis=1)
            p = jnp.exp2(s_scr[u] - row_max[u]).astype(BF16)
            pv = jnp.dot(p, v_ext, preferred_element_type=F32)
            acc = jnp.where(even, pv[:tq, :LANES], pv[tq:, :LANES])
            l = jnp.where(even, pv[:tq, LANES:], pv[tq:, LANES:])
            m = jnp.where(even, row_max[u][:tq], row_max[u][tq:])
            if not first:
                sp = sp_ref[0, 0, rows, :]
                be = LANES * (hp // ATT_STAT_PAIRS) + 16 * (hp % ATT_STAT_PAIRS)
                bo = be + HEAD_DIM
                m_prev = jnp.where(even, sp[:, be:be + 1], sp[:, bo:bo + 1])
                l_prev = jnp.where(even, sp[:, be + 8:be + 9], sp[:, bo + 8:bo + 9])
                acc_prev = op_ref[0, 0, rows, cols].astype(F32)
                m_new = jnp.maximum(m_prev, m)
                a_prev = jnp.exp2(m_prev - m_new)
                a_cur = jnp.exp2(m - m_new)
                acc = acc_prev * a_prev + acc * a_cur
                l = l_prev * a_prev + l * a_cur
                m = m_new
            if last:
                o_ref[0, 0, rows, cols] = (acc / l).astype(BF16)
            else:
                slab_ref[u] = acc
                for j in range(CARRY_SPLIT):
                    o_ref[0, j, 0, out_rows, cols] = (
                        slab_ref[u, pl.ds(j, sub_rows, stride=CARRY_SPLIT), :].astype(BF16))
                in_zone = ((lane & (HEAD_DIM - 1)) >> 4) == hp
                stats = jnp.where(in_zone, jnp.where(stat_is_max, m, l), stats)
        if not last:
            su = n_sub * n_pairs + sb
            slab_ref[su] = stats
            for j in range(CARRY_SPLIT):
                st_ref[0, j, 0, out_rows, :] = slab_ref[su, pl.ds(j, sub_rows, stride=CARRY_SPLIT), :]


def _attention_pattern(q, k, v, o_prev, st_prev, first, last):
    b, dil, length, width = q.shape
    qs = min(length, ATT_QUERIES_PER_STEP)
    n_sub = qs // ATT_TQ
    n_pairs = min(ATT_UNITS_PER_STEP // n_sub, width // LANES)
    hw = n_pairs * LANES
    n_hg = width // hw
    n_units = n_sub * n_pairs
    n_stat = n_pairs // ATT_STAT_PAIRS
    assert last or n_stat == 1
    blk = pl.BlockSpec((1, 1, qs, hw), lambda bi, r, g, qi: (bi, r, qi, g))
    seq = pl.BlockSpec((1, 1, length, hw), lambda bi, r, g, qi: (bi, r, 0, g))
    stat = pl.BlockSpec((1, 1, qs, n_stat * LANES), lambda bi, r, g, qi: (bi, r, qi, g))
    in_specs, args = [blk, seq, seq], [q, k, v]
    if not first:
        in_specs += [blk, stat]
        args += [o_prev, st_prev]
    scratch = [pltpu.VMEM((n_units, 2 * ATT_TQ, ATT_TK), F32)]
    if last:
        out_specs = [blk]
        out_shape = [jax.ShapeDtypeStruct((b, dil, length, width), BF16)]
    else:
        cs = CARRY_SPLIT
        carry = lambda w: pl.BlockSpec((1, cs, 1, qs // cs, w), lambda bi, r, g, qi: (bi, 0, r, qi, g))
        out_specs = [carry(hw), carry(LANES)]
        out_shape = [jax.ShapeDtypeStruct((b, cs, dil, length // cs, width), BF16),
                     jax.ShapeDtypeStruct((b, cs, dil, length // cs, n_hg * LANES), F32)]
        scratch.append(pltpu.VMEM((n_units + n_sub, ATT_TQ, LANES), F32))
    outs = pl.pallas_call(
        functools.partial(_attn_kernel, length=length, n_sub=n_sub, n_pairs=n_pairs, first=first,
                          last=last),
        grid=(b, dil, n_hg, length // qs),
        in_specs=in_specs, out_specs=out_specs, out_shape=out_shape,
        scratch_shapes=scratch,
        compiler_params=_params("parallel", "parallel", "parallel", "arbitrary"),
        name=f"attention_dil{dil}",
    )(*args)
    if last:
        return outs[0], None
    nxt = lambda t: t.reshape(b, CARRY_SPLIT * dil, length // CARRY_SPLIT, t.shape[-1])
    return nxt(outs[0]), nxt(outs[1])


def _dilated_attention(qkv_planes):
    o = st = None
    for i, (q, k, v) in enumerate(qkv_planes):
        o, st = _attention_pattern(q, k, v, o, st, i == 0, i == len(qkv_planes) - 1)
    return o


def _split_dot(v, mat, passes):
    out = None
    r = v
    for i in range(passes):
        piece = r.astype(BF16)
        term = jnp.dot(piece, mat, preferred_element_type=F32)
        out = term if out is None else out + term
        if i + 1 < passes:
            r = r - piece.astype(F32)
    return out


def _split_dot_lhs_const(mat, v):
    out = None
    r = v
    for i in range(3):
        piece = r.astype(BF16)
        term = jnp.dot(mat, piece, preferred_element_type=F32)
        out = term if out is None else out + term
        if i < 2:
            r = r - piece.astype(F32)
    return out


def _ssm_kernel(xa_ref, dt_ref, ldt_ref, cs_ref, z_ref, alog_ref, dsk_ref, nw_ref, ef_ref, eb_ref,
                y_ref, hb_ref, bt_c, hrun_ref, *, n_steps):
    L = CHUNK
    cps = SSM_CHUNKS_PER_STEP
    step = pl.program_id(1)
    is_bwd = step < n_steps
    blk = jnp.where(is_bwd, n_steps - 1 - step, step - n_steps)

    @pl.when((step == 0) | (step == n_steps))
    def _():
        hrun_ref[...] = jnp.zeros_like(hrun_ref)

    n_bc = N_SSM_GROUPS * D_STATE
    gw = D_INNER // N_SSM_GROUPS
    li = lax.broadcasted_iota(jnp.int32, (L, L), 0)
    si = lax.broadcasted_iota(jnp.int32, (L, L), 1)
    a_log2 = -jnp.exp(alog_ref[...]) * LOG2E

    def scalars(rows):
        dt, ldt, cs = dt_ref[0, rows, :], ldt_ref[0, rows, :], cs_ref[0, rows, :]
        return dt, ldt, cs, cs - dt * a_log2, cs[L - 1:L, :]

    def state_update(xs, bts, weights, tot, e_ref):
        xw = (_split_dot(weights, e_ref[...], 1) * xs).astype(BF16)
        decay = _split_dot(jnp.broadcast_to(jnp.exp2(tot), (8, LANES)), e_ref[...], 3)[0:1]
        for g in range(N_SSM_GROUPS):
            cols = slice(g * gw, (g + 1) * gw)
            s_g = jnp.dot(bts[g], xw[:, cols], preferred_element_type=F32)
            hrun_ref[:, cols] = hrun_ref[:, cols] * decay[:, cols] + s_g

    def bwd_chunk(i, carry):
        ci = cps - 1 - i
        c = blk * cps + ci
        rows = pl.ds(pl.multiple_of(ci * L, L), L)
        dt, ldt, cs, ecs, tot = scalars(rows)
        hb_ref[c] = hrun_ref[...].astype(BF16)
        xs = xa_ref[0, rows, 0:D_INNER].astype(F32)
        bm = xa_ref[0, rows, D_INNER:D_INNER + n_bc].astype(F32)
        bts = [bm[:, g * D_STATE:(g + 1) * D_STATE].T.astype(BF16) for g in range(N_SSM_GROUPS)]
        bt_c[c] = jnp.concatenate(bts, axis=1)
        state_update(xs, bts, jnp.exp2(ecs + ldt), tot, eb_ref)
        return carry

    def fwd_chunk(ci, carry):
        c = blk * cps + ci
        rows = pl.ds(pl.multiple_of(ci * L, L), L)
        dt, ldt, cs, ecs, tot = scalars(rows)
        xs_b = xa_ref[0, rows, 0:D_INNER]
        xs = xs_b.astype(F32)
        bts = [bt_c[c, :, g * D_STATE:(g + 1) * D_STATE] for g in range(N_SSM_GROUPS)]
        hf_in = hrun_ref[...].astype(BF16)
        hb_in = hb_ref[c]
        scale_f = _split_dot(jnp.exp2(cs), ef_ref[...], 1)
        scale_b = _split_dot(jnp.exp2(tot - ecs), eb_ref[...], 1)
        row_f = (cs - ldt).T
        row_b = (ecs + ldt).T
        dt_t = dt.T
        lane = lax.broadcasted_iota(jnp.int32, (L, LANES), 1)
        even = lane < HEAD_DIM
        below = si < li
        diag = si == li
        hpg = N_SSM_HEADS // N_SSM_GROUPS
        y_groups = []
        for g in range(N_SSM_GROUPS):
            cols = slice(g * gw, (g + 1) * gw)
            cg = xa_ref[0, rows, D_INNER + n_bc + g * D_STATE:D_INNER + n_bc + (g + 1) * D_STATE]
            bg = xa_ref[0, rows, D_INNER + g * D_STATE:D_INNER + (g + 1) * D_STATE]
            gmat = lax.dot_general(cg, bg, (((1,), (1,)), ((), ())), preferred_element_type=F32)
            y_off = (scale_f[:, cols] * jnp.dot(cg, hf_in[:, cols], preferred_element_type=F32)
                     + scale_b[:, cols] * jnp.dot(cg, hb_in[:, cols], preferred_element_type=F32))
            pairs = []
            for pr in range(hpg // 2):
                h0 = g * hpg + 2 * pr
                xs_pair = xs_b[:, h0 * HEAD_DIM:(h0 + 2) * HEAD_DIM]
                res = []
                for h in (h0, h0 + 1):
                    hb = N_SSM_HEADS + h
                    sel = jnp.where(below, cs[:, h:h + 1] - row_f[h:h + 1, :],
                                    row_b[hb:hb + 1, :] - ecs[:, hb:hb + 1])
                    e = jnp.exp2(sel) + jnp.where(diag, dt_t[h:h + 1, :], 0.0)
                    w = (gmat * e).astype(BF16)
                    res.append(jnp.dot(w, xs_pair, preferred_element_type=F32))
                pairs.append(jnp.where(even, res[0], res[1]))
            y_groups.append(jnp.concatenate(pairs, axis=1) + y_off)
        y = jnp.concatenate(y_groups, axis=1) + dsk_ref[...] * xs
        zf = z_ref[0, rows, :].astype(F32)
        gated = y * _silu(zf)
        outs = []
        for g in range(N_SSM_GROUPS):
            gg = gated[:, g * gw:(g + 1) * gw]
            outs.append(gg * lax.rsqrt(jnp.mean(gg * gg, axis=-1, keepdims=True) + EPS))
        y_ref[0, rows, :] = (jnp.concatenate(outs, axis=1) * nw_ref[...]).astype(BF16)
        state_update(xs, bts, jnp.exp2(tot - cs + ldt), tot, ef_ref)
        return carry

    @pl.when(is_bwd)
    def _():
        lax.fori_loop(0, cps, bwd_chunk, 0)

    @pl.when(jnp.logical_not(is_bwd))
    def _():
        lax.fori_loop(0, cps, fwd_chunk, 0)


SSM_CHUNKS_PER_STEP = 4


def _head_expanders():
    col_head = np.arange(D_INNER) // HEAD_DIM
    rows = np.arange(LANES)[:, None]
    ef = (rows == col_head[None, :]).astype(np.float32)
    eb = (rows == col_head[None, :] + N_SSM_HEADS).astype(np.float32)
    return jnp.asarray(ef, BF16), jnp.asarray(eb, BF16)


def _ssm_mixer(z, xa, dt, ldt, cs, a_log_f, a_log_b, d_skip, norm_w):
    b, s, _ = z.shape
    n = s // CHUNK
    pad = LANES - 2 * N_SSM_HEADS
    alog = jnp.pad(jnp.concatenate([a_log_f, a_log_b]), (0, pad)).reshape(1, LANES)
    dsk = jnp.repeat(d_skip, HEAD_DIM).reshape(1, D_INNER)
    ef, eb = _head_expanders()

    rows = SSM_CHUNKS_PER_STEP * CHUNK
    n_steps = s // rows

    def block_of(st):
        return jnp.where(st < n_steps, n_steps - 1 - st, st - n_steps)

    per_chunk = pl.BlockSpec((1, rows, LANES), lambda bi, st: (bi, block_of(st), 0))
    fwd_only = pl.BlockSpec((1, rows, D_INNER), lambda bi, st: (bi, jnp.maximum(st - n_steps, 0), 0))
    return pl.pallas_call(
        functools.partial(_ssm_kernel, n_steps=n_steps),
        grid=(b, 2 * n_steps),
        in_specs=[
            pl.BlockSpec((1, rows, XBC_WIDTH), lambda bi, st: (bi, block_of(st), 0)),
            per_chunk, per_chunk, per_chunk, fwd_only,
            _const_spec((1, LANES)), _const_spec((1, D_INNER)), _const_spec((1, D_INNER)),
            _const_spec((LANES, D_INNER)), _const_spec((LANES, D_INNER)),
        ],
        out_specs=fwd_only,
        out_shape=jax.ShapeDtypeStruct((b, s, D_INNER), BF16),
        scratch_shapes=[pltpu.VMEM((n, D_STATE, D_INNER), BF16),
                        pltpu.VMEM((n, D_STATE, N_SSM_GROUPS * CHUNK), BF16),
                        pltpu.VMEM((D_STATE, D_INNER), F32)],
        compiler_params=_params("arbitrary", "arbitrary"),
        name="ssm_mixer",
    )(xa, dt, ldt, cs, z, alog, dsk, norm_w.reshape(1, D_INNER), ef, eb)


def _outproj_kernel(x_ref, a_ref, s_ref, wa_ref, ws_ref, nw_ref, x1_ref, h_ref, slab_ref):
    tm = x_ref.shape[0]
    dil = a_ref.shape[1]
    n_cb = ATTN_WIDTH // LANES
    for r in range(dil):
        blk = a_ref[0, r].astype(F32)
        for cb in range(n_cb):
            slab_ref[cb, pl.ds(r, tm // dil, stride=dil), :] = blk[:, cb * LANES:(cb + 1) * LANES]
    attn = jnp.concatenate([slab_ref[cb] for cb in range(n_cb)], axis=1).astype(BF16)
    x1 = (x_ref[...] + jnp.dot(attn, wa_ref[...], preferred_element_type=F32)
          + jnp.dot(s_ref[...], ws_ref[...], preferred_element_type=F32))
    x1_ref[...] = x1
    h_ref[...] = _rms(x1, nw_ref[...]).astype(BF16)


def _out_projection(x2d, attn_planes, ssm, w_out, norm_w, seq, tm=512):
    t_rows = x2d.shape[0]
    dil = attn_planes.shape[1]
    nseq = seq // tm
    row = pl.BlockSpec((tm, D_MODEL), lambda i: (i, 0))
    planes = pl.BlockSpec((1, dil, tm // dil, ATTN_WIDTH), lambda i: (i // nseq, 0, i % nseq, 0))
    wa = w_out[:ATTN_WIDTH].astype(BF16)
    ws = w_out[ATTN_WIDTH:].astype(BF16)
    return pl.pallas_call(
        _outproj_kernel,
        grid=(t_rows // tm,),
        in_specs=[row, planes, row, _const_spec((ATTN_WIDTH, D_MODEL)), _const_spec((D_INNER, D_MODEL)),
                  _const_spec((1, D_MODEL))],
        out_specs=[row, row],
        out_shape=[jax.ShapeDtypeStruct((t_rows, D_MODEL), F32),
                   jax.ShapeDtypeStruct((t_rows, D_MODEL), BF16)],
        scratch_shapes=[pltpu.VMEM((ATTN_WIDTH // LANES, tm, LANES), F32)],
        compiler_params=_params("parallel"),
        name="out_projection",
    )(x2d, attn_planes, ssm, wa, ws, norm_w.reshape(1, D_MODEL))


FFN_COLS = 256
FFN_HALO = BF16_ROWS
FFN_SLOTS = 2


def _ffn_up_kernel(hc_ref, hp_ref, hn_ref, w_ref, cw_ref, cb_ref, act_ref, lhs_scr, u_scr, *,
                   blocks_per_seq):
    tm = hc_ref.shape[0]
    pos = pl.program_id(0) % blocks_per_seq
    rows = tm + 2 * FFN_HALO
    lhs_scr[0:FFN_HALO, :] = jnp.where(pos > 0, hp_ref[...], jnp.zeros_like(hp_ref))
    lhs_scr[FFN_HALO:FFN_HALO + tm, :] = hc_ref[...]
    lhs_scr[FFN_HALO + tm:rows, :] = jnp.where(pos < blocks_per_seq - 1, hn_ref[...],
                                               jnp.zeros_like(hn_ref))

    n_chunks = D_FF // FFN_COLS

    def chunk_cols(j, half):
        return pl.ds(pl.multiple_of(j * FFN_COLS + half * D_FF, LANES), FFN_COLS)

    def project(j, slot):
        for half in range(2):
            u_scr[2 * slot + half] = jnp.dot(lhs_scr[...], w_ref[:, chunk_cols(j, half)],
                                             preferred_element_type=F32)

    def conv(j, slot, half):
        return _conv3_rows(u_scr[2 * slot + half], cw_ref[:, chunk_cols(j, half)],
                           cb_ref[:, chunk_cols(j, half)], FFN_HALO, tm)

    def finish(j, slot):
        act_ref[:, chunk_cols(j, 0)] = (_silu(conv(j, slot, 0)) * conv(j, slot, 1)).astype(BF16)

    _software_pipeline(n_chunks, project, finish, depth=1, slots=FFN_SLOTS, rolled=True)


def _ffn_up(h, w_up, conv_w, conv_b, seq, tm=512):
    t_rows = h.shape[0]
    hpb = tm // FFN_HALO
    last_halo = t_rows // FFN_HALO - 1
    width = 2 * D_FF
    return pl.pallas_call(
        functools.partial(_ffn_up_kernel, blocks_per_seq=seq // tm),
        grid=(t_rows // tm,),
        in_specs=[
            pl.BlockSpec((tm, D_MODEL), lambda i: (i, 0)),
            pl.BlockSpec((FFN_HALO, D_MODEL), lambda i: (jnp.maximum(i * hpb - 1, 0), 0)),
            pl.BlockSpec((FFN_HALO, D_MODEL), lambda i: (jnp.minimum((i + 1) * hpb, last_halo), 0)),
            _const_spec((D_MODEL, width)), _const_spec((3, width)), _const_spec((1, width)),
        ],
        out_specs=pl.BlockSpec((tm, D_FF), lambda i: (i, 0)),
        out_shape=jax.ShapeDtypeStruct((t_rows, D_FF), BF16),
        scratch_shapes=[pltpu.VMEM((tm + 2 * FFN_HALO, D_MODEL), BF16),
                        pltpu.VMEM((2 * FFN_SLOTS, tm + 2 * FFN_HALO, FFN_COLS), F32)],
        compiler_params=_params("parallel"),
        name="ffn_up",
    )(h, h, h, w_up.astype(BF16), conv_w.T, conv_b.reshape(1, width))


def _ffn_down_kernel(a_ref, x1_ref, wd_ref, nw_ref, o_ref):
    acc = x1_ref[...] + jnp.dot(a_ref[...], wd_ref[...], preferred_element_type=F32)
    o_ref[...] = _rms(acc, nw_ref[...])


def _ffn_down(act, x1, w_down, norm_w, tm=512):
    t_rows = act.shape[0]
    return pl.pallas_call(
        _ffn_down_kernel,
        grid=(t_rows // tm,),
        in_specs=[pl.BlockSpec((tm, D_FF), lambda i: (i, 0)), pl.BlockSpec((tm, D_MODEL), lambda i: (i, 0)),
                  _const_spec((D_FF, D_MODEL)), _const_spec((1, D_MODEL))],
        out_specs=pl.BlockSpec((tm, D_MODEL), lambda i: (i, 0)),
        out_shape=jax.ShapeDtypeStruct((t_rows, D_MODEL), F32),
        compiler_params=_params("parallel"),
        name="ffn_down",
    )(act, x1, w_down.astype(BF16), norm_w.reshape(1, D_MODEL))


def kernel(x, norm1_w, w_in, ssm_conv_w, ssm_conv_b, a_log_f, a_log_b, dt_bias_f, dt_bias_b, d_skip,
           ssm_norm_w, w_out, norm2_w, w_up, ffn_conv_w, ffn_conv_b, w_down, final_norm_w):
    b, s, d = x.shape
    depth = w_in.shape[0]
    x2d = x.reshape(b * s, d)
    for layer in range(depth):
        (q1, k1, v1, q4, k4, v4, q16, k16, v16, z) = _in_projection(
            x2d, norm1_w[layer], w_in[layer], b, s)
        xa, dt, ldt, cs = _ssm_projection(
            x2d, norm1_w[layer], w_in[layer], ssm_conv_w[layer], ssm_conv_b[layer],
            a_log_f[layer], a_log_b[layer], dt_bias_f[layer], dt_bias_b[layer], s)
        nat = lambda t: t.reshape(b, 1, s, t.shape[-1])
        attn = _dilated_attention([(nat(q1), nat(k1), nat(v1)), (q4, k4, v4), (q16, k16, v16)])
        sh = lambda t: t.reshape(b, s, t.shape[-1])
        ssm = _ssm_mixer(sh(z), sh(xa), sh(dt), sh(ldt), sh(cs), a_log_f[layer], a_log_b[layer],
                         d_skip[layer], ssm_norm_w[layer])
        x1, h2 = _out_projection(x2d, attn, ssm.reshape(b * s, -1), w_out[layer], norm2_w[layer], s)
        act = _ffn_up(h2, w_up[layer], ffn_conv_w[layer], ffn_conv_b[layer], s)
        assert depth == 1
        x2d = _ffn_down(act, x1, w_down[layer], final_norm_w)
    return x2d.reshape(b, s, d)
```

```python
import functools

import numpy as np
import jax
import jax.numpy as jnp
from jax import lax
from jax.experimental import pallas as pl
from jax.experimental.pallas import tpu as pltpu

F32 = jnp.float32
BF16 = jnp.bfloat16

D_MODEL = 1024
HEAD_DIM = 64
N_ATTN_HEADS = 16
ATTN_WIDTH = N_ATTN_HEADS * HEAD_DIM
ROPE_DIM = HEAD_DIM // 4
ROPE_THETA = 500000.0
DILATIONS = (1, 4, 16)
BAND_HALF = 64
CARRY_SPLIT = 4

D_INNER = 1024
N_SSM_HEADS = 16
N_SSM_GROUPS = 4
D_STATE = 128
CHUNK = 128
XBC_WIDTH = D_INNER + 2 * N_SSM_GROUPS * D_STATE
D_FF = 2816
EPS = 1e-6

LANES = 128
BF16_ROWS = 16
VMEM_LIMIT = 56 * 1024 * 1024
NEG_BIG = -1e30
LOG2E = 1.4426950408889634


def _params(*sem):
    return pltpu.CompilerParams(dimension_semantics=sem, vmem_limit_bytes=VMEM_LIMIT)


def _const_spec(shape):
    return pl.BlockSpec(shape, lambda *_: (0,) * len(shape))


def _rms(x, w):
    return x * lax.rsqrt(jnp.mean(x * x, axis=-1, keepdims=True) + EPS) * w


def _silu(y):
    h = 0.5 * y
    return h + h * jnp.tanh(h)


def _conv3_rows(u, w, b, halo, rows):
    r = u.shape[0]
    y = pltpu.roll(u, 1, 0) * w[0:1] + u * w[1:2] + pltpu.roll(u, r - 1, 0) * w[2:3] + b
    return y[halo:halo + rows]


def _software_pipeline(n_chunks, project, finish, depth, slots, rolled):
    for k in range(min(depth, n_chunks)):
        project(k, k % slots)
    n_rolled = max(n_chunks - depth, 0) // slots if rolled else 0

    def body(i, carry):
        k0 = i * slots
        for s in range(slots):
            project(k0 + s + depth, (s + depth) % slots)
            finish(k0 + s, s)
        return carry

    if n_rolled:
        lax.fori_loop(0, n_rolled, body, 0)
    for k in range(n_rolled * slots, n_chunks):
        if k + depth < n_chunks:
            project(k + depth, (k + depth) % slots)
        finish(k, k % slots)


def _inproj_kernel(x_ref, nw_ref, wq_ref, wk_ref, wv_ref, wz_ref, rc_ref, rs1_ref, rs2_ref,
                   q4_ref, k4_ref, v4_ref, q16_ref, k16_ref, v16_ref, z_ref, slab_ref, slab2_ref):
    tm = x_ref.shape[0]
    h = _rms(x_ref[...], nw_ref[...]).astype(BF16)
    rc, rs1, rs2 = rc_ref[...], rs1_ref[...], rs2_ref[...]
    n_cb = ATTN_WIDTH // LANES

    def emit(idx, w_ref, out_refs, rope):
        t = jnp.dot(h, w_ref[...], preferred_element_type=F32)
        for cb in range(n_cb):
            cols = slice(cb * LANES, (cb + 1) * LANES)
            blk = t[:, cols]
            if rope:
                lo = pltpu.roll(blk, ROPE_DIM // 2, 1)
                hi = pltpu.roll(blk, LANES - ROPE_DIM // 2, 1)
                blk = blk * rc + lo * rs1 + hi * rs2
            slab = slab_ref.at[idx * n_cb + cb]
            slab2 = slab2_ref.at[idx * n_cb + cb]
            slab[...] = blk
            n4 = tm // CARRY_SPLIT
            n16 = n4 // CARRY_SPLIT
            for r4 in range(CARRY_SPLIT):
                p4 = slab[pl.ds(r4, n4, stride=CARRY_SPLIT), :]
                out_refs[0][0, r4, :, cols] = p4.astype(BF16)
                slab2[r4 * n4:(r4 + 1) * n4, :] = p4
                for j in range(CARRY_SPLIT):
                    out_refs[1][0, CARRY_SPLIT * j + r4, :, cols] = (
                        slab2[pl.ds(r4 * n4 + j, n16, stride=CARRY_SPLIT), :].astype(BF16))

    emit(0, wq_ref, (q4_ref, q16_ref), True)
    emit(1, wk_ref, (k4_ref, k16_ref), True)
    emit(2, wv_ref, (v4_ref, v16_ref), False)
    z_ref[...] = jnp.dot(h, wz_ref[...], preferred_element_type=F32).astype(BF16)


def _rope_tables(seq):
    half = ROPE_DIM // 2
    inv_freq = jnp.power(ROPE_THETA, -jnp.arange(half, dtype=F32) * 2.0 / ROPE_DIM)
    ang = jnp.arange(seq, dtype=F32)[:, None] * inv_freq[None, :]
    cos, sin = jnp.cos(ang), jnp.sin(ang)
    one = jnp.ones((seq, HEAD_DIM - ROPE_DIM), F32)
    zero8 = jnp.zeros((seq, half), F32)
    zero = jnp.zeros((seq, HEAD_DIM - ROPE_DIM), F32)
    rc = jnp.concatenate([cos, cos, one], axis=1)
    rs1 = jnp.concatenate([zero8, sin, zero], axis=1)
    rs2 = jnp.concatenate([-sin, zero8, zero], axis=1)
    rep = LANES // HEAD_DIM
    return tuple(jnp.tile(t, (1, rep)) for t in (rc, rs1, rs2))


def _in_projection(x2d, norm_w, w_in, batch, seq, tm=256):
    t_rows = x2d.shape[0]
    a = ATTN_WIDTH
    wq = (w_in[:, :a] * (HEAD_DIM ** -0.5 * LOG2E)).astype(BF16)
    wk = w_in[:, a:2 * a].astype(BF16)
    wv = w_in[:, 2 * a:3 * a].astype(BF16)
    wz = w_in[:, 3 * a:3 * a + D_INNER].astype(BF16)
    rc, rs1, rs2 = _rope_tables(seq)
    nseq = seq // tm
    row = lambda width: pl.BlockSpec((tm, width), lambda i: (i, 0))
    tab = pl.BlockSpec((tm, LANES), lambda i: (i % nseq, 0))
    plane = lambda dil: pl.BlockSpec((1, dil, tm // dil, a), lambda i: (i // nseq, 0, i % nseq, 0))
    plane_shape = lambda dil: jax.ShapeDtypeStruct((batch, dil, seq // dil, a), BF16)
    d4, d16 = DILATIONS[1:]
    return pl.pallas_call(
        _inproj_kernel,
        grid=(t_rows // tm,),
        in_specs=[row(D_MODEL), _const_spec((1, D_MODEL)),
                  _const_spec((D_MODEL, a)), _const_spec((D_MODEL, a)), _const_spec((D_MODEL, a)),
                  _const_spec((D_MODEL, D_INNER)), tab, tab, tab],
        out_specs=[plane(d4)] * 3 + [plane(d16)] * 3 + [row(D_INNER)],
        out_shape=[plane_shape(d4)] * 3 + [plane_shape(d16)] * 3
        + [jax.ShapeDtypeStruct((t_rows, D_INNER), BF16)],
        scratch_shapes=[pltpu.VMEM((3 * a // LANES, tm, LANES), F32)] * 2,
        compiler_params=_params("parallel"),
        name="in_projection",
    )(x2d, norm_w.reshape(1, D_MODEL), wq, wk, wv, wz, rc, rs1, rs2)


SSM_HALO = BF16_ROWS
SSM_COLS = 256
SSM_SLOTS = 4


def _ssm_proj_kernel(xc_ref, xp_ref, xn_ref, nw_ref, wx_ref, wdt_ref, cw_ref, cb_ref, alog_ref, dtb_ref,
                     xa_ref, dt_ref, ldt_ref, cs_ref, lhs_scr, u_scr, *, blocks_per_seq):
    tm = xc_ref.shape[0]
    pos = pl.program_id(0) % blocks_per_seq
    nw = nw_ref[...]
    rows = tm + 2 * SSM_HALO
    h = _rms(xc_ref[...], nw).astype(BF16)
    zero_halo = jnp.zeros((SSM_HALO, D_MODEL), BF16)
    lhs_scr[0:SSM_HALO, :] = jnp.where(pos > 0, _rms(xp_ref[...], nw).astype(BF16), zero_halo)
    lhs_scr[SSM_HALO:SSM_HALO + tm, :] = h
    lhs_scr[SSM_HALO + tm:rows, :] = jnp.where(pos < blocks_per_seq - 1,
                                               _rms(xn_ref[...], nw).astype(BF16), zero_halo)
    n_chunks = XBC_WIDTH // SSM_COLS

    def chunk_cols(j):
        return pl.ds(pl.multiple_of(j * SSM_COLS, LANES), SSM_COLS)

    def project(j, slot):
        u_scr[slot] = jnp.dot(lhs_scr[...], wx_ref[:, chunk_cols(j)], preferred_element_type=F32)

    def finish(j, slot):
        y = _conv3_rows(u_scr[slot], cw_ref[:, chunk_cols(j)], cb_ref[:, chunk_cols(j)], SSM_HALO, tm)
        xa_ref[:, chunk_cols(j)] = _silu(y).astype(BF16)

    _software_pipeline(n_chunks, project, finish, depth=2, slots=SSM_SLOTS, rolled=False)

    x_dt = jnp.dot(h, wdt_ref[...], preferred_element_type=F32) + dtb_ref[...]
    dt = jnp.maximum(x_dt, 0.0) + jnp.log1p(jnp.exp(-jnp.abs(x_dt)))
    a = dt * (-jnp.exp(alog_ref[...]) * LOG2E)
    tri = (lax.broadcasted_iota(jnp.int32, (CHUNK, CHUNK), 1)
           <= lax.broadcasted_iota(jnp.int32, (CHUNK, CHUNK), 0)).astype(BF16)
    dt_ref[...] = dt
    ldt_ref[...] = jnp.log(dt) * LOG2E
    for ch in range(tm // CHUNK):
        rows_c = slice(ch * CHUNK, (ch + 1) * CHUNK)
        cs_ref[rows_c, :] = _split_dot_lhs_const(tri, a[rows_c])


def _ssm_projection(x2d, norm_w, w_in, conv_w, conv_b, a_log_f, a_log_b, dt_bias_f, dt_bias_b,
                    seq, tm=512):
    t_rows = x2d.shape[0]
    pad = LANES - 2 * N_SSM_HEADS
    alog = jnp.pad(jnp.concatenate([a_log_f, a_log_b]), (0, pad)).reshape(1, LANES)
    dtb = jnp.pad(jnp.concatenate([dt_bias_f, dt_bias_b]), (0, pad)).reshape(1, LANES)
    o = 3 * ATTN_WIDTH + D_INNER
    wx = w_in[:, o:o + XBC_WIDTH].astype(BF16)
    wdt = jnp.pad(w_in[:, o + XBC_WIDTH:], ((0, 0), (0, pad))).astype(BF16)
    hpb = tm // SSM_HALO
    last_halo = t_rows // SSM_HALO - 1
    row = lambda width: pl.BlockSpec((tm, width), lambda i: (i, 0))
    return pl.pallas_call(
        functools.partial(_ssm_proj_kernel, blocks_per_seq=seq // tm),
        grid=(t_rows // tm,),
        in_specs=[row(D_MODEL),
                  pl.BlockSpec((SSM_HALO, D_MODEL), lambda i: (jnp.maximum(i * hpb - 1, 0), 0)),
                  pl.BlockSpec((SSM_HALO, D_MODEL), lambda i: (jnp.minimum((i + 1) * hpb, last_halo), 0)),
                  _const_spec((1, D_MODEL)), _const_spec((D_MODEL, XBC_WIDTH)),
                  _const_spec((D_MODEL, LANES)),
                  _const_spec((3, XBC_WIDTH)), _const_spec((1, XBC_WIDTH)),
                  _const_spec((1, LANES)), _const_spec((1, LANES))],
        out_specs=[row(XBC_WIDTH), row(LANES), row(LANES), row(LANES)],
        out_shape=[jax.ShapeDtypeStruct((t_rows, XBC_WIDTH), BF16)]
        + [jax.ShapeDtypeStruct((t_rows, LANES), F32)] * 3,
        scratch_shapes=[pltpu.VMEM((tm + 2 * SSM_HALO, D_MODEL), BF16),
                        pltpu.VMEM((SSM_SLOTS, tm + 2 * SSM_HALO, SSM_COLS), F32)],
        compiler_params=_params("parallel"),
        name="ssm_projection",
    )(x2d, x2d, x2d, norm_w.reshape(1, D_MODEL), wx, wdt, conv_w.T, conv_b.reshape(1, XBC_WIDTH),
      alog, dtb)


ATT_TQ = 128
ATT_TK = ATT_TQ + 2 * BAND_HALF
ATT_UNITS_PER_STEP = 16
ATT_STAT_PAIRS = 4
ATT_QUERIES_PER_STEP = 512


def _attn_kernel(*refs, length, n_sub, n_pairs, first, last):
    interleaved = first
    q_ref, k_ref, v_ref = refs[:3]
    if first:
        o_ref, st_ref = refs[3:]
    elif last:
        op_ref, sp_ref, o_ref = refs[3:]
    else:
        op_ref, sp_ref, o_ref, st_ref, slab_ref = refs[3:]
    tq, tk = ATT_TQ, ATT_TK
    qi = pl.program_id(3)
    lane = lax.broadcasted_iota(jnp.int32, (tq, LANES), 1)
    even = lane < HEAD_DIM
    stat_is_max = (lane & 15) < 8
    row_i = lax.broadcasted_iota(jnp.int32, (tq, tk), 0)
    col_i = lax.broadcasted_iota(jnp.int32, (tq, tk), 1)
    nt = (((1,), (1,)), ((), ()))
    ones = jnp.ones((tk, LANES), BF16)
    sub_rows = tq // CARRY_SPLIT
    key_rows = tk // CARRY_SPLIT

    if interleaved:
        delta = (CARRY_SPLIT * ((col_i & (key_rows - 1)) - (row_i & (sub_rows - 1)))
                 + (col_i // key_rows - row_i // sub_rows))
    else:
        delta = col_i - row_i

    def window_start(sb):
        if interleaved:
            a0 = (qi * n_sub + sb) * sub_rows
            ws = pl.multiple_of(jnp.clip(a0 - BAND_HALF // CARRY_SPLIT, 0, length - key_rows),
                                BF16_ROWS)
            return ws, CARRY_SPLIT * (ws - a0)
        q0 = (qi * n_sub + sb) * tq
        ws = pl.multiple_of(jnp.clip(q0 - BAND_HALF, 0, length - tk), BAND_HALF)
        return ws, ws - q0

    def load_q(sb, cols):
        if interleaved:
            return jnp.concatenate([q_ref[0, j, sb * sub_rows:(sb + 1) * sub_rows, cols]
                                    for j in range(CARRY_SPLIT)], axis=0)
        return q_ref[0, 0, sb * tq:(sb + 1) * tq, cols]

    def load_keys(ref, ws, cols):
        if interleaved:
            return jnp.concatenate([ref[0, j, pl.ds(ws, key_rows), cols]
                                    for j in range(CARRY_SPLIT)], axis=0)
        return ref[0, 0, pl.ds(ws, tk), cols]

    scores, row_max = [], []
    for sb in range(n_sub):
        ws, offset = window_start(sb)
        bias = jnp.where(jnp.abs(delta + offset) <= BAND_HALF, 0.0, NEG_BIG)
        bias2 = jnp.concatenate([bias, bias], axis=0)
        for hp in range(n_pairs):
            cols = slice(hp * LANES, (hp + 1) * LANES)
            q2 = load_q(sb, cols)
            zero = jnp.zeros_like(q2)
            qq = jnp.concatenate([jnp.where(even, q2, zero), jnp.where(even, zero, q2)], axis=0)
            k2 = load_keys(k_ref, ws, cols)
            s = lax.dot_general(qq, k2, nt, preferred_element_type=F32) + bias2
            scores.append(s)
            row_max.append(jnp.max(s, axis=-1, keepdims=True))

    for sb in range(n_sub):
        ws, _ = window_start(sb)
        rows = slice(sb * tq, (sb + 1) * tq)
        out_rows = slice(sb * sub_rows, (sb + 1) * sub_rows)
        stats = jnp.zeros((tq, LANES), F32)
        for hp in range(n_pairs):
            u = sb * n_pairs + hp
            cols = slice(hp * LANES, (hp + 1) * LANES)
            v_ext = jnp.concatenate([load_keys(v_ref, ws, cols), ones], axis=1)
            p = jnp.exp2(scores[u] - row_max[u]).astype(BF16)
            pv = jnp.dot(p, v_ext, preferred_element_type=F32)
            acc = jnp.where(even, pv[:tq, :LANES], pv[tq:, :LANES])
            l = jnp.where(even, pv[:tq, LANES:], pv[tq:, LANES:])
            m = jnp.where(even, row_max[u][:tq], row_max[u][tq:])
            if not first:
                sp = sp_ref[0, 0, rows, :]
                be = LANES * (hp // ATT_STAT_PAIRS) + 16 * (hp % ATT_STAT_PAIRS)
                bo = be + HEAD_DIM
                m_prev = jnp.where(even, sp[:, be:be + 1], sp[:, bo:bo + 1])
                l_prev = jnp.where(even, sp[:, be + 8:be + 9], sp[:, bo + 8:bo + 9])
                acc_prev = op_ref[0, 0, rows, cols].astype(F32)
                m_new = jnp.maximum(m_prev, m)
                a_prev = jnp.exp2(m_prev - m_new)
                a_cur = jnp.exp2(m - m_new)
                acc = acc_prev * a_prev + acc * a_cur
                l = l_prev * a_prev + l * a_cur
                m = m_new
            if last:
                o_ref[0, 0, rows, cols] = (acc / l).astype(BF16)
                continue
            in_zone = ((lane & (HEAD_DIM - 1)) >> 4) == hp
            stats = jnp.where(in_zone, jnp.where(stat_is_max, m, l), stats)
            if interleaved:
                for j in range(CARRY_SPLIT):
                    o_ref[0, j, out_rows, cols] = acc[j * sub_rows:(j + 1) * sub_rows].astype(BF16)
            else:
                slab_ref[hp] = acc
                for j in range(CARRY_SPLIT):
                    o_ref[0, j, 0, out_rows, cols] = (
                        slab_ref[hp, pl.ds(j, sub_rows, stride=CARRY_SPLIT), :].astype(BF16))
        if last:
            continue
        if interleaved:
            for j in range(CARRY_SPLIT):
                st_ref[0, j, out_rows, :] = stats[j * sub_rows:(j + 1) * sub_rows]
        else:
            slab_ref[n_pairs] = stats
            for j in range(CARRY_SPLIT):
                st_ref[0, j, 0, out_rows, :] = (
                    slab_ref[n_pairs, pl.ds(j, sub_rows, stride=CARRY_SPLIT), :])


def _attention_pattern(q, k, v, o_prev, st_prev, first, last):
    cs = CARRY_SPLIT
    if first:
        b, _, length, width = q.shape
        dil, qs = 1, min(length, ATT_QUERIES_PER_STEP // cs)
        n_sub = qs * cs // ATT_TQ
    else:
        b, dil, length, width = q.shape
        qs = min(length, ATT_QUERIES_PER_STEP)
        n_sub = qs // ATT_TQ
    n_pairs = min(ATT_UNITS_PER_STEP // n_sub, width // LANES)
    hw = n_pairs * LANES
    n_hg = width // hw
    n_stat = n_pairs // ATT_STAT_PAIRS
    assert last or n_stat == 1
    scratch = []
    if first:
        blk = pl.BlockSpec((1, cs, qs, hw), lambda bi, r, g, qi: (bi, 0, qi, g))
        seq = pl.BlockSpec((1, cs, length, hw), lambda bi, r, g, qi: (bi, 0, 0, g))
        stat = pl.BlockSpec((1, cs, qs, LANES), lambda bi, r, g, qi: (bi, 0, qi, g))
    else:
        blk = pl.BlockSpec((1, 1, qs, hw), lambda bi, r, g, qi: (bi, r, qi, g))
        seq = pl.BlockSpec((1, 1, length, hw), lambda bi, r, g, qi: (bi, r, 0, g))
        stat = pl.BlockSpec((1, 1, qs, n_stat * LANES), lambda bi, r, g, qi: (bi, r, qi, g))
    in_specs, args = [blk, seq, seq], [q, k, v]
    if not first:
        in_specs += [blk, stat]
        args += [o_prev, st_prev]
    if first:
        out_specs = [blk, stat]
        out_shape = [jax.ShapeDtypeStruct((b, cs, length, width), BF16),
                     jax.ShapeDtypeStruct((b, cs, length, n_hg * LANES), F32)]
    elif last:
        out_specs = [blk]
        out_shape = [jax.ShapeDtypeStruct((b, dil, length, width), BF16)]
    else:
        carry = lambda w: pl.BlockSpec((1, cs, 1, qs // cs, w), lambda bi, r, g, qi: (bi, 0, r, qi, g))
        out_specs = [carry(hw), carry(LANES)]
        out_shape = [jax.ShapeDtypeStruct((b, cs, dil, length // cs, width), BF16),
                     jax.ShapeDtypeStruct((b, cs, dil, length // cs, n_hg * LANES), F32)]
        scratch.append(pltpu.VMEM((n_pairs + 1, ATT_TQ, LANES), F32))
    outs = pl.pallas_call(
        functools.partial(_attn_kernel, length=length, n_sub=n_sub, n_pairs=n_pairs, first=first,
                          last=last),
        grid=(b, dil, n_hg, length // qs),
        in_specs=in_specs, out_specs=out_specs, out_shape=out_shape,
        scratch_shapes=scratch,
        compiler_params=_params("parallel", "parallel", "parallel", "arbitrary"),
        name=f"attention_dil{dil}",
    )(*args)
    if last:
        return outs[0], None
    if first:
        return outs[0], outs[1]
    nxt = lambda t: t.reshape(b, cs * dil, length // cs, t.shape[-1])
    return nxt(outs[0]), nxt(outs[1])


def _dilated_attention(qkv_planes):
    o = st = None
    for i, (q, k, v) in enumerate(qkv_planes):
        o, st = _attention_pattern(q, k, v, o, st, i == 0, i == len(qkv_planes) - 1)
    return o


def _split_dot(v, mat, passes):
    out = None
    r = v
    for i in range(passes):
        piece = r.astype(BF16)
        term = jnp.dot(piece, mat, preferred_element_type=F32)
        out = term if out is None else out + term
        if i + 1 < passes:
            r = r - piece.astype(F32)
    return out


def _split_dot_lhs_const(mat, v):
    out = None
    r = v
    for i in range(3):
        piece = r.astype(BF16)
        term = jnp.dot(mat, piece, preferred_element_type=F32)
        out = term if out is None else out + term
        if i < 2:
            r = r - piece.astype(F32)
    return out


def _ssm_kernel(xa_ref, dt_ref, ldt_ref, cs_ref, z_ref, alog_ref, dsk_ref, nw_ref, ef_ref, eb_ref,
                y_ref, hb_ref, bt_c, hrun_ref, *, n_steps):
    L = CHUNK
    cps = SSM_CHUNKS_PER_STEP
    step = pl.program_id(1)
    is_bwd = step < n_steps
    blk = jnp.where(is_bwd, n_steps - 1 - step, step - n_steps)

    @pl.when((step == 0) | (step == n_steps))
    def _():
        hrun_ref[...] = jnp.zeros_like(hrun_ref)

    n_bc = N_SSM_GROUPS * D_STATE
    gw = D_INNER // N_SSM_GROUPS
    li = lax.broadcasted_iota(jnp.int32, (L, L), 0)
    si = lax.broadcasted_iota(jnp.int32, (L, L), 1)
    a_log2 = -jnp.exp(alog_ref[...]) * LOG2E

    def scalars(rows):
        dt, ldt, cs = dt_ref[0, rows, :], ldt_ref[0, rows, :], cs_ref[0, rows, :]
        return dt, ldt, cs, cs - dt * a_log2, cs[L - 1:L, :]

    def state_update(xs, bts, weights, tot, e_ref):
        xw = (_split_dot(weights, e_ref[...], 1) * xs).astype(BF16)
        decay = _split_dot(jnp.broadcast_to(jnp.exp2(tot), (8, LANES)), e_ref[...], 3)[0:1]
        for g in range(N_SSM_GROUPS):
            cols = slice(g * gw, (g + 1) * gw)
            s_g = jnp.dot(bts[g], xw[:, cols], preferred_element_type=F32)
            hrun_ref[:, cols] = hrun_ref[:, cols] * decay[:, cols] + s_g

    def bwd_chunk(i, carry):
        ci = cps - 1 - i
        c = blk * cps + ci
        rows = pl.ds(pl.multiple_of(ci * L, L), L)
        dt, ldt, cs, ecs, tot = scalars(rows)
        hb_ref[c] = hrun_ref[...].astype(BF16)
        xs = xa_ref[0, rows, 0:D_INNER].astype(F32)
        bm = xa_ref[0, rows, D_INNER:D_INNER + n_bc].astype(F32)
        bts = [bm[:, g * D_STATE:(g + 1) * D_STATE].T.astype(BF16) for g in range(N_SSM_GROUPS)]
        bt_c[c] = jnp.concatenate(bts, axis=1)
        state_update(xs, bts, jnp.exp2(ecs + ldt), tot, eb_ref)
        return carry

    def fwd_chunk(ci, carry):
        c = blk * cps + ci
        rows = pl.ds(pl.multiple_of(ci * L, L), L)
        dt, ldt, cs, ecs, tot = scalars(rows)
        xs_b = xa_ref[0, rows, 0:D_INNER]
        xs = xs_b.astype(F32)
        bts = [bt_c[c, :, g * D_STATE:(g + 1) * D_STATE] for g in range(N_SSM_GROUPS)]
        hf_in = hrun_ref[...].astype(BF16)
        hb_in = hb_ref[c]
        scale_f = _split_dot(jnp.exp2(cs), ef_ref[...], 1)
        scale_b = _split_dot(jnp.exp2(tot - ecs), eb_ref[...], 1)
        row_f = (cs - ldt).T
        row_b = (ecs + ldt).T
        dt_t = dt.T
        lane = lax.broadcasted_iota(jnp.int32, (L, LANES), 1)
        even = lane < HEAD_DIM
        below = si < li
        diag = si == li
        hpg = N_SSM_HEADS // N_SSM_GROUPS
        y_groups = []
        for g in range(N_SSM_GROUPS):
            cols = slice(g * gw, (g + 1) * gw)
            cg = xa_ref[0, rows, D_INNER + n_bc + g * D_STATE:D_INNER + n_bc + (g + 1) * D_STATE]
            bg = xa_ref[0, rows, D_INNER + g * D_STATE:D_INNER + (g + 1) * D_STATE]
            gmat = lax.dot_general(cg, bg, (((1,), (1,)), ((), ())), preferred_element_type=F32)
            y_off = (scale_f[:, cols] * jnp.dot(cg, hf_in[:, cols], preferred_element_type=F32)
                     + scale_b[:, cols] * jnp.dot(cg, hb_in[:, cols], preferred_element_type=F32))
            pairs = []
            for pr in range(hpg // 2):
                h0 = g * hpg + 2 * pr
                xs_pair = xs_b[:, h0 * HEAD_DIM:(h0 + 2) * HEAD_DIM]
                res = []
                for h in (h0, h0 + 1):
                    hb = N_SSM_HEADS + h
                    sel = jnp.where(below, cs[:, h:h + 1] - row_f[h:h + 1, :],
                                    row_b[hb:hb + 1, :] - ecs[:, hb:hb + 1])
                    e = jnp.exp2(sel) + jnp.where(diag, dt_t[h:h + 1, :], 0.0)
                    w = (gmat * e).astype(BF16)
                    res.append(jnp.dot(w, xs_pair, preferred_element_type=F32))
                pairs.append(jnp.where(even, res[0], res[1]))
            y_groups.append(jnp.concatenate(pairs, axis=1) + y_off)
        y = jnp.concatenate(y_groups, axis=1) + dsk_ref[...] * xs
        zf = z_ref[0, rows, :].astype(F32)
        gated = y * _silu(zf)
        outs = []
        for g in range(N_SSM_GROUPS):
            gg = gated[:, g * gw:(g + 1) * gw]
            outs.append(gg * lax.rsqrt(jnp.mean(gg * gg, axis=-1, keepdims=True) + EPS))
        y_ref[0, rows, :] = (jnp.concatenate(outs, axis=1) * nw_ref[...]).astype(BF16)
        state_update(xs, bts, jnp.exp2(tot - cs + ldt), tot, ef_ref)
        return carry

    @pl.when(is_bwd)
    def _():
        lax.fori_loop(0, cps, bwd_chunk, 0)

    @pl.when(jnp.logical_not(is_bwd))
    def _():
        lax.fori_loop(0, cps, fwd_chunk, 0)


SSM_CHUNKS_PER_STEP = 4


def _head_expanders():
    col_head = np.arange(D_INNER) // HEAD_DIM
    rows = np.arange(LANES)[:, None]
    ef = (rows == col_head[None, :]).astype(np.float32)
    eb = (rows == col_head[None, :] + N_SSM_HEADS).astype(np.float32)
    return jnp.asarray(ef, BF16), jnp.asarray(eb, BF16)


def _ssm_mixer(z, xa, dt, ldt, cs, a_log_f, a_log_b, d_skip, norm_w):
    b, s, _ = z.shape
    n = s // CHUNK
    pad = LANES - 2 * N_SSM_HEADS
    alog = jnp.pad(jnp.concatenate([a_log_f, a_log_b]), (0, pad)).reshape(1, LANES)
    dsk = jnp.repeat(d_skip, HEAD_DIM).reshape(1, D_INNER)
    ef, eb = _head_expanders()

    rows = SSM_CHUNKS_PER_STEP * CHUNK
    n_steps = s // rows

    def block_of(st):
        return jnp.where(st < n_steps, n_steps - 1 - st, st - n_steps)

    per_chunk = pl.BlockSpec((1, rows, LANES), lambda bi, st: (bi, block_of(st), 0))
    fwd_only = pl.BlockSpec((1, rows, D_INNER), lambda bi, st: (bi, jnp.maximum(st - n_steps, 0), 0))
    return pl.pallas_call(
        functools.partial(_ssm_kernel, n_steps=n_steps),
        grid=(b, 2 * n_steps),
        in_specs=[
            pl.BlockSpec((1, rows, XBC_WIDTH), lambda bi, st: (bi, block_of(st), 0)),
            per_chunk, per_chunk, per_chunk, fwd_only,
            _const_spec((1, LANES)), _const_spec((1, D_INNER)), _const_spec((1, D_INNER)),
            _const_spec((LANES, D_INNER)), _const_spec((LANES, D_INNER)),
        ],
        out_specs=fwd_only,
        out_shape=jax.ShapeDtypeStruct((b, s, D_INNER), BF16),
        scratch_shapes=[pltpu.VMEM((n, D_STATE, D_INNER), BF16),
                        pltpu.VMEM((n, D_STATE, N_SSM_GROUPS * CHUNK), BF16),
                        pltpu.VMEM((D_STATE, D_INNER), F32)],
        compiler_params=_params("arbitrary", "arbitrary"),
        name="ssm_mixer",
    )(xa, dt, ldt, cs, z, alog, dsk, norm_w.reshape(1, D_INNER), ef, eb)


def _outproj_kernel(x_ref, a_ref, s_ref, wa_ref, ws_ref, nw_ref, x1_ref, h_ref, slab_ref):
    tm = x_ref.shape[0]
    dil = a_ref.shape[1]
    n_cb = ATTN_WIDTH // LANES
    for r in range(dil):
        blk = a_ref[0, r].astype(F32)
        for cb in range(n_cb):
            slab_ref[cb, pl.ds(r, tm // dil, stride=dil), :] = blk[:, cb * LANES:(cb + 1) * LANES]
    attn = jnp.concatenate([slab_ref[cb] for cb in range(n_cb)], axis=1).astype(BF16)
    x1 = (x_ref[...] + jnp.dot(attn, wa_ref[...], preferred_element_type=F32)
          + jnp.dot(s_ref[...], ws_ref[...], preferred_element_type=F32))
    x1_ref[...] = x1
    h_ref[...] = _rms(x1, nw_ref[...]).astype(BF16)


def _out_projection(x2d, attn_planes, ssm, w_out, norm_w, seq, tm=512):
    t_rows = x2d.shape[0]
    dil = attn_planes.shape[1]
    nseq = seq // tm
    row = pl.BlockSpec((tm, D_MODEL), lambda i: (i, 0))
    planes = pl.BlockSpec((1, dil, tm // dil, ATTN_WIDTH), lambda i: (i // nseq, 0, i % nseq, 0))
    wa = w_out[:ATTN_WIDTH].astype(BF16)
    ws = w_out[ATTN_WIDTH:].astype(BF16)
    return pl.pallas_call(
        _outproj_kernel,
        grid=(t_rows // tm,),
        in_specs=[row, planes, row, _const_spec((ATTN_WIDTH, D_MODEL)), _const_spec((D_INNER, D_MODEL)),
                  _const_spec((1, D_MODEL))],
        out_specs=[row, row],
        out_shape=[jax.ShapeDtypeStruct((t_rows, D_MODEL), F32),
                   jax.ShapeDtypeStruct((t_rows, D_MODEL), BF16)],
        scratch_shapes=[pltpu.VMEM((ATTN_WIDTH // LANES, tm, LANES), F32)],
        compiler_params=_params("parallel"),
        name="out_projection",
    )(x2d, attn_planes, ssm, wa, ws, norm_w.reshape(1, D_MODEL))


FFN_COLS = 256
FFN_HALO = BF16_ROWS
FFN_SLOTS = 2


def _ffn_up_kernel(hc_ref, hp_ref, hn_ref, w_ref, cw_ref, cb_ref, act_ref, lhs_scr, u_scr, *,
                   blocks_per_seq):
    tm = hc_ref.shape[0]
    pos = pl.program_id(0) % blocks_per_seq
    rows = tm + 2 * FFN_HALO
    lhs_scr[0:FFN_HALO, :] = jnp.where(pos > 0, hp_ref[...], jnp.zeros_like(hp_ref))
    lhs_scr[FFN_HALO:FFN_HALO + tm, :] = hc_ref[...]
    lhs_scr[FFN_HALO + tm:rows, :] = jnp.where(pos < blocks_per_seq - 1, hn_ref[...],
                                               jnp.zeros_like(hn_ref))

    n_chunks = D_FF // FFN_COLS

    def chunk_cols(j, half):
        return pl.ds(pl.multiple_of(j * FFN_COLS + half * D_FF, LANES), FFN_COLS)

    def project(j, slot):
        for half in range(2):
            u_scr[2 * slot + half] = jnp.dot(lhs_scr[...], w_ref[:, chunk_cols(j, half)],
                                             preferred_element_type=F32)

    def conv(j, slot, half):
        return _conv3_rows(u_scr[2 * slot + half], cw_ref[:, chunk_cols(j, half)],
                           cb_ref[:, chunk_cols(j, half)], FFN_HALO, tm)

    def finish(j, slot):
        act_ref[:, chunk_cols(j, 0)] = (_silu(conv(j, slot, 0)) * conv(j, slot, 1)).astype(BF16)

    _software_pipeline(n_chunks, project, finish, depth=1, slots=FFN_SLOTS, rolled=True)


def _ffn_up(h, w_up, conv_w, conv_b, seq, tm=512):
    t_rows = h.shape[0]
    hpb = tm // FFN_HALO
    last_halo = t_rows // FFN_HALO - 1
    width = 2 * D_FF
    return pl.pallas_call(
        functools.partial(_ffn_up_kernel, blocks_per_seq=seq // tm),
        grid=(t_rows // tm,),
        in_specs=[
            pl.BlockSpec((tm, D_MODEL), lambda i: (i, 0)),
            pl.BlockSpec((FFN_HALO, D_MODEL), lambda i: (jnp.maximum(i * hpb - 1, 0), 0)),
            pl.BlockSpec((FFN_HALO, D_MODEL), lambda i: (jnp.minimum((i + 1) * hpb, last_halo), 0)),
            _const_spec((D_MODEL, width)), _const_spec((3, width)), _const_spec((1, width)),
        ],
        out_specs=pl.BlockSpec((tm, D_FF), lambda i: (i, 0)),
        out_shape=jax.ShapeDtypeStruct((t_rows, D_FF), BF16),
        scratch_shapes=[pltpu.VMEM((tm + 2 * FFN_HALO, D_MODEL), BF16),
                        pltpu.VMEM((2 * FFN_SLOTS, tm + 2 * FFN_HALO, FFN_COLS), F32)],
        compiler_params=_params("parallel"),
        name="ffn_up",
    )(h, h, h, w_up.astype(BF16), conv_w.T, conv_b.reshape(1, width))


def _ffn_down_kernel(a_ref, x1_ref, wd_ref, nw_ref, o_ref):
    acc = x1_ref[...] + jnp.dot(a_ref[...], wd_ref[...], preferred_element_type=F32)
    o_ref[...] = _rms(acc, nw_ref[...])


def _ffn_down(act, x1, w_down, norm_w, tm=512):
    t_rows = act.shape[0]
    return pl.pallas_call(
        _ffn_down_kernel,
        grid=(t_rows // tm,),
        in_specs=[pl.BlockSpec((tm, D_FF), lambda i: (i, 0)), pl.BlockSpec((tm, D_MODEL), lambda i: (i, 0)),
                  _const_spec((D_FF, D_MODEL)), _const_spec((1, D_MODEL))],
        out_specs=pl.BlockSpec((tm, D_MODEL), lambda i: (i, 0)),
        out_shape=jax.ShapeDtypeStruct((t_rows, D_MODEL), F32),
        compiler_params=_params("parallel"),
        name="ffn_down",
    )(act, x1, w_down.astype(BF16), norm_w.reshape(1, D_MODEL))


def kernel(x, norm1_w, w_in, ssm_conv_w, ssm_conv_b, a_log_f, a_log_b, dt_bias_f, dt_bias_b, d_skip,
           ssm_norm_w, w_out, norm2_w, w_up, ffn_conv_w, ffn_conv_b, w_down, final_norm_w):
    b, s, d = x.shape
    depth = w_in.shape[0]
    x2d = x.reshape(b * s, d)
    for layer in range(depth):
        (q4, k4, v4, q16, k16, v16, z) = _in_projection(x2d, norm1_w[layer], w_in[layer], b, s)
        xa, dt, ldt, cs = _ssm_projection(
            x2d, norm1_w[layer], w_in[layer], ssm_conv_w[layer], ssm_conv_b[layer],
            a_log_f[layer], a_log_b[layer], dt_bias_f[layer], dt_bias_b[layer], s)
        attn = _dilated_attention([(q4, k4, v4), (q4, k4, v4), (q16, k16, v16)])
        sh = lambda t: t.reshape(b, s, t.shape[-1])
        ssm = _ssm_mixer(sh(z), sh(xa), sh(dt), sh(ldt), sh(cs), a_log_f[layer], a_log_b[layer],
                         d_skip[layer], ssm_norm_w[layer])
        x1, h2 = _out_projection(x2d, attn, ssm.reshape(b * s, -1), w_out[layer], norm2_w[layer], s)
        act = _ffn_up(h2, w_up[layer], ffn_conv_w[layer], ffn_conv_b[layer], s)
        assert depth == 1
        x2d = _ffn_down(act, x1, w_down[layer], final_norm_w)
    return x2d.reshape(b, s, d)
```

```python
import functools

import numpy as np
import jax
import jax.numpy as jnp
from jax import lax
from jax.experimental import pallas as pl
from jax.experimental.pallas import tpu as pltpu

F32 = jnp.float32
BF16 = jnp.bfloat16

D_MODEL = 1024
HEAD_DIM = 64
N_ATTN_HEADS = 16
ATTN_WIDTH = N_ATTN_HEADS * HEAD_DIM
ROPE_DIM = HEAD_DIM // 4
ROPE_THETA = 500000.0
DILATIONS = (1, 4, 16)
BAND_HALF = 64
CARRY_SPLIT = 4

D_INNER = 1024
N_SSM_HEADS = 16
N_SSM_GROUPS = 4
D_STATE = 128
CHUNK = 128
XBC_WIDTH = D_INNER + 2 * N_SSM_GROUPS * D_STATE
D_FF = 2816
EPS = 1e-6

LANES = 128
BF16_ROWS = 16
VMEM_LIMIT = 56 * 1024 * 1024
NEG_BIG = -1e30
LOG2E = 1.4426950408889634


def _params(*sem):
    return pltpu.CompilerParams(dimension_semantics=sem, vmem_limit_bytes=VMEM_LIMIT)


def _const_spec(shape):
    return pl.BlockSpec(shape, lambda *_: (0,) * len(shape))


def _rms(x, w):
    return x * lax.rsqrt(jnp.mean(x * x, axis=-1, keepdims=True) + EPS) * w


def _silu(y):
    h = 0.5 * y
    return h + h * jnp.tanh(h)


def _conv3_rows(u, w, b, halo, rows):
    r = u.shape[0]
    y = pltpu.roll(u, 1, 0) * w[0:1] + u * w[1:2] + pltpu.roll(u, r - 1, 0) * w[2:3] + b
    return y[halo:halo + rows]


def _software_pipeline(n_chunks, project, finish, depth, slots, rolled):
    for k in range(min(depth, n_chunks)):
        project(k, k % slots)
    n_rolled = max(n_chunks - depth, 0) // slots if rolled else 0

    def body(i, carry):
        k0 = i * slots
        for s in range(slots):
            project(k0 + s + depth, (s + depth) % slots)
            finish(k0 + s, s)
        return carry

    if n_rolled:
        lax.fori_loop(0, n_rolled, body, 0)
    for k in range(n_rolled * slots, n_chunks):
        if k + depth < n_chunks:
            project(k + depth, (k + depth) % slots)
        finish(k, k % slots)


def _inproj_kernel(x_ref, nw_ref, wq_ref, wk_ref, wv_ref, wz_ref, rc_ref, rs1_ref, rs2_ref,
                   q4_ref, k4_ref, v4_ref, q16_ref, k16_ref, v16_ref, z_ref, slab_ref, slab2_ref):
    tm = x_ref.shape[0]
    h = _rms(x_ref[...], nw_ref[...]).astype(BF16)
    rc, rs1, rs2 = rc_ref[...], rs1_ref[...], rs2_ref[...]
    n_cb = ATTN_WIDTH // LANES

    def emit(idx, w_ref, out_refs, rope):
        t = jnp.dot(h, w_ref[...], preferred_element_type=F32)
        for cb in range(n_cb):
            cols = slice(cb * LANES, (cb + 1) * LANES)
            blk = t[:, cols]
            if rope:
                lo = pltpu.roll(blk, ROPE_DIM // 2, 1)
                hi = pltpu.roll(blk, LANES - ROPE_DIM // 2, 1)
                blk = blk * rc + lo * rs1 + hi * rs2
            slab = slab_ref.at[idx * n_cb + cb]
            slab2 = slab2_ref.at[idx * n_cb + cb]
            slab[...] = blk
            n4 = tm // CARRY_SPLIT
            n16 = n4 // CARRY_SPLIT
            for r4 in range(CARRY_SPLIT):
                p4 = slab[pl.ds(r4, n4, stride=CARRY_SPLIT), :]
                out_refs[0][0, r4, :, cols] = p4.astype(BF16)
                slab2[r4 * n4:(r4 + 1) * n4, :] = p4
                for j in range(CARRY_SPLIT):
                    out_refs[1][0, CARRY_SPLIT * j + r4, :, cols] = (
                        slab2[pl.ds(r4 * n4 + j, n16, stride=CARRY_SPLIT), :].astype(BF16))

    emit(0, wq_ref, (q4_ref, q16_ref), True)
    emit(1, wk_ref, (k4_ref, k16_ref), True)
    emit(2, wv_ref, (v4_ref, v16_ref), False)
    z_ref[...] = jnp.dot(h, wz_ref[...], preferred_element_type=F32).astype(BF16)


def _rope_tables(seq):
    half = ROPE_DIM // 2
    inv_freq = jnp.power(ROPE_THETA, -jnp.arange(half, dtype=F32) * 2.0 / ROPE_DIM)
    ang = jnp.arange(seq, dtype=F32)[:, None] * inv_freq[None, :]
    cos, sin = jnp.cos(ang), jnp.sin(ang)
    one = jnp.ones((seq, HEAD_DIM - ROPE_DIM), F32)
    zero8 = jnp.zeros((seq, half), F32)
    zero = jnp.zeros((seq, HEAD_DIM - ROPE_DIM), F32)
    rc = jnp.concatenate([cos, cos, one], axis=1)
    rs1 = jnp.concatenate([zero8, sin, zero], axis=1)
    rs2 = jnp.concatenate([-sin, zero8, zero], axis=1)
    rep = LANES // HEAD_DIM
    return tuple(jnp.tile(t, (1, rep)) for t in (rc, rs1, rs2))


def _in_projection(x2d, norm_w, w_in, batch, seq, tm=512):
    t_rows = x2d.shape[0]
    a = ATTN_WIDTH
    wq = (w_in[:, :a] * (HEAD_DIM ** -0.5 * LOG2E)).astype(BF16)
    wk = w_in[:, a:2 * a].astype(BF16)
    wv = w_in[:, 2 * a:3 * a].astype(BF16)
    wz = w_in[:, 3 * a:3 * a + D_INNER].astype(BF16)
    rc, rs1, rs2 = _rope_tables(seq)
    nseq = seq // tm
    row = lambda width: pl.BlockSpec((tm, width), lambda i: (i, 0))
    tab = pl.BlockSpec((tm, LANES), lambda i: (i % nseq, 0))
    plane = lambda dil: pl.BlockSpec((1, dil, tm // dil, a), lambda i: (i // nseq, 0, i % nseq, 0))
    plane_shape = lambda dil: jax.ShapeDtypeStruct((batch, dil, seq // dil, a), BF16)
    d4, d16 = DILATIONS[1:]
    return pl.pallas_call(
        _inproj_kernel,
        grid=(t_rows // tm,),
        in_specs=[row(D_MODEL), _const_spec((1, D_MODEL)),
                  _const_spec((D_MODEL, a)), _const_spec((D_MODEL, a)), _const_spec((D_MODEL, a)),
                  _const_spec((D_MODEL, D_INNER)), tab, tab, tab],
        out_specs=[plane(d4)] * 3 + [plane(d16)] * 3 + [row(D_INNER)],
        out_shape=[plane_shape(d4)] * 3 + [plane_shape(d16)] * 3
        + [jax.ShapeDtypeStruct((t_rows, D_INNER), BF16)],
        scratch_shapes=[pltpu.VMEM((3 * a // LANES, tm, LANES), F32)] * 2,
        compiler_params=_params("parallel"),
        name="in_projection",
    )(x2d, norm_w.reshape(1, D_MODEL), wq, wk, wv, wz, rc, rs1, rs2)


SSM_HALO = BF16_ROWS
SSM_COLS = 256
SSM_SLOTS = 4


def _ssm_proj_kernel(xc_ref, xp_ref, xn_ref, nw_ref, wx_ref, wdt_ref, cw_ref, cb_ref, alog_ref, dtb_ref,
                     xa_ref, dt_ref, ldt_ref, cs_ref, lhs_scr, u_scr, *, blocks_per_seq):
    tm = xc_ref.shape[0]
    pos = pl.program_id(0) % blocks_per_seq
    nw = nw_ref[...]
    rows = tm + 2 * SSM_HALO
    h = _rms(xc_ref[...], nw).astype(BF16)
    zero_halo = jnp.zeros((SSM_HALO, D_MODEL), BF16)
    lhs_scr[0:SSM_HALO, :] = jnp.where(pos > 0, _rms(xp_ref[...], nw).astype(BF16), zero_halo)
    lhs_scr[SSM_HALO:SSM_HALO + tm, :] = h
    lhs_scr[SSM_HALO + tm:rows, :] = jnp.where(pos < blocks_per_seq - 1,
                                               _rms(xn_ref[...], nw).astype(BF16), zero_halo)
    n_chunks = XBC_WIDTH // SSM_COLS

    def chunk_cols(j):
        return pl.ds(pl.multiple_of(j * SSM_COLS, LANES), SSM_COLS)

    def project(j, slot):
        u_scr[slot] = jnp.dot(lhs_scr[...], wx_ref[:, chunk_cols(j)], preferred_element_type=F32)

    def finish(j, slot):
        y = _conv3_rows(u_scr[slot], cw_ref[:, chunk_cols(j)], cb_ref[:, chunk_cols(j)], SSM_HALO, tm)
        xa_ref[:, chunk_cols(j)] = _silu(y).astype(BF16)

    _software_pipeline(n_chunks, project, finish, depth=2, slots=SSM_SLOTS, rolled=False)

    x_dt = jnp.dot(h, wdt_ref[...], preferred_element_type=F32) + dtb_ref[...]
    dt = jnp.maximum(x_dt, 0.0) + jnp.log1p(jnp.exp(-jnp.abs(x_dt)))
    a = dt * (-jnp.exp(alog_ref[...]) * LOG2E)
    tri = (lax.broadcasted_iota(jnp.int32, (CHUNK, CHUNK), 1)
           <= lax.broadcasted_iota(jnp.int32, (CHUNK, CHUNK), 0)).astype(BF16)
    dt_ref[...] = dt
    ldt_ref[...] = jnp.log(dt) * LOG2E
    for ch in range(tm // CHUNK):
        rows_c = slice(ch * CHUNK, (ch + 1) * CHUNK)
        cs_ref[rows_c, :] = _split_dot_lhs_const(tri, a[rows_c])


def _ssm_projection(x2d, norm_w, w_in, conv_w, conv_b, a_log_f, a_log_b, dt_bias_f, dt_bias_b,
                    seq, tm=1024):
    t_rows = x2d.shape[0]
    pad = LANES - 2 * N_SSM_HEADS
    alog = jnp.pad(jnp.concatenate([a_log_f, a_log_b]), (0, pad)).reshape(1, LANES)
    dtb = jnp.pad(jnp.concatenate([dt_bias_f, dt_bias_b]), (0, pad)).reshape(1, LANES)
    o = 3 * ATTN_WIDTH + D_INNER
    wx = w_in[:, o:o + XBC_WIDTH].astype(BF16)
    wdt = jnp.pad(w_in[:, o + XBC_WIDTH:], ((0, 0), (0, pad))).astype(BF16)
    hpb = tm // SSM_HALO
    last_halo = t_rows // SSM_HALO - 1
    row = lambda width: pl.BlockSpec((tm, width), lambda i: (i, 0))
    return pl.pallas_call(
        functools.partial(_ssm_proj_kernel, blocks_per_seq=seq // tm),
        grid=(t_rows // tm,),
        in_specs=[row(D_MODEL),
                  pl.BlockSpec((SSM_HALO, D_MODEL), lambda i: (jnp.maximum(i * hpb - 1, 0), 0)),
                  pl.BlockSpec((SSM_HALO, D_MODEL), lambda i: (jnp.minimum((i + 1) * hpb, last_halo), 0)),
                  _const_spec((1, D_MODEL)), _const_spec((D_MODEL, XBC_WIDTH)),
                  _const_spec((D_MODEL, LANES)),
                  _const_spec((3, XBC_WIDTH)), _const_spec((1, XBC_WIDTH)),
                  _const_spec((1, LANES)), _const_spec((1, LANES))],
        out_specs=[row(XBC_WIDTH), row(LANES), row(LANES), row(LANES)],
        out_shape=[jax.ShapeDtypeStruct((t_rows, XBC_WIDTH), BF16)]
        + [jax.ShapeDtypeStruct((t_rows, LANES), F32)] * 3,
        scratch_shapes=[pltpu.VMEM((tm + 2 * SSM_HALO, D_MODEL), BF16),
                        pltpu.VMEM((SSM_SLOTS, tm + 2 * SSM_HALO, SSM_COLS), F32)],
        compiler_params=_params("parallel"),
        name="ssm_projection",
    )(x2d, x2d, x2d, norm_w.reshape(1, D_MODEL), wx, wdt, conv_w.T, conv_b.reshape(1, XBC_WIDTH),
      alog, dtb)


ATT_TQ = 128
ATT_TK = ATT_TQ + 2 * BAND_HALF
ATT_UNITS_PER_STEP = 32
ATT_STAT_PAIRS = 4
ATT_QUERIES_PER_STEP = 1024


def _attn_kernel(*refs, length, n_sub, n_pairs, first, last):
    interleaved = first
    q_ref, k_ref, v_ref = refs[:3]
    if first:
        o_ref, st_ref = refs[3:]
    elif last:
        op_ref, sp_ref, o_ref = refs[3:]
    else:
        op_ref, sp_ref, o_ref, st_ref, slab_ref = refs[3:]
    tq, tk = ATT_TQ, ATT_TK
    qi = pl.program_id(3)
    lane = lax.broadcasted_iota(jnp.int32, (tq, LANES), 1)
    even = lane < HEAD_DIM
    stat_is_max = (lane & 15) < 8
    row_i = lax.broadcasted_iota(jnp.int32, (tq, tk), 0)
    col_i = lax.broadcasted_iota(jnp.int32, (tq, tk), 1)
    nt = (((1,), (1,)), ((), ()))
    ones = jnp.ones((tk, LANES), BF16)
    sub_rows = tq // CARRY_SPLIT
    key_rows = tk // CARRY_SPLIT

    if interleaved:
        delta = (CARRY_SPLIT * ((col_i & (key_rows - 1)) - (row_i & (sub_rows - 1)))
                 + (col_i // key_rows - row_i // sub_rows))
    else:
        delta = col_i - row_i

    def window_start(sb):
        if interleaved:
            a0 = (qi * n_sub + sb) * sub_rows
            ws = pl.multiple_of(jnp.clip(a0 - BAND_HALF // CARRY_SPLIT, 0, length - key_rows),
                                BF16_ROWS)
            return ws, CARRY_SPLIT * (ws - a0)
        q0 = (qi * n_sub + sb) * tq
        ws = pl.multiple_of(jnp.clip(q0 - BAND_HALF, 0, length - tk), BAND_HALF)
        return ws, ws - q0

    def load_q(sb, cols):
        if interleaved:
            return jnp.concatenate([q_ref[0, j, sb * sub_rows:(sb + 1) * sub_rows, cols]
                                    for j in range(CARRY_SPLIT)], axis=0)
        return q_ref[0, 0, sb * tq:(sb + 1) * tq, cols]

    def load_keys(ref, ws, cols):
        if interleaved:
            return jnp.concatenate([ref[0, j, pl.ds(ws, key_rows), cols]
                                    for j in range(CARRY_SPLIT)], axis=0)
        return ref[0, 0, pl.ds(ws, tk), cols]

    scores, row_max = [], []
    for sb in range(n_sub):
        ws, offset = window_start(sb)
        bias = jnp.where(jnp.abs(delta + offset) <= BAND_HALF, 0.0, NEG_BIG)
        bias2 = jnp.concatenate([bias, bias], axis=0)
        for hp in range(n_pairs):
            cols = slice(hp * LANES, (hp + 1) * LANES)
            q2 = load_q(sb, cols)
            zero = jnp.zeros_like(q2)
            qq = jnp.concatenate([jnp.where(even, q2, zero), jnp.where(even, zero, q2)], axis=0)
            k2 = load_keys(k_ref, ws, cols)
            s = lax.dot_general(qq, k2, nt, preferred_element_type=F32) + bias2
            scores.append(s)
            row_max.append(jnp.max(s, axis=-1, keepdims=True))

    for sb in range(n_sub):
        ws, _ = window_start(sb)
        rows = slice(sb * tq, (sb + 1) * tq)
        out_rows = slice(sb * sub_rows, (sb + 1) * sub_rows)
        stats = jnp.zeros((tq, LANES), F32)
        for hp in range(n_pairs):
            u = sb * n_pairs + hp
            cols = slice(hp * LANES, (hp + 1) * LANES)
            v_ext = jnp.concatenate([load_keys(v_ref, ws, cols), ones], axis=1)
            p = jnp.exp2(scores[u] - row_max[u]).astype(BF16)
            pv = jnp.dot(p, v_ext, preferred_element_type=F32)
            acc = jnp.where(even, pv[:tq, :LANES], pv[tq:, :LANES])
            l = jnp.where(even, pv[:tq, LANES:], pv[tq:, LANES:])
            m = jnp.where(even, row_max[u][:tq], row_max[u][tq:])
            if not first:
                sp = sp_ref[0, 0, rows, :]
                be = LANES * (hp // ATT_STAT_PAIRS) + 16 * (hp % ATT_STAT_PAIRS)
                bo = be + HEAD_DIM
                m_prev = jnp.where(even, sp[:, be:be + 1], sp[:, bo:bo + 1])
                l_prev = jnp.where(even, sp[:, be + 8:be + 9], sp[:, bo + 8:bo + 9])
                acc_prev = op_ref[0, 0, rows, cols].astype(F32)
                m_new = jnp.maximum(m_prev, m)
                a_prev = jnp.exp2(m_prev - m_new)
                a_cur = jnp.exp2(m - m_new)
                acc = acc_prev * a_prev + acc * a_cur
                l = l_prev * a_prev + l * a_cur
                m = m_new
            if last:
                o_ref[0, 0, rows, cols] = (acc / l).astype(BF16)
                continue
            in_zone = ((lane & (HEAD_DIM - 1)) >> 4) == hp
            stats = jnp.where(in_zone, jnp.where(stat_is_max, m, l), stats)
            if interleaved:
                for j in range(CARRY_SPLIT):
                    o_ref[0, j, out_rows, cols] = acc[j * sub_rows:(j + 1) * sub_rows].astype(BF16)
            else:
                slab_ref[hp] = acc
                for j in range(CARRY_SPLIT):
                    o_ref[0, j, 0, out_rows, cols] = (
                        slab_ref[hp, pl.ds(j, sub_rows, stride=CARRY_SPLIT), :].astype(BF16))
        if last:
            continue
        if interleaved:
            for j in range(CARRY_SPLIT):
                st_ref[0, j, out_rows, :] = stats[j * sub_rows:(j + 1) * sub_rows]
        else:
            slab_ref[n_pairs] = stats
            for j in range(CARRY_SPLIT):
                st_ref[0, j, 0, out_rows, :] = (
                    slab_ref[n_pairs, pl.ds(j, sub_rows, stride=CARRY_SPLIT), :])


def _attention_pattern(q, k, v, o_prev, st_prev, first, last):
    cs = CARRY_SPLIT
    if first:
        b, _, length, width = q.shape
        dil, qs = 1, min(length, ATT_QUERIES_PER_STEP // cs)
        n_sub = qs * cs // ATT_TQ
    else:
        b, dil, length, width = q.shape
        qs = min(length, ATT_QUERIES_PER_STEP)
        n_sub = qs // ATT_TQ
    n_pairs = min(ATT_UNITS_PER_STEP // n_sub, width // LANES)
    hw = n_pairs * LANES
    n_hg = width // hw
    n_stat = n_pairs // ATT_STAT_PAIRS
    assert last or n_stat == 1
    scratch = []
    if first:
        blk = pl.BlockSpec((1, cs, qs, hw), lambda bi, r, g, qi: (bi, 0, qi, g))
        seq = pl.BlockSpec((1, cs, length, hw), lambda bi, r, g, qi: (bi, 0, 0, g))
        stat = pl.BlockSpec((1, cs, qs, LANES), lambda bi, r, g, qi: (bi, 0, qi, g))
    else:
        blk = pl.BlockSpec((1, 1, qs, hw), lambda bi, r, g, qi: (bi, r, qi, g))
        seq = pl.BlockSpec((1, 1, length, hw), lambda bi, r, g, qi: (bi, r, 0, g))
        stat = pl.BlockSpec((1, 1, qs, n_stat * LANES), lambda bi, r, g, qi: (bi, r, qi, g))
    in_specs, args = [blk, seq, seq], [q, k, v]
    if not first:
        in_specs += [blk, stat]
        args += [o_prev, st_prev]
    if first:
        out_specs = [blk, stat]
        out_shape = [jax.ShapeDtypeStruct((b, cs, length, width), BF16),
                     jax.ShapeDtypeStruct((b, cs, length, n_hg * LANES), F32)]
    elif last:
        out_specs = [blk]
        out_shape = [jax.ShapeDtypeStruct((b, dil, length, width), BF16)]
    else:
        carry = lambda w: pl.BlockSpec((1, cs, 1, qs // cs, w), lambda bi, r, g, qi: (bi, 0, r, qi, g))
        out_specs = [carry(hw), carry(LANES)]
        out_shape = [jax.ShapeDtypeStruct((b, cs, dil, length // cs, width), BF16),
                     jax.ShapeDtypeStruct((b, cs, dil, length // cs, n_hg * LANES), F32)]
        scratch.append(pltpu.VMEM((n_pairs + 1, ATT_TQ, LANES), F32))
    outs = pl.pallas_call(
        functools.partial(_attn_kernel, length=length, n_sub=n_sub, n_pairs=n_pairs, first=first,
                          last=last),
        grid=(b, dil, n_hg, length // qs),
        in_specs=in_specs, out_specs=out_specs, out_shape=out_shape,
        scratch_shapes=scratch,
        compiler_params=_params("parallel", "parallel", "parallel", "arbitrary"),
        name=f"attention_dil{dil}",
    )(*args)
    if last:
        return outs[0], None
    if first:
        return outs[0], outs[1]
    nxt = lambda t: t.reshape(b, cs * dil, length // cs, t.shape[-1])
    return nxt(outs[0]), nxt(outs[1])


def _dilated_attention(qkv_planes):
    o = st = None
    for i, (q, k, v) in enumerate(qkv_planes):
        o, st = _attention_pattern(q, k, v, o, st, i == 0, i == len(qkv_planes) - 1)
    return o


def _split_dot(v, mat, passes):
    out = None
    r = v
    for i in range(passes):
        piece = r.astype(BF16)
        term = jnp.dot(piece, mat, preferred_element_type=F32)
        out = term if out is None else out + term
        if i + 1 < passes:
            r = r - piece.astype(F32)
    return out


def _split_dot_lhs_const(mat, v):
    out = None
    r = v
    for i in range(3):
        piece = r.astype(BF16)
        term = jnp.dot(mat, piece, preferred_element_type=F32)
        out = term if out is None else out + term
        if i < 2:
            r = r - piece.astype(F32)
    return out


def _ssm_kernel(xa_ref, dt_ref, ldt_ref, cs_ref, z_ref, alog_ref, dsk_ref, nw_ref, ef_ref, eb_ref,
                y_ref, hb_ref, bt_c, hrun_ref, *, n_steps):
    L = CHUNK
    cps = SSM_CHUNKS_PER_STEP
    step = pl.program_id(1)
    is_bwd = step < n_steps
    blk = jnp.where(is_bwd, n_steps - 1 - step, step - n_steps)

    @pl.when((step == 0) | (step == n_steps))
    def _():
        hrun_ref[...] = jnp.zeros_like(hrun_ref)

    n_bc = N_SSM_GROUPS * D_STATE
    gw = D_INNER // N_SSM_GROUPS
    li = lax.broadcasted_iota(jnp.int32, (L, L), 0)
    si = lax.broadcasted_iota(jnp.int32, (L, L), 1)
    a_log2 = -jnp.exp(alog_ref[...]) * LOG2E

    def scalars(rows):
        dt, ldt, cs = dt_ref[0, rows, :], ldt_ref[0, rows, :], cs_ref[0, rows, :]
        return dt, ldt, cs, cs - dt * a_log2, cs[L - 1:L, :]

    def state_update(xs, bts, weights, tot, e_ref):
        xw = (_split_dot(weights, e_ref[...], 1) * xs).astype(BF16)
        decay = _split_dot(jnp.broadcast_to(jnp.exp2(tot), (8, LANES)), e_ref[...], 3)[0:1]
        for g in range(N_SSM_GROUPS):
            cols = slice(g * gw, (g + 1) * gw)
            s_g = jnp.dot(bts[g], xw[:, cols], preferred_element_type=F32)
            hrun_ref[:, cols] = hrun_ref[:, cols] * decay[:, cols] + s_g

    def bwd_chunk(i, carry):
        ci = cps - 1 - i
        c = blk * cps + ci
        rows = pl.ds(pl.multiple_of(ci * L, L), L)
        dt, ldt, cs, ecs, tot = scalars(rows)
        hb_ref[c] = hrun_ref[...].astype(BF16)
        xs = xa_ref[0, rows, 0:D_INNER].astype(F32)
        bm = xa_ref[0, rows, D_INNER:D_INNER + n_bc].astype(F32)
        bts = [bm[:, g * D_STATE:(g + 1) * D_STATE].T.astype(BF16) for g in range(N_SSM_GROUPS)]
        bt_c[c] = jnp.concatenate(bts, axis=1)
        state_update(xs, bts, jnp.exp2(ecs + ldt), tot, eb_ref)
        return carry

    def fwd_chunk(ci, carry):
        c = blk * cps + ci
        rows = pl.ds(pl.multiple_of(ci * L, L), L)
        dt, ldt, cs, ecs, tot = scalars(rows)
        xs_b = xa_ref[0, rows, 0:D_INNER]
        xs = xs_b.astype(F32)
        bts = [bt_c[c, :, g * D_STATE:(g + 1) * D_STATE] for g in range(N_SSM_GROUPS)]
        hf_in = hrun_ref[...].astype(BF16)
        hb_in = hb_ref[c]
        scale_f = _split_dot(jnp.exp2(cs), ef_ref[...], 1)
        scale_b = _split_dot(jnp.exp2(tot - ecs), eb_ref[...], 1)
        row_f = (cs - ldt).T
        row_b = (ecs + ldt).T
        dt_t = dt.T
        lane = lax.broadcasted_iota(jnp.int32, (L, LANES), 1)
        even = lane < HEAD_DIM
        below = si < li
        diag = si == li
        hpg = N_SSM_HEADS // N_SSM_GROUPS
        y_groups = []
        for g in range(N_SSM_GROUPS):
            cols = slice(g * gw, (g + 1) * gw)
            cg = xa_ref[0, rows, D_INNER + n_bc + g * D_STATE:D_INNER + n_bc + (g + 1) * D_STATE]
            bg = xa_ref[0, rows, D_INNER + g * D_STATE:D_INNER + (g + 1) * D_STATE]
            gmat = lax.dot_general(cg, bg, (((1,), (1,)), ((), ())), preferred_element_type=F32)
            y_off = (scale_f[:, cols] * jnp.dot(cg, hf_in[:, cols], preferred_element_type=F32)
                     + scale_b[:, cols] * jnp.dot(cg, hb_in[:, cols], preferred_element_type=F32))
            pairs = []
            for pr in range(hpg // 2):
                h0 = g * hpg + 2 * pr
                xs_pair = xs_b[:, h0 * HEAD_DIM:(h0 + 2) * HEAD_DIM]
                res = []
                for h in (h0, h0 + 1):
                    hb = N_SSM_HEADS + h
                    sel = jnp.where(below, cs[:, h:h + 1] - row_f[h:h + 1, :],
                                    row_b[hb:hb + 1, :] - ecs[:, hb:hb + 1])
                    e = jnp.exp2(sel) + jnp.where(diag, dt_t[h:h + 1, :], 0.0)
                    w = (gmat * e).astype(BF16)
                    res.append(jnp.dot(w, xs_pair, preferred_element_type=F32))
                pairs.append(jnp.where(even, res[0], res[1]))
            y_groups.append(jnp.concatenate(pairs, axis=1) + y_off)
        y = jnp.concatenate(y_groups, axis=1) + dsk_ref[...] * xs
        zf = z_ref[0, rows, :].astype(F32)
        gated = y * _silu(zf)
        outs = []
        for g in range(N_SSM_GROUPS):
            gg = gated[:, g * gw:(g + 1) * gw]
            outs.append(gg * lax.rsqrt(jnp.mean(gg * gg, axis=-1, keepdims=True) + EPS))
        y_ref[0, rows, :] = (jnp.concatenate(outs, axis=1) * nw_ref[...]).astype(BF16)
        state_update(xs, bts, jnp.exp2(tot - cs + ldt), tot, ef_ref)
        return carry

    @pl.when(is_bwd)
    def _():
        lax.fori_loop(0, cps, bwd_chunk, 0)

    @pl.when(jnp.logical_not(is_bwd))
    def _():
        lax.fori_loop(0, cps, fwd_chunk, 0)


SSM_CHUNKS_PER_STEP = 8


def _head_expanders():
    col_head = np.arange(D_INNER) // HEAD_DIM
    rows = np.arange(LANES)[:, None]
    ef = (rows == col_head[None, :]).astype(np.float32)
    eb = (rows == col_head[None, :] + N_SSM_HEADS).astype(np.float32)
    return jnp.asarray(ef, BF16), jnp.asarray(eb, BF16)


def _ssm_mixer(z, xa, dt, ldt, cs, a_log_f, a_log_b, d_skip, norm_w):
    b, s, _ = z.shape
    n = s // CHUNK
    pad = LANES - 2 * N_SSM_HEADS
    alog = jnp.pad(jnp.concatenate([a_log_f, a_log_b]), (0, pad)).reshape(1, LANES)
    dsk = jnp.repeat(d_skip, HEAD_DIM).reshape(1, D_INNER)
    ef, eb = _head_expanders()

    rows = SSM_CHUNKS_PER_STEP * CHUNK
    n_steps = s // rows

    def block_of(st):
        return jnp.where(st < n_steps, n_steps - 1 - st, st - n_steps)

    per_chunk = pl.BlockSpec((1, rows, LANES), lambda bi, st: (bi, block_of(st), 0))
    fwd_only = pl.BlockSpec((1, rows, D_INNER), lambda bi, st: (bi, jnp.maximum(st - n_steps, 0), 0))
    return pl.pallas_call(
        functools.partial(_ssm_kernel, n_steps=n_steps),
        grid=(b, 2 * n_steps),
        in_specs=[
            pl.BlockSpec((1, rows, XBC_WIDTH), lambda bi, st: (bi, block_of(st), 0)),
            per_chunk, per_chunk, per_chunk, fwd_only,
            _const_spec((1, LANES)), _const_spec((1, D_INNER)), _const_spec((1, D_INNER)),
            _const_spec((LANES, D_INNER)), _const_spec((LANES, D_INNER)),
        ],
        out_specs=fwd_only,
        out_shape=jax.ShapeDtypeStruct((b, s, D_INNER), BF16),
        scratch_shapes=[pltpu.VMEM((n, D_STATE, D_INNER), BF16),
                        pltpu.VMEM((n, D_STATE, N_SSM_GROUPS * CHUNK), BF16),
                        pltpu.VMEM((D_STATE, D_INNER), F32)],
        compiler_params=_params("arbitrary", "arbitrary"),
        name="ssm_mixer",
    )(xa, dt, ldt, cs, z, alog, dsk, norm_w.reshape(1, D_INNER), ef, eb)


def _outproj_kernel(x_ref, a_ref, s_ref, wa_ref, ws_ref, nw_ref, x1_ref, h_ref, slab_ref):
    tm = x_ref.shape[0]
    dil = a_ref.shape[1]
    n_cb = ATTN_WIDTH // LANES
    for r in range(dil):
        blk = a_ref[0, r].astype(F32)
        for cb in range(n_cb):
            slab_ref[cb, pl.ds(r, tm // dil, stride=dil), :] = blk[:, cb * LANES:(cb + 1) * LANES]
    attn = jnp.concatenate([slab_ref[cb] for cb in range(n_cb)], axis=1).astype(BF16)
    x1 = (x_ref[...] + jnp.dot(attn, wa_ref[...], preferred_element_type=F32)
          + jnp.dot(s_ref[...], ws_ref[...], preferred_element_type=F32))
    x1_ref[...] = x1
    h_ref[...] = _rms(x1, nw_ref[...]).astype(BF16)


def _out_projection(x2d, attn_planes, ssm, w_out, norm_w, seq, tm=1024):
    t_rows = x2d.shape[0]
    dil = attn_planes.shape[1]
    nseq = seq // tm
    row = pl.BlockSpec((tm, D_MODEL), lambda i: (i, 0))
    planes = pl.BlockSpec((1, dil, tm // dil, ATTN_WIDTH), lambda i: (i // nseq, 0, i % nseq, 0))
    wa = w_out[:ATTN_WIDTH].astype(BF16)
    ws = w_out[ATTN_WIDTH:].astype(BF16)
    return pl.pallas_call(
        _outproj_kernel,
        grid=(t_rows // tm,),
        in_specs=[row, planes, row, _const_spec((ATTN_WIDTH, D_MODEL)), _const_spec((D_INNER, D_MODEL)),
                  _const_spec((1, D_MODEL))],
        out_specs=[row, row],
        out_shape=[jax.ShapeDtypeStruct((t_rows, D_MODEL), F32),
                   jax.ShapeDtypeStruct((t_rows, D_MODEL), BF16)],
        scratch_shapes=[pltpu.VMEM((ATTN_WIDTH // LANES, tm, LANES), F32)],
        compiler_params=_params("parallel"),
        name="out_projection",
    )(x2d, attn_planes, ssm, wa, ws, norm_w.reshape(1, D_MODEL))


FFN_COLS = 256
FFN_HALO = BF16_ROWS
FFN_SLOTS = 2


def _ffn_up_kernel(hc_ref, hp_ref, hn_ref, w_ref, cw_ref, cb_ref, act_ref, lhs_scr, u_scr, *,
                   blocks_per_seq):
    tm = hc_ref.shape[0]
    pos = pl.program_id(0) % blocks_per_seq
    rows = tm + 2 * FFN_HALO
    lhs_scr[0:FFN_HALO, :] = jnp.where(pos > 0, hp_ref[...], jnp.zeros_like(hp_ref))
    lhs_scr[FFN_HALO:FFN_HALO + tm, :] = hc_ref[...]
    lhs_scr[FFN_HALO + tm:rows, :] = jnp.where(pos < blocks_per_seq - 1, hn_ref[...],
                                               jnp.zeros_like(hn_ref))

    n_chunks = D_FF // FFN_COLS

    def chunk_cols(j, half):
        return pl.ds(pl.multiple_of(j * FFN_COLS + half * D_FF, LANES), FFN_COLS)

    def project(j, slot):
        for half in range(2):
            u_scr[2 * slot + half] = jnp.dot(lhs_scr[...], w_ref[:, chunk_cols(j, half)],
                                             preferred_element_type=F32)

    def conv(j, slot, half):
        return _conv3_rows(u_scr[2 * slot + half], cw_ref[:, chunk_cols(j, half)],
                           cb_ref[:, chunk_cols(j, half)], FFN_HALO, tm)

    def finish(j, slot):
        act_ref[:, chunk_cols(j, 0)] = (_silu(conv(j, slot, 0)) * conv(j, slot, 1)).astype(BF16)

    _software_pipeline(n_chunks, project, finish, depth=1, slots=FFN_SLOTS, rolled=True)


def _ffn_up(h, w_up, conv_w, conv_b, seq, tm=1024):
    t_rows = h.shape[0]
    hpb = tm // FFN_HALO
    last_halo = t_rows // FFN_HALO - 1
    width = 2 * D_FF
    return pl.pallas_call(
        functools.partial(_ffn_up_kernel, blocks_per_seq=seq // tm),
        grid=(t_rows // tm,),
        in_specs=[
            pl.BlockSpec((tm, D_MODEL), lambda i: (i, 0)),
            pl.BlockSpec((FFN_HALO, D_MODEL), lambda i: (jnp.maximum(i * hpb - 1, 0), 0)),
            pl.BlockSpec((FFN_HALO, D_MODEL), lambda i: (jnp.minimum((i + 1) * hpb, last_halo), 0)),
            _const_spec((D_MODEL, width)), _const_spec((3, width)), _const_spec((1, width)),
        ],
        out_specs=pl.BlockSpec((tm, D_FF), lambda i: (i, 0)),
        out_shape=jax.ShapeDtypeStruct((t_rows, D_FF), BF16),
        scratch_shapes=[pltpu.VMEM((tm + 2 * FFN_HALO, D_MODEL), BF16),
                        pltpu.VMEM((2 * FFN_SLOTS, tm + 2 * FFN_HALO, FFN_COLS), F32)],
        compiler_params=_params("parallel"),
        name="ffn_up",
    )(h, h, h, w_up.astype(BF16), conv_w.T, conv_b.reshape(1, width))


def _ffn_down_kernel(a_ref, x1_ref, wd_ref, nw_ref, o_ref):
    acc = x1_ref[...] + jnp.dot(a_ref[...], wd_ref[...], preferred_element_type=F32)
    o_ref[...] = _rms(acc, nw_ref[...])


def _ffn_down(act, x1, w_down, norm_w, tm=512):
    t_rows = act.shape[0]
    return pl.pallas_call(
        _ffn_down_kernel,
        grid=(t_rows // tm,),
        in_specs=[pl.BlockSpec((tm, D_FF), lambda i: (i, 0)), pl.BlockSpec((tm, D_MODEL), lambda i: (i, 0)),
                  _const_spec((D_FF, D_MODEL)), _const_spec((1, D_MODEL))],
        out_specs=pl.BlockSpec((tm, D_MODEL), lambda i: (i, 0)),
        out_shape=jax.ShapeDtypeStruct((t_rows, D_MODEL), F32),
        compiler_params=_params("parallel"),
        name="ffn_down",
    )(act, x1, w_down.astype(BF16), norm_w.reshape(1, D_MODEL))


def kernel(x, norm1_w, w_in, ssm_conv_w, ssm_conv_b, a_log_f, a_log_b, dt_bias_f, dt_bias_b, d_skip,
           ssm_norm_w, w_out, norm2_w, w_up, ffn_conv_w, ffn_conv_b, w_down, final_norm_w):
    b, s, d = x.shape
    depth = w_in.shape[0]
    x2d = x.reshape(b * s, d)
    for layer in range(depth):
        (q4, k4, v4, q16, k16, v16, z) = _in_projection(x2d, norm1_w[layer], w_in[layer], b, s)
        xa, dt, ldt, cs = _ssm_projection(
            x2d, norm1_w[layer], w_in[layer], ssm_conv_w[layer], ssm_conv_b[layer],
            a_log_f[layer], a_log_b[layer], dt_bias_f[layer], dt_bias_b[layer], s)
        attn = _dilated_attention([(q4, k4, v4), (q4, k4, v4), (q16, k16, v16)])
        sh = lambda t: t.reshape(b, s, t.shape[-1])
        ssm = _ssm_mixer(sh(z), sh(xa), sh(dt), sh(ldt), sh(cs), a_log_f[layer], a_log_b[layer],
                         d_skip[layer], ssm_norm_w[layer])
        x1, h2 = _out_projection(x2d, attn, ssm.reshape(b * s, -1), w_out[layer], norm2_w[layer], s)
        act = _ffn_up(h2, w_up[layer], ffn_conv_w[layer], ffn_conv_b[layer], s)
        assert depth == 1
        x2d = _ffn_down(act, x1, w_down[layer], final_norm_w)
    return x2d.reshape(b, s, d)
```

```python
import functools

import numpy as np
import jax
import jax.numpy as jnp
from jax import lax
from jax.experimental import pallas as pl
from jax.experimental.pallas import tpu as pltpu

F32 = jnp.float32
BF16 = jnp.bfloat16

D_MODEL = 1024
HEAD_DIM = 64
N_ATTN_HEADS = 16
ATTN_WIDTH = N_ATTN_HEADS * HEAD_DIM
ROPE_DIM = HEAD_DIM // 4
ROPE_THETA = 500000.0
DILATIONS = (1, 4, 16)
BAND_HALF = 64
CARRY_SPLIT = 4

D_INNER = 1024
N_SSM_HEADS = 16
N_SSM_GROUPS = 4
D_STATE = 128
CHUNK = 128
XBC_WIDTH = D_INNER + 2 * N_SSM_GROUPS * D_STATE
D_FF = 2816
EPS = 1e-6

LANES = 128
BF16_ROWS = 16
VMEM_LIMIT = 56 * 1024 * 1024
NEG_BIG = -1e30
LOG2E = 1.4426950408889634


def _params(*sem):
    return pltpu.CompilerParams(dimension_semantics=sem, vmem_limit_bytes=VMEM_LIMIT)


def _const_spec(shape):
    return pl.BlockSpec(shape, lambda *_: (0,) * len(shape))


def _rms(x, w):
    return x * lax.rsqrt(jnp.mean(x * x, axis=-1, keepdims=True) + EPS) * w


def _silu(y):
    h = 0.5 * y
    return h + h * jnp.tanh(h)


def _conv3_rows(u, w, b, halo, rows):
    r = u.shape[0]
    y = pltpu.roll(u, 1, 0) * w[0:1] + u * w[1:2] + pltpu.roll(u, r - 1, 0) * w[2:3] + b
    return y[halo:halo + rows]


def _software_pipeline(n_chunks, project, finish, depth, slots, rolled):
    for k in range(min(depth, n_chunks)):
        project(k, k % slots)
    n_rolled = max(n_chunks - depth, 0) // slots if rolled else 0

    def body(i, carry):
        k0 = i * slots
        for s in range(slots):
            project(k0 + s + depth, (s + depth) % slots)
            finish(k0 + s, s)
        return carry

    if n_rolled:
        lax.fori_loop(0, n_rolled, body, 0)
    for k in range(n_rolled * slots, n_chunks):
        if k + depth < n_chunks:
            project(k + depth, (k + depth) % slots)
        finish(k, k % slots)


def _inproj_kernel(x_ref, nw_ref, wq_ref, wk_ref, wv_ref, wz_ref, rc_ref, rs1_ref, rs2_ref,
                   q4_ref, k4_ref, v4_ref, q16_ref, k16_ref, v16_ref, z_ref, slab_ref, slab2_ref):
    tm = x_ref.shape[0]
    h = _rms(x_ref[...], nw_ref[...]).astype(BF16)
    rc, rs1, rs2 = rc_ref[...], rs1_ref[...], rs2_ref[...]
    n_cb = ATTN_WIDTH // LANES

    def emit(idx, w_ref, out_refs, rope):
        t = jnp.dot(h, w_ref[...], preferred_element_type=F32)
        for cb in range(n_cb):
            cols = slice(cb * LANES, (cb + 1) * LANES)
            blk = t[:, cols]
            if rope:
                lo = pltpu.roll(blk, ROPE_DIM // 2, 1)
                hi = pltpu.roll(blk, LANES - ROPE_DIM // 2, 1)
                blk = blk * rc + lo * rs1 + hi * rs2
            slab = slab_ref.at[idx * n_cb + cb]
            slab2 = slab2_ref.at[idx * n_cb + cb]
            slab[...] = blk
            n4 = tm // CARRY_SPLIT
            n16 = n4 // CARRY_SPLIT
            for r4 in range(CARRY_SPLIT):
                p4 = slab[pl.ds(r4, n4, stride=CARRY_SPLIT), :]
                out_refs[0][0, r4, :, cols] = p4.astype(BF16)
                slab2[r4 * n4:(r4 + 1) * n4, :] = p4
                for j in range(CARRY_SPLIT):
                    out_refs[1][0, CARRY_SPLIT * j + r4, :, cols] = (
                        slab2[pl.ds(r4 * n4 + j, n16, stride=CARRY_SPLIT), :].astype(BF16))

    emit(0, wq_ref, (q4_ref, q16_ref), True)
    emit(1, wk_ref, (k4_ref, k16_ref), True)
    emit(2, wv_ref, (v4_ref, v16_ref), False)
    z_ref[...] = jnp.dot(h, wz_ref[...], preferred_element_type=F32).astype(BF16)


def _rope_tables(seq):
    half = ROPE_DIM // 2
    inv_freq = jnp.power(ROPE_THETA, -jnp.arange(half, dtype=F32) * 2.0 / ROPE_DIM)
    ang = jnp.arange(seq, dtype=F32)[:, None] * inv_freq[None, :]
    cos, sin = jnp.cos(ang), jnp.sin(ang)
    one = jnp.ones((seq, HEAD_DIM - ROPE_DIM), F32)
    zero8 = jnp.zeros((seq, half), F32)
    zero = jnp.zeros((seq, HEAD_DIM - ROPE_DIM), F32)
    rc = jnp.concatenate([cos, cos, one], axis=1)
    rs1 = jnp.concatenate([zero8, sin, zero], axis=1)
    rs2 = jnp.concatenate([-sin, zero8, zero], axis=1)
    rep = LANES // HEAD_DIM
    return tuple(jnp.tile(t, (1, rep)) for t in (rc, rs1, rs2))


def _in_projection(x2d, norm_w, w_in, batch, seq, tm=512):
    t_rows = x2d.shape[0]
    a = ATTN_WIDTH
    wq = (w_in[:, :a] * (HEAD_DIM ** -0.5 * LOG2E)).astype(BF16)
    wk = w_in[:, a:2 * a].astype(BF16)
    wv = w_in[:, 2 * a:3 * a].astype(BF16)
    wz = w_in[:, 3 * a:3 * a + D_INNER].astype(BF16)
    rc, rs1, rs2 = _rope_tables(seq)
    nseq = seq // tm
    row = lambda width: pl.BlockSpec((tm, width), lambda i: (i, 0))
    tab = pl.BlockSpec((tm, LANES), lambda i: (i % nseq, 0))
    plane = lambda dil: pl.BlockSpec((1, dil, tm // dil, a), lambda i: (i // nseq, 0, i % nseq, 0))
    plane_shape = lambda dil: jax.ShapeDtypeStruct((batch, dil, seq // dil, a), BF16)
    d4, d16 = DILATIONS[1:]
    return pl.pallas_call(
        _inproj_kernel,
        grid=(t_rows // tm,),
        in_specs=[row(D_MODEL), _const_spec((1, D_MODEL)),
                  _const_spec((D_MODEL, a)), _const_spec((D_MODEL, a)), _const_spec((D_MODEL, a)),
                  _const_spec((D_MODEL, D_INNER)), tab, tab, tab],
        out_specs=[plane(d4)] * 3 + [plane(d16)] * 3 + [row(D_INNER)],
        out_shape=[plane_shape(d4)] * 3 + [plane_shape(d16)] * 3
        + [jax.ShapeDtypeStruct((t_rows, D_INNER), BF16)],
        scratch_shapes=[pltpu.VMEM((3 * a // LANES, tm, LANES), F32)] * 2,
        compiler_params=_params("parallel"),
        name="in_projection",
    )(x2d, norm_w.reshape(1, D_MODEL), wq, wk, wv, wz, rc, rs1, rs2)


SSM_HALO = BF16_ROWS
SSM_COLS = 256
SSM_SLOTS = 4


def _ssm_proj_kernel(xc_ref, xp_ref, xn_ref, nw_ref, wx_ref, wdt_ref, cw_ref, cb_ref, alog_ref, dtb_ref,
                     xa_ref, dt_ref, ldt_ref, cs_ref, lhs_scr, u_scr, *, blocks_per_seq):
    tm = xc_ref.shape[0]
    pos = pl.program_id(0) % blocks_per_seq
    nw = nw_ref[...]
    rows = tm + 2 * SSM_HALO
    h = _rms(xc_ref[...], nw).astype(BF16)
    zero_halo = jnp.zeros((SSM_HALO, D_MODEL), BF16)
    lhs_scr[0:SSM_HALO, :] = jnp.where(pos > 0, _rms(xp_ref[...], nw).astype(BF16), zero_halo)
    lhs_scr[SSM_HALO:SSM_HALO + tm, :] = h
    lhs_scr[SSM_HALO + tm:rows, :] = jnp.where(pos < blocks_per_seq - 1,
                                               _rms(xn_ref[...], nw).astype(BF16), zero_halo)
    n_chunks = XBC_WIDTH // SSM_COLS

    def chunk_cols(j):
        return pl.ds(pl.multiple_of(j * SSM_COLS, LANES), SSM_COLS)

    def project(j, slot):
        u_scr[slot] = jnp.dot(lhs_scr[...], wx_ref[:, chunk_cols(j)], preferred_element_type=F32)

    def finish(j, slot):
        y = _conv3_rows(u_scr[slot], cw_ref[:, chunk_cols(j)], cb_ref[:, chunk_cols(j)], SSM_HALO, tm)
        xa_ref[:, chunk_cols(j)] = _silu(y).astype(BF16)

    _software_pipeline(n_chunks, project, finish, depth=2, slots=SSM_SLOTS, rolled=False)

    x_dt = jnp.dot(h, wdt_ref[...], preferred_element_type=F32) + dtb_ref[...]
    dt = jnp.maximum(x_dt, 0.0) + jnp.log1p(jnp.exp(-jnp.abs(x_dt)))
    a = dt * (-jnp.exp(alog_ref[...]) * LOG2E)
    tri = (lax.broadcasted_iota(jnp.int32, (CHUNK, CHUNK), 1)
           <= lax.broadcasted_iota(jnp.int32, (CHUNK, CHUNK), 0)).astype(BF16)
    dt_ref[...] = dt
    ldt_ref[...] = jnp.log(dt) * LOG2E
    for ch in range(tm // CHUNK):
        rows_c = slice(ch * CHUNK, (ch + 1) * CHUNK)
        cs_ref[rows_c, :] = _split_dot_lhs_const(tri, a[rows_c])


def _ssm_projection(x2d, norm_w, w_in, conv_w, conv_b, a_log_f, a_log_b, dt_bias_f, dt_bias_b,
                    seq, tm=1024):
    t_rows = x2d.shape[0]
    pad = LANES - 2 * N_SSM_HEADS
    alog = jnp.pad(jnp.concatenate([a_log_f, a_log_b]), (0, pad)).reshape(1, LANES)
    dtb = jnp.pad(jnp.concatenate([dt_bias_f, dt_bias_b]), (0, pad)).reshape(1, LANES)
    o = 3 * ATTN_WIDTH + D_INNER
    wx = w_in[:, o:o + XBC_WIDTH].astype(BF16)
    wdt = jnp.pad(w_in[:, o + XBC_WIDTH:], ((0, 0), (0, pad))).astype(BF16)
    hpb = tm // SSM_HALO
    last_halo = t_rows // SSM_HALO - 1
    row = lambda width: pl.BlockSpec((tm, width), lambda i: (i, 0))
    return pl.pallas_call(
        functools.partial(_ssm_proj_kernel, blocks_per_seq=seq // tm),
        grid=(t_rows // tm,),
        in_specs=[row(D_MODEL),
                  pl.BlockSpec((SSM_HALO, D_MODEL), lambda i: (jnp.maximum(i * hpb - 1, 0), 0)),
                  pl.BlockSpec((SSM_HALO, D_MODEL), lambda i: (jnp.minimum((i + 1) * hpb, last_halo), 0)),
                  _const_spec((1, D_MODEL)), _const_spec((D_MODEL, XBC_WIDTH)),
                  _const_spec((D_MODEL, LANES)),
                  _const_spec((3, XBC_WIDTH)), _const_spec((1, XBC_WIDTH)),
                  _const_spec((1, LANES)), _const_spec((1, LANES))],
        out_specs=[row(XBC_WIDTH), row(LANES), row(LANES), row(LANES)],
        out_shape=[jax.ShapeDtypeStruct((t_rows, XBC_WIDTH), BF16)]
        + [jax.ShapeDtypeStruct((t_rows, LANES), F32)] * 3,
        scratch_shapes=[pltpu.VMEM((tm + 2 * SSM_HALO, D_MODEL), BF16),
                        pltpu.VMEM((SSM_SLOTS, tm + 2 * SSM_HALO, SSM_COLS), F32)],
        compiler_params=_params("parallel"),
        name="ssm_projection",
    )(x2d, x2d, x2d, norm_w.reshape(1, D_MODEL), wx, wdt, conv_w.T, conv_b.reshape(1, XBC_WIDTH),
      alog, dtb)


ATT_TQ = 128
ATT_TK = ATT_TQ + 2 * BAND_HALF
ATT_UNITS_PER_STEP = 32
ATT_STAT_PAIRS = 4
ATT_QUERIES_PER_STEP = 1024


def _attn_kernel(*refs, length, n_sub, n_pairs, n_planes, first, last):
    interleaved = first
    q_ref, k_ref, v_ref = refs[:3]
    if first:
        o_ref, st_ref = refs[3:]
    elif last:
        op_ref, sp_ref, o_ref = refs[3:]
    else:
        op_ref, sp_ref, o_ref, st_ref, slab_ref = refs[3:]
    tq, tk = ATT_TQ, ATT_TK
    qi = pl.program_id(3)
    lane = lax.broadcasted_iota(jnp.int32, (tq, LANES), 1)
    even = lane < HEAD_DIM
    stat_is_max = (lane & 15) < 8
    row_i = lax.broadcasted_iota(jnp.int32, (tq, tk), 0)
    col_i = lax.broadcasted_iota(jnp.int32, (tq, tk), 1)
    nt = (((1,), (1,)), ((), ()))
    ones = jnp.ones((tk, LANES), BF16)
    sub_rows = tq // CARRY_SPLIT
    key_rows = tk // CARRY_SPLIT

    if interleaved:
        delta = (CARRY_SPLIT * ((col_i & (key_rows - 1)) - (row_i & (sub_rows - 1)))
                 + (col_i // key_rows - row_i // sub_rows))
    else:
        delta = col_i - row_i

    def window_start(sb):
        if interleaved:
            a0 = (qi * n_sub + sb) * sub_rows
            ws = pl.multiple_of(jnp.clip(a0 - BAND_HALF // CARRY_SPLIT, 0, length - key_rows),
                                BF16_ROWS)
            return ws, CARRY_SPLIT * (ws - a0)
        q0 = (qi * n_sub + sb % n_sub) * tq
        ws = pl.multiple_of(jnp.clip(q0 - BAND_HALF, 0, length - tk), BAND_HALF)
        return ws, ws - q0

    def load_q(sb, cols):
        if interleaved:
            return jnp.concatenate([q_ref[0, j, sb * sub_rows:(sb + 1) * sub_rows, cols]
                                    for j in range(CARRY_SPLIT)], axis=0)
        return q_ref[0, sb // n_sub, (sb % n_sub) * tq:(sb % n_sub + 1) * tq, cols]

    def load_keys(ref, sb, ws, cols):
        if interleaved:
            return jnp.concatenate([ref[0, j, pl.ds(ws, key_rows), cols]
                                    for j in range(CARRY_SPLIT)], axis=0)
        return ref[0, sb // n_sub, pl.ds(ws, tk), cols]

    scores, row_max = [], []
    for sb in range(n_planes * n_sub):
        ws, offset = window_start(sb)
        bias = jnp.where(jnp.abs(delta + offset) <= BAND_HALF, 0.0, NEG_BIG)
        bias2 = jnp.concatenate([bias, bias], axis=0)
        for hp in range(n_pairs):
            cols = slice(hp * LANES, (hp + 1) * LANES)
            q2 = load_q(sb, cols)
            zero = jnp.zeros_like(q2)
            qq = jnp.concatenate([jnp.where(even, q2, zero), jnp.where(even, zero, q2)], axis=0)
            k2 = load_keys(k_ref, sb, ws, cols)
            s = lax.dot_general(qq, k2, nt, preferred_element_type=F32) + bias2
            scores.append(s)
            row_max.append(jnp.max(s, axis=-1, keepdims=True))

    for sb in range(n_planes * n_sub):
        ws, _ = window_start(sb)
        plane = sb // n_sub
        rows = slice((sb % n_sub) * tq, (sb % n_sub + 1) * tq)
        out_rows = slice(sb * sub_rows, (sb + 1) * sub_rows)
        stats = jnp.zeros((tq, LANES), F32)
        for hp in range(n_pairs):
            u = sb * n_pairs + hp
            cols = slice(hp * LANES, (hp + 1) * LANES)
            v_ext = jnp.concatenate([load_keys(v_ref, sb, ws, cols), ones], axis=1)
            p = jnp.exp2(scores[u] - row_max[u]).astype(BF16)
            pv = jnp.dot(p, v_ext, preferred_element_type=F32)
            acc = jnp.where(even, pv[:tq, :LANES], pv[tq:, :LANES])
            l = jnp.where(even, pv[:tq, LANES:], pv[tq:, LANES:])
            m = jnp.where(even, row_max[u][:tq], row_max[u][tq:])
            if not first:
                sp = sp_ref[0, plane, rows, :]
                be = LANES * (hp // ATT_STAT_PAIRS) + 16 * (hp % ATT_STAT_PAIRS)
                bo = be + HEAD_DIM
                m_prev = jnp.where(even, sp[:, be:be + 1], sp[:, bo:bo + 1])
                l_prev = jnp.where(even, sp[:, be + 8:be + 9], sp[:, bo + 8:bo + 9])
                acc_prev = op_ref[0, plane, rows, cols].astype(F32)
                m_new = jnp.maximum(m_prev, m)
                a_prev = jnp.exp2(m_prev - m_new)
                a_cur = jnp.exp2(m - m_new)
                acc = acc_prev * a_prev + acc * a_cur
                l = l_prev * a_prev + l * a_cur
                m = m_new
            if last:
                o_ref[0, plane, rows, cols] = (acc / l).astype(BF16)
                continue
            in_zone = ((lane & (HEAD_DIM - 1)) >> 4) == hp
            stats = jnp.where(in_zone, jnp.where(stat_is_max, m, l), stats)
            if interleaved:
                for j in range(CARRY_SPLIT):
                    o_ref[0, j, out_rows, cols] = acc[j * sub_rows:(j + 1) * sub_rows].astype(BF16)
            else:
                slab_ref[hp] = acc
                for j in range(CARRY_SPLIT):
                    o_ref[0, j, 0, out_rows, cols] = (
                        slab_ref[hp, pl.ds(j, sub_rows, stride=CARRY_SPLIT), :].astype(BF16))
        if last:
            continue
        if interleaved:
            for j in range(CARRY_SPLIT):
                st_ref[0, j, out_rows, :] = stats[j * sub_rows:(j + 1) * sub_rows]
        else:
            slab_ref[n_pairs] = stats
            for j in range(CARRY_SPLIT):
                st_ref[0, j, 0, out_rows, :] = (
                    slab_ref[n_pairs, pl.ds(j, sub_rows, stride=CARRY_SPLIT), :])


def _attention_pattern(q, k, v, o_prev, st_prev, first, last):
    cs = CARRY_SPLIT
    if first:
        b, _, length, width = q.shape
        dil, qs = 1, min(length, ATT_QUERIES_PER_STEP // cs)
        n_sub = qs * cs // ATT_TQ
    else:
        b, dil, length, width = q.shape
        qs = min(length, ATT_QUERIES_PER_STEP)
        n_sub = qs // ATT_TQ
    n_pairs = min(ATT_UNITS_PER_STEP // n_sub, width // LANES)
    hw = n_pairs * LANES
    n_hg = width // hw
    n_stat = n_pairs // ATT_STAT_PAIRS
    assert last or n_stat == 1
    n_planes = min(max(ATT_UNITS_PER_STEP // (n_sub * n_pairs), 1), dil) if last else 1
    scratch = []
    if first:
        blk = pl.BlockSpec((1, cs, qs, hw), lambda bi, r, g, qi: (bi, 0, qi, g))
        seq = pl.BlockSpec((1, cs, length, hw), lambda bi, r, g, qi: (bi, 0, 0, g))
        stat = pl.BlockSpec((1, cs, qs, LANES), lambda bi, r, g, qi: (bi, 0, qi, g))
    else:
        blk = pl.BlockSpec((1, n_planes, qs, hw), lambda bi, r, g, qi: (bi, r, qi, g))
        seq = pl.BlockSpec((1, n_planes, length, hw), lambda bi, r, g, qi: (bi, r, 0, g))
        stat = pl.BlockSpec((1, n_planes, qs, n_stat * LANES), lambda bi, r, g, qi: (bi, r, qi, g))
    in_specs, args = [blk, seq, seq], [q, k, v]
    if not first:
        in_specs += [blk, stat]
        args += [o_prev, st_prev]
    if first:
        out_specs = [blk, stat]
        out_shape = [jax.ShapeDtypeStruct((b, cs, length, width), BF16),
                     jax.ShapeDtypeStruct((b, cs, length, n_hg * LANES), F32)]
    elif last:
        out_specs = [blk]
        out_shape = [jax.ShapeDtypeStruct((b, dil, length, width), BF16)]
    else:
        carry = lambda w: pl.BlockSpec((1, cs, 1, qs // cs, w), lambda bi, r, g, qi: (bi, 0, r, qi, g))
        out_specs = [carry(hw), carry(LANES)]
        out_shape = [jax.ShapeDtypeStruct((b, cs, dil, length // cs, width), BF16),
                     jax.ShapeDtypeStruct((b, cs, dil, length // cs, n_hg * LANES), F32)]
        scratch.append(pltpu.VMEM((n_pairs + 1, ATT_TQ, LANES), F32))
    outs = pl.pallas_call(
        functools.partial(_attn_kernel, length=length, n_sub=n_sub, n_pairs=n_pairs,
                          n_planes=n_planes, first=first, last=last),
        grid=(b, dil // n_planes, n_hg, length // qs),
        in_specs=in_specs, out_specs=out_specs, out_shape=out_shape,
        scratch_shapes=scratch,
        compiler_params=_params("parallel", "parallel", "parallel", "arbitrary"),
        name=f"attention_dil{dil}",
    )(*args)
    if last:
        return outs[0], None
    if first:
        return outs[0], outs[1]
    nxt = lambda t: t.reshape(b, cs * dil, length // cs, t.shape[-1])
    return nxt(outs[0]), nxt(outs[1])


def _dilated_attention(qkv_planes):
    o = st = None
    for i, (q, k, v) in enumerate(qkv_planes):
        o, st = _attention_pattern(q, k, v, o, st, i == 0, i == len(qkv_planes) - 1)
    return o


def _split_dot(v, mat, passes):
    out = None
    r = v
    for i in range(passes):
        piece = r.astype(BF16)
        term = jnp.dot(piece, mat, preferred_element_type=F32)
        out = term if out is None else out + term
        if i + 1 < passes:
            r = r - piece.astype(F32)
    return out


def _split_dot_lhs_const(mat, v):
    out = None
    r = v
    for i in range(3):
        piece = r.astype(BF16)
        term = jnp.dot(mat, piece, preferred_element_type=F32)
        out = term if out is None else out + term
        if i < 2:
            r = r - piece.astype(F32)
    return out


def _ssm_kernel(xa_ref, dt_ref, ldt_ref, cs_ref, z_ref, alog_ref, dsk_ref, nw_ref, ef_ref, eb_ref,
                y_ref, hb_ref, bt_c, hrun_ref, *, n_steps):
    L = CHUNK
    cps = SSM_CHUNKS_PER_STEP
    step = pl.program_id(1)
    is_bwd = step < n_steps
    blk = jnp.where(is_bwd, n_steps - 1 - step, step - n_steps)

    @pl.when((step == 0) | (step == n_steps))
    def _():
        hrun_ref[...] = jnp.zeros_like(hrun_ref)

    n_bc = N_SSM_GROUPS * D_STATE
    gw = D_INNER // N_SSM_GROUPS
    li = lax.broadcasted_iota(jnp.int32, (L, L), 0)
    si = lax.broadcasted_iota(jnp.int32, (L, L), 1)
    a_log2 = -jnp.exp(alog_ref[...]) * LOG2E

    def scalars(rows):
        dt, ldt, cs = dt_ref[0, rows, :], ldt_ref[0, rows, :], cs_ref[0, rows, :]
        return dt, ldt, cs, cs - dt * a_log2, cs[L - 1:L, :]

    def state_update(xs, bts, weights, tot, e_ref):
        xw = (_split_dot(weights, e_ref[...], 1) * xs).astype(BF16)
        decay = _split_dot(jnp.broadcast_to(jnp.exp2(tot), (8, LANES)), e_ref[...], 3)[0:1]
        for g in range(N_SSM_GROUPS):
            cols = slice(g * gw, (g + 1) * gw)
            s_g = jnp.dot(bts[g], xw[:, cols], preferred_element_type=F32)
            hrun_ref[:, cols] = hrun_ref[:, cols] * decay[:, cols] + s_g

    def bwd_chunk(i, carry):
        ci = cps - 1 - i
        c = blk * cps + ci
        rows = pl.ds(pl.multiple_of(ci * L, L), L)
        dt, ldt, cs, ecs, tot = scalars(rows)
        hb_ref[c] = hrun_ref[...].astype(BF16)
        xs = xa_ref[0, rows, 0:D_INNER].astype(F32)
        bm = xa_ref[0, rows, D_INNER:D_INNER + n_bc].astype(F32)
        bts = [bm[:, g * D_STATE:(g + 1) * D_STATE].T.astype(BF16) for g in range(N_SSM_GROUPS)]
        bt_c[c] = jnp.concatenate(bts, axis=1)
        state_update(xs, bts, jnp.exp2(ecs + ldt), tot, eb_ref)
        return carry

    def fwd_chunk(ci, carry):
        c = blk * cps + ci
        rows = pl.ds(pl.multiple_of(ci * L, L), L)
        dt, ldt, cs, ecs, tot = scalars(rows)
        xs_b = xa_ref[0, rows, 0:D_INNER]
        xs = xs_b.astype(F32)
        bts = [bt_c[c, :, g * D_STATE:(g + 1) * D_STATE] for g in range(N_SSM_GROUPS)]
        hf_in = hrun_ref[...].astype(BF16)
        hb_in = hb_ref[c]
        scale_f = _split_dot(jnp.exp2(cs), ef_ref[...], 1)
        scale_b = _split_dot(jnp.exp2(tot - ecs), eb_ref[...], 1)
        row_f = (cs - ldt).T
        row_b = (ecs + ldt).T
        dt_t = dt.T
        lane = lax.broadcasted_iota(jnp.int32, (L, LANES), 1)
        even = lane < HEAD_DIM
        below = si < li
        diag = si == li
        hpg = N_SSM_HEADS // N_SSM_GROUPS
        y_groups = []
        for g in range(N_SSM_GROUPS):
            cols = slice(g * gw, (g + 1) * gw)
            cg = xa_ref[0, rows, D_INNER + n_bc + g * D_STATE:D_INNER + n_bc + (g + 1) * D_STATE]
            bg = xa_ref[0, rows, D_INNER + g * D_STATE:D_INNER + (g + 1) * D_STATE]
            gmat = lax.dot_general(cg, bg, (((1,), (1,)), ((), ())), preferred_element_type=F32)
            y_off = (scale_f[:, cols] * jnp.dot(cg, hf_in[:, cols], preferred_element_type=F32)
                     + scale_b[:, cols] * jnp.dot(cg, hb_in[:, cols], preferred_element_type=F32))
            pairs = []
            for pr in range(hpg // 2):
                h0 = g * hpg + 2 * pr
                xs_pair = xs_b[:, h0 * HEAD_DIM:(h0 + 2) * HEAD_DIM]
                res = []
                for h in (h0, h0 + 1):
                    hb = N_SSM_HEADS + h
                    sel = jnp.where(below, cs[:, h:h + 1] - row_f[h:h + 1, :],
                                    row_b[hb:hb + 1, :] - ecs[:, hb:hb + 1])
                    e = jnp.exp2(sel) + jnp.where(diag, dt_t[h:h + 1, :], 0.0)
                    w = (gmat * e).astype(BF16)
                    res.append(jnp.dot(w, xs_pair, preferred_element_type=F32))
                pairs.append(jnp.where(even, res[0], res[1]))
            y_groups.append(jnp.concatenate(pairs, axis=1) + y_off)
        y = jnp.concatenate(y_groups, axis=1) + dsk_ref[...] * xs
        zf = z_ref[0, rows, :].astype(F32)
        gated = y * _silu(zf)
        outs = []
        for g in range(N_SSM_GROUPS):
            gg = gated[:, g * gw:(g + 1) * gw]
            outs.append(gg * lax.rsqrt(jnp.mean(gg * gg, axis=-1, keepdims=True) + EPS))
        y_ref[0, rows, :] = (jnp.concatenate(outs, axis=1) * nw_ref[...]).astype(BF16)
        state_update(xs, bts, jnp.exp2(tot - cs + ldt), tot, ef_ref)
        return carry

    @pl.when(is_bwd)
    def _():
        lax.fori_loop(0, cps, bwd_chunk, 0)

    @pl.when(jnp.logical_not(is_bwd))
    def _():
        lax.fori_loop(0, cps, fwd_chunk, 0)


SSM_CHUNKS_PER_STEP = 16


def _head_expanders():
    col_head = np.arange(D_INNER) // HEAD_DIM
    rows = np.arange(LANES)[:, None]
    ef = (rows == col_head[None, :]).astype(np.float32)
    eb = (rows == col_head[None, :] + N_SSM_HEADS).astype(np.float32)
    return jnp.asarray(ef, BF16), jnp.asarray(eb, BF16)


def _ssm_mixer(z, xa, dt, ldt, cs, a_log_f, a_log_b, d_skip, norm_w):
    b, s, _ = z.shape
    n = s // CHUNK
    pad = LANES - 2 * N_SSM_HEADS
    alog = jnp.pad(jnp.concatenate([a_log_f, a_log_b]), (0, pad)).reshape(1, LANES)
    dsk = jnp.repeat(d_skip, HEAD_DIM).reshape(1, D_INNER)
    ef, eb = _head_expanders()

    rows = SSM_CHUNKS_PER_STEP * CHUNK
    n_steps = s // rows

    def block_of(st):
        return jnp.where(st < n_steps, n_steps - 1 - st, st - n_steps)

    per_chunk = pl.BlockSpec((1, rows, LANES), lambda bi, st: (bi, block_of(st), 0))
    fwd_only = pl.BlockSpec((1, rows, D_INNER), lambda bi, st: (bi, jnp.maximum(st - n_steps, 0), 0))
    return pl.pallas_call(
        functools.partial(_ssm_kernel, n_steps=n_steps),
        grid=(b, 2 * n_steps),
        in_specs=[
            pl.BlockSpec((1, rows, XBC_WIDTH), lambda bi, st: (bi, block_of(st), 0)),
            per_chunk, per_chunk, per_chunk, fwd_only,
            _const_spec((1, LANES)), _const_spec((1, D_INNER)), _const_spec((1, D_INNER)),
            _const_spec((LANES, D_INNER)), _const_spec((LANES, D_INNER)),
        ],
        out_specs=fwd_only,
        out_shape=jax.ShapeDtypeStruct((b, s, D_INNER), BF16),
        scratch_shapes=[pltpu.VMEM((n, D_STATE, D_INNER), BF16),
                        pltpu.VMEM((n, D_STATE, N_SSM_GROUPS * CHUNK), BF16),
                        pltpu.VMEM((D_STATE, D_INNER), F32)],
        compiler_params=_params("arbitrary", "arbitrary"),
        name="ssm_mixer",
    )(xa, dt, ldt, cs, z, alog, dsk, norm_w.reshape(1, D_INNER), ef, eb)


def _outproj_kernel(x_ref, a_ref, s_ref, wa_ref, ws_ref, nw_ref, x1_ref, h_ref, slab_ref):
    tm = x_ref.shape[0]
    dil = a_ref.shape[1]
    n_cb = ATTN_WIDTH // LANES
    for r in range(dil):
        blk = a_ref[0, r].astype(F32)
        for cb in range(n_cb):
            slab_ref[cb, pl.ds(r, tm // dil, stride=dil), :] = blk[:, cb * LANES:(cb + 1) * LANES]
    attn = jnp.concatenate([slab_ref[cb] for cb in range(n_cb)], axis=1).astype(BF16)
    x1 = (x_ref[...] + jnp.dot(attn, wa_ref[...], preferred_element_type=F32)
          + jnp.dot(s_ref[...], ws_ref[...], preferred_element_type=F32))
    x1_ref[...] = x1
    h_ref[...] = _rms(x1, nw_ref[...]).astype(BF16)


def _out_projection(x2d, attn_planes, ssm, w_out, norm_w, seq, tm=1024):
    t_rows = x2d.shape[0]
    dil = attn_planes.shape[1]
    nseq = seq // tm
    row = pl.BlockSpec((tm, D_MODEL), lambda i: (i, 0))
    planes = pl.BlockSpec((1, dil, tm // dil, ATTN_WIDTH), lambda i: (i // nseq, 0, i % nseq, 0))
    wa = w_out[:ATTN_WIDTH].astype(BF16)
    ws = w_out[ATTN_WIDTH:].astype(BF16)
    return pl.pallas_call(
        _outproj_kernel,
        grid=(t_rows // tm,),
        in_specs=[row, planes, row, _const_spec((ATTN_WIDTH, D_MODEL)), _const_spec((D_INNER, D_MODEL)),
                  _const_spec((1, D_MODEL))],
        out_specs=[row, row],
        out_shape=[jax.ShapeDtypeStruct((t_rows, D_MODEL), F32),
                   jax.ShapeDtypeStruct((t_rows, D_MODEL), BF16)],
        scratch_shapes=[pltpu.VMEM((ATTN_WIDTH // LANES, tm, LANES), F32)],
        compiler_params=_params("parallel"),
        name="out_projection",
    )(x2d, attn_planes, ssm, wa, ws, norm_w.reshape(1, D_MODEL))


FFN_COLS = 256
FFN_HALO = BF16_ROWS
FFN_SLOTS = 2


def _ffn_up_kernel(hc_ref, hp_ref, hn_ref, w_ref, cw_ref, cb_ref, act_ref, lhs_scr, u_scr, *,
                   blocks_per_seq):
    tm = hc_ref.shape[0]
    pos = pl.program_id(0) % blocks_per_seq
    rows = tm + 2 * FFN_HALO
    lhs_scr[0:FFN_HALO, :] = jnp.where(pos > 0, hp_ref[...], jnp.zeros_like(hp_ref))
    lhs_scr[FFN_HALO:FFN_HALO + tm, :] = hc_ref[...]
    lhs_scr[FFN_HALO + tm:rows, :] = jnp.where(pos < blocks_per_seq - 1, hn_ref[...],
                                               jnp.zeros_like(hn_ref))

    n_chunks = D_FF // FFN_COLS

    def chunk_cols(j, half):
        return pl.ds(pl.multiple_of(j * FFN_COLS + half * D_FF, LANES), FFN_COLS)

    def project(j, slot):
        for half in range(2):
            u_scr[2 * slot + half] = jnp.dot(lhs_scr[...], w_ref[:, chunk_cols(j, half)],
                                             preferred_element_type=F32)

    def conv(j, slot, half):
        return _conv3_rows(u_scr[2 * slot + half], cw_ref[:, chunk_cols(j, half)],
                           cb_ref[:, chunk_cols(j, half)], FFN_HALO, tm)

    def finish(j, slot):
        act_ref[:, chunk_cols(j, 0)] = (_silu(conv(j, slot, 0)) * conv(j, slot, 1)).astype(BF16)

    _software_pipeline(n_chunks, project, finish, depth=1, slots=FFN_SLOTS, rolled=True)


def _ffn_up(h, w_up, conv_w, conv_b, seq, tm=1024):
    t_rows = h.shape[0]
    hpb = tm // FFN_HALO
    last_halo = t_rows // FFN_HALO - 1
    width = 2 * D_FF
    return pl.pallas_call(
        functools.partial(_ffn_up_kernel, blocks_per_seq=seq // tm),
        grid=(t_rows // tm,),
        in_specs=[
            pl.BlockSpec((tm, D_MODEL), lambda i: (i, 0)),
            pl.BlockSpec((FFN_HALO, D_MODEL), lambda i: (jnp.maximum(i * hpb - 1, 0), 0)),
            pl.BlockSpec((FFN_HALO, D_MODEL), lambda i: (jnp.minimum((i + 1) * hpb, last_halo), 0)),
            _const_spec((D_MODEL, width)), _const_spec((3, width)), _const_spec((1, width)),
        ],
        out_specs=pl.BlockSpec((tm, D_FF), lambda i: (i, 0)),
        out_shape=jax.ShapeDtypeStruct((t_rows, D_FF), BF16),
        scratch_shapes=[pltpu.VMEM((tm + 2 * FFN_HALO, D_MODEL), BF16),
                        pltpu.VMEM((2 * FFN_SLOTS, tm + 2 * FFN_HALO, FFN_COLS), F32)],
        compiler_params=_params("parallel"),
        name="ffn_up",
    )(h, h, h, w_up.astype(BF16), conv_w.T, conv_b.reshape(1, width))


def _ffn_down_kernel(a_ref, x1_ref, wd_ref, nw_ref, o_ref):
    acc = x1_ref[...] + jnp.dot(a_ref[...], wd_ref[...], preferred_element_type=F32)
    o_ref[...] = _rms(acc, nw_ref[...])


def _ffn_down(act, x1, w_down, norm_w, tm=1024):
    t_rows = act.shape[0]
    return pl.pallas_call(
        _ffn_down_kernel,
        grid=(t_rows // tm,),
        in_specs=[pl.BlockSpec((tm, D_FF), lambda i: (i, 0)), pl.BlockSpec((tm, D_MODEL), lambda i: (i, 0)),
                  _const_spec((D_FF, D_MODEL)), _const_spec((1, D_MODEL))],
        out_specs=pl.BlockSpec((tm, D_MODEL), lambda i: (i, 0)),
        out_shape=jax.ShapeDtypeStruct((t_rows, D_MODEL), F32),
        compiler_params=_params("parallel"),
        name="ffn_down",
    )(act, x1, w_down.astype(BF16), norm_w.reshape(1, D_MODEL))


def kernel(x, norm1_w, w_in, ssm_conv_w, ssm_conv_b, a_log_f, a_log_b, dt_bias_f, dt_bias_b, d_skip,
           ssm_norm_w, w_out, norm2_w, w_up, ffn_conv_w, ffn_conv_b, w_down, final_norm_w):
    b, s, d = x.shape
    depth = w_in.shape[0]
    x2d = x.reshape(b * s, d)
    for layer in range(depth):
        (q4, k4, v4, q16, k16, v16, z) = _in_projection(x2d, norm1_w[layer], w_in[layer], b, s)
        xa, dt, ldt, cs = _ssm_projection(
            x2d, norm1_w[layer], w_in[layer], ssm_conv_w[layer], ssm_conv_b[layer],
            a_log_f[layer], a_log_b[layer], dt_bias_f[layer], dt_bias_b[layer], s)
        attn = _dilated_attention([(q4, k4, v4), (q4, k4, v4), (q16, k16, v16)])
        sh = lambda t: t.reshape(b, s, t.shape[-1])
        ssm = _ssm_mixer(sh(z), sh(xa), sh(dt), sh(ldt), sh(cs), a_log_f[layer], a_log_b[layer],
                         d_skip[layer], ssm_norm_w[layer])
        x1, h2 = _out_projection(x2d, attn, ssm.reshape(b * s, -1), w_out[layer], norm2_w[layer], s)
        act = _ffn_up(h2, w_up[layer], ffn_conv_w[layer], ffn_conv_b[layer], s)
        assert depth == 1
        x2d = _ffn_down(act, x1, w_down[layer], final_norm_w)
    return x2d.reshape(b, s, d)
```

```python
import functools

import numpy as np
import jax
import jax.numpy as jnp
from jax import lax
from jax.experimental import pallas as pl
from jax.experimental.pallas import tpu as pltpu

F32 = jnp.float32
BF16 = jnp.bfloat16

D_MODEL = 1024
HEAD_DIM = 64
N_ATTN_HEADS = 16
ATTN_WIDTH = N_ATTN_HEADS * HEAD_DIM
ROPE_DIM = HEAD_DIM // 4
ROPE_THETA = 500000.0
DILATIONS = (1, 4, 16)
BAND_HALF = 64
CARRY_SPLIT = 4

D_INNER = 1024
N_SSM_HEADS = 16
N_SSM_GROUPS = 4
D_STATE = 128
CHUNK = 128
XBC_WIDTH = D_INNER + 2 * N_SSM_GROUPS * D_STATE
D_FF = 2816
EPS = 1e-6

LANES = 128
BF16_ROWS = 16
VMEM_LIMIT = 56 * 1024 * 1024
NEG_BIG = -1e30
LOG2E = 1.4426950408889634


def _params(*sem):
    return pltpu.CompilerParams(dimension_semantics=sem, vmem_limit_bytes=VMEM_LIMIT)


def _const_spec(shape):
    return pl.BlockSpec(shape, lambda *_: (0,) * len(shape))


def _rms(x, w):
    return x * lax.rsqrt(jnp.mean(x * x, axis=-1, keepdims=True) + EPS) * w


def _silu(y):
    h = 0.5 * y
    return h + h * jnp.tanh(h)


def _conv3_rows(u, w, b, halo, rows):
    r = u.shape[0]
    y = pltpu.roll(u, 1, 0) * w[0:1] + u * w[1:2] + pltpu.roll(u, r - 1, 0) * w[2:3] + b
    return y[halo:halo + rows]


def _software_pipeline(n_chunks, project, finish, depth, slots, rolled):
    for k in range(min(depth, n_chunks)):
        project(k, k % slots)
    n_rolled = max(n_chunks - depth, 0) // slots if rolled else 0

    def body(i, carry):
        k0 = i * slots
        for s in range(slots):
            project(k0 + s + depth, (s + depth) % slots)
            finish(k0 + s, s)
        return carry

    if n_rolled:
        lax.fori_loop(0, n_rolled, body, 0)
    for k in range(n_rolled * slots, n_chunks):
        if k + depth < n_chunks:
            project(k + depth, (k + depth) % slots)
        finish(k, k % slots)


def _inproj_kernel(x_ref, nw_ref, wq_ref, wk_ref, wv_ref, wz_ref, rc_ref, rs1_ref, rs2_ref,
                   q4_ref, k4_ref, v4_ref, q16_ref, k16_ref, v16_ref, z_ref, slab_ref, slab2_ref):
    tm = x_ref.shape[0]
    h = _rms(x_ref[...], nw_ref[...]).astype(BF16)
    rc, rs1, rs2 = rc_ref[...], rs1_ref[...], rs2_ref[...]
    n_cb = ATTN_WIDTH // LANES

    def emit(idx, w_ref, out_refs, rope):
        t = jnp.dot(h, w_ref[...], preferred_element_type=F32)
        for cb in range(n_cb):
            cols = slice(cb * LANES, (cb + 1) * LANES)
            blk = t[:, cols]
            if rope:
                lo = pltpu.roll(blk, ROPE_DIM // 2, 1)
                hi = pltpu.roll(blk, LANES - ROPE_DIM // 2, 1)
                blk = blk * rc + lo * rs1 + hi * rs2
            slab = slab_ref.at[idx * n_cb + cb]
            slab2 = slab2_ref.at[idx * n_cb + cb]
            slab[...] = blk
            n4 = tm // CARRY_SPLIT
            n16 = n4 // CARRY_SPLIT
            for r4 in range(CARRY_SPLIT):
                p4 = slab[pl.ds(r4, n4, stride=CARRY_SPLIT), :]
                out_refs[0][0, r4, :, cols] = p4.astype(BF16)
                slab2[r4 * n4:(r4 + 1) * n4, :] = p4
                for j in range(CARRY_SPLIT):
                    out_refs[1][0, CARRY_SPLIT * j + r4, :, cols] = (
                        slab2[pl.ds(r4 * n4 + j, n16, stride=CARRY_SPLIT), :].astype(BF16))

    emit(0, wq_ref, (q4_ref, q16_ref), True)
    emit(1, wk_ref, (k4_ref, k16_ref), True)
    emit(2, wv_ref, (v4_ref, v16_ref), False)
    z_ref[...] = jnp.dot(h, wz_ref[...], preferred_element_type=F32).astype(BF16)


def _rope_tables(seq):
    half = ROPE_DIM // 2
    inv_freq = jnp.power(ROPE_THETA, -jnp.arange(half, dtype=F32) * 2.0 / ROPE_DIM)
    ang = jnp.arange(seq, dtype=F32)[:, None] * inv_freq[None, :]
    cos, sin = jnp.cos(ang), jnp.sin(ang)
    one = jnp.ones((seq, HEAD_DIM - ROPE_DIM), F32)
    zero8 = jnp.zeros((seq, half), F32)
    zero = jnp.zeros((seq, HEAD_DIM - ROPE_DIM), F32)
    rc = jnp.concatenate([cos, cos, one], axis=1)
    rs1 = jnp.concatenate([zero8, sin, zero], axis=1)
    rs2 = jnp.concatenate([-sin, zero8, zero], axis=1)
    rep = LANES // HEAD_DIM
    return tuple(jnp.tile(t, (1, rep)) for t in (rc, rs1, rs2))


def _in_projection(x2d, norm_w, w_in, batch, seq, tm=512):
    t_rows = x2d.shape[0]
    a = ATTN_WIDTH
    wq = (w_in[:, :a] * (HEAD_DIM ** -0.5 * LOG2E)).astype(BF16)
    wk = w_in[:, a:2 * a].astype(BF16)
    wv = w_in[:, 2 * a:3 * a].astype(BF16)
    wz = w_in[:, 3 * a:3 * a + D_INNER].astype(BF16)
    rc, rs1, rs2 = _rope_tables(seq)
    nseq = seq // tm
    row = lambda width: pl.BlockSpec((tm, width), lambda i: (i, 0))
    tab = pl.BlockSpec((tm, LANES), lambda i: (i % nseq, 0))
    plane = lambda dil: pl.BlockSpec((1, dil, tm // dil, a), lambda i: (i // nseq, 0, i % nseq, 0))
    plane_shape = lambda dil: jax.ShapeDtypeStruct((batch, dil, seq // dil, a), BF16)
    d4, d16 = DILATIONS[1:]
    return pl.pallas_call(
        _inproj_kernel,
        grid=(t_rows // tm,),
        in_specs=[row(D_MODEL), _const_spec((1, D_MODEL)),
                  _const_spec((D_MODEL, a)), _const_spec((D_MODEL, a)), _const_spec((D_MODEL, a)),
                  _const_spec((D_MODEL, D_INNER)), tab, tab, tab],
        out_specs=[plane(d4)] * 3 + [plane(d16)] * 3 + [row(D_INNER)],
        out_shape=[plane_shape(d4)] * 3 + [plane_shape(d16)] * 3
        + [jax.ShapeDtypeStruct((t_rows, D_INNER), BF16)],
        scratch_shapes=[pltpu.VMEM((3 * a // LANES, tm, LANES), F32)] * 2,
        compiler_params=_params("parallel"),
        name="in_projection",
    )(x2d, norm_w.reshape(1, D_MODEL), wq, wk, wv, wz, rc, rs1, rs2)


SSM_HALO = BF16_ROWS
SSM_COLS = 256
SSM_SLOTS = 4


def _ssm_proj_kernel(xc_ref, xp_ref, xn_ref, nw_ref, wx_ref, wdt_ref, cw_ref, cb_ref, alog_ref, dtb_ref,
                     xa_ref, dt_ref, ldt_ref, cs_ref, lhs_scr, u_scr, *, blocks_per_seq):
    tm = xc_ref.shape[0]
    pos = pl.program_id(0) % blocks_per_seq
    nw = nw_ref[...]
    rows = tm + 2 * SSM_HALO
    h = _rms(xc_ref[...], nw).astype(BF16)
    zero_halo = jnp.zeros((SSM_HALO, D_MODEL), BF16)
    lhs_scr[0:SSM_HALO, :] = jnp.where(pos > 0, _rms(xp_ref[...], nw).astype(BF16), zero_halo)
    lhs_scr[SSM_HALO:SSM_HALO + tm, :] = h
    lhs_scr[SSM_HALO + tm:rows, :] = jnp.where(pos < blocks_per_seq - 1,
                                               _rms(xn_ref[...], nw).astype(BF16), zero_halo)
    n_chunks = XBC_WIDTH // SSM_COLS

    def chunk_cols(j):
        return pl.ds(pl.multiple_of(j * SSM_COLS, LANES), SSM_COLS)

    def project(j, slot):
        u_scr[slot] = jnp.dot(lhs_scr[...], wx_ref[:, chunk_cols(j)], preferred_element_type=F32)

    def finish(j, slot):
        y = _conv3_rows(u_scr[slot], cw_ref[:, chunk_cols(j)], cb_ref[:, chunk_cols(j)], SSM_HALO, tm)
        xa_ref[:, chunk_cols(j)] = _silu(y).astype(BF16)

    _software_pipeline(n_chunks, project, finish, depth=2, slots=SSM_SLOTS, rolled=False)

    x_dt = jnp.dot(h, wdt_ref[...], preferred_element_type=F32) + dtb_ref[...]
    dt = jnp.maximum(x_dt, 0.0) + jnp.log1p(jnp.exp(-jnp.abs(x_dt)))
    a = dt * (-jnp.exp(alog_ref[...]) * LOG2E)
    tri = (lax.broadcasted_iota(jnp.int32, (CHUNK, CHUNK), 1)
           <= lax.broadcasted_iota(jnp.int32, (CHUNK, CHUNK), 0)).astype(BF16)
    dt_ref[...] = dt
    ldt_ref[...] = jnp.log(dt) * LOG2E
    for ch in range(tm // CHUNK):
        rows_c = slice(ch * CHUNK, (ch + 1) * CHUNK)
        cs_ref[rows_c, :] = _split_dot_lhs_const(tri, a[rows_c])


def _ssm_projection(x2d, norm_w, w_in, conv_w, conv_b, a_log_f, a_log_b, dt_bias_f, dt_bias_b,
                    seq, tm=1024):
    t_rows = x2d.shape[0]
    pad = LANES - 2 * N_SSM_HEADS
    alog = jnp.pad(jnp.concatenate([a_log_f, a_log_b]), (0, pad)).reshape(1, LANES)
    dtb = jnp.pad(jnp.concatenate([dt_bias_f, dt_bias_b]), (0, pad)).reshape(1, LANES)
    o = 3 * ATTN_WIDTH + D_INNER
    wx = w_in[:, o:o + XBC_WIDTH].astype(BF16)
    wdt = jnp.pad(w_in[:, o + XBC_WIDTH:], ((0, 0), (0, pad))).astype(BF16)
    hpb = tm // SSM_HALO
    last_halo = t_rows // SSM_HALO - 1
    row = lambda width: pl.BlockSpec((tm, width), lambda i: (i, 0))
    return pl.pallas_call(
        functools.partial(_ssm_proj_kernel, blocks_per_seq=seq // tm),
        grid=(t_rows // tm,),
        in_specs=[row(D_MODEL),
                  pl.BlockSpec((SSM_HALO, D_MODEL), lambda i: (jnp.maximum(i * hpb - 1, 0), 0)),
                  pl.BlockSpec((SSM_HALO, D_MODEL), lambda i: (jnp.minimum((i + 1) * hpb, last_halo), 0)),
                  _const_spec((1, D_MODEL)), _const_spec((D_MODEL, XBC_WIDTH)),
                  _const_spec((D_MODEL, LANES)),
                  _const_spec((3, XBC_WIDTH)), _const_spec((1, XBC_WIDTH)),
                  _const_spec((1, LANES)), _const_spec((1, LANES))],
        out_specs=[row(XBC_WIDTH), row(LANES), row(LANES), row(LANES)],
        out_shape=[jax.ShapeDtypeStruct((t_rows, XBC_WIDTH), BF16)]
        + [jax.ShapeDtypeStruct((t_rows, LANES), F32)] * 3,
        scratch_shapes=[pltpu.VMEM((tm + 2 * SSM_HALO, D_MODEL), BF16),
                        pltpu.VMEM((SSM_SLOTS, tm + 2 * SSM_HALO, SSM_COLS), F32)],
        compiler_params=_params("parallel"),
        name="ssm_projection",
    )(x2d, x2d, x2d, norm_w.reshape(1, D_MODEL), wx, wdt, conv_w.T, conv_b.reshape(1, XBC_WIDTH),
      alog, dtb)


ATT_TQ = 128
ATT_TK = ATT_TQ + 2 * BAND_HALF
ATT_UNITS_PER_STEP = 32
ATT_STAT_PAIRS = 4
ATT_QUERIES_PER_STEP = 1024


def _attn_kernel(*refs, length, n_sub, n_pairs, n_planes, first, last):
    interleaved = first
    q_ref, k_ref, v_ref = refs[:3]
    if first:
        o_ref, st_ref = refs[3:]
    elif last:
        op_ref, sp_ref, o_ref = refs[3:]
    else:
        op_ref, sp_ref, o_ref, st_ref, slab_ref = refs[3:]
    tq, tk = ATT_TQ, ATT_TK
    qi = pl.program_id(3)
    lane = lax.broadcasted_iota(jnp.int32, (tq, LANES), 1)
    even = lane < HEAD_DIM
    stat_is_max = (lane & 15) < 8
    row_i = lax.broadcasted_iota(jnp.int32, (tq, tk), 0)
    col_i = lax.broadcasted_iota(jnp.int32, (tq, tk), 1)
    nt = (((1,), (1,)), ((), ()))
    ones = jnp.ones((tk, LANES), BF16)
    sub_rows = tq // CARRY_SPLIT
    key_rows = tk // CARRY_SPLIT

    if interleaved:
        delta = (CARRY_SPLIT * ((col_i & (key_rows - 1)) - (row_i & (sub_rows - 1)))
                 + (col_i // key_rows - row_i // sub_rows))
    else:
        delta = col_i - row_i

    def window_start(sb):
        if interleaved:
            a0 = (qi * n_sub + sb) * sub_rows
            ws = pl.multiple_of(jnp.clip(a0 - BAND_HALF // CARRY_SPLIT, 0, length - key_rows),
                                BF16_ROWS)
            return ws, CARRY_SPLIT * (ws - a0)
        q0 = (qi * n_sub + sb % n_sub) * tq
        ws = pl.multiple_of(jnp.clip(q0 - BAND_HALF, 0, length - tk), BAND_HALF)
        return ws, ws - q0

    def load_q(sb, cols):
        if interleaved:
            return jnp.concatenate([q_ref[0, j, sb * sub_rows:(sb + 1) * sub_rows, cols]
                                    for j in range(CARRY_SPLIT)], axis=0)
        return q_ref[0, sb // n_sub, (sb % n_sub) * tq:(sb % n_sub + 1) * tq, cols]

    def load_keys(ref, sb, ws, cols):
        if interleaved:
            return jnp.concatenate([ref[0, j, pl.ds(ws, key_rows), cols]
                                    for j in range(CARRY_SPLIT)], axis=0)
        return ref[0, sb // n_sub, pl.ds(ws, tk), cols]

    scores, row_max = [], []
    for sb in range(n_planes * n_sub):
        ws, offset = window_start(sb)
        bias = jnp.where(jnp.abs(delta + offset) <= BAND_HALF, 0.0, NEG_BIG)
        bias2 = jnp.concatenate([bias, bias], axis=0)
        for hp in range(n_pairs):
            cols = slice(hp * LANES, (hp + 1) * LANES)
            q2 = load_q(sb, cols)
            zero = jnp.zeros_like(q2)
            qq = jnp.concatenate([jnp.where(even, q2, zero), jnp.where(even, zero, q2)], axis=0)
            k2 = load_keys(k_ref, sb, ws, cols)
            s = lax.dot_general(qq, k2, nt, preferred_element_type=F32) + bias2
            scores.append(s)
            row_max.append(jnp.max(s, axis=-1, keepdims=True))

    for sb in range(n_planes * n_sub):
        ws, _ = window_start(sb)
        plane = sb // n_sub
        rows = slice((sb % n_sub) * tq, (sb % n_sub + 1) * tq)
        out_rows = slice(sb * sub_rows, (sb + 1) * sub_rows)
        stats = jnp.zeros((tq, LANES), F32)
        for hp in range(n_pairs):
            u = sb * n_pairs + hp
            cols = slice(hp * LANES, (hp + 1) * LANES)
            v_ext = jnp.concatenate([load_keys(v_ref, sb, ws, cols), ones], axis=1)
            p = jnp.exp2(scores[u] - row_max[u]).astype(BF16)
            pv = jnp.dot(p, v_ext, preferred_element_type=F32)
            acc = jnp.where(even, pv[:tq, :LANES], pv[tq:, :LANES])
            l = jnp.where(even, pv[:tq, LANES:], pv[tq:, LANES:])
            m = jnp.where(even, row_max[u][:tq], row_max[u][tq:])
            if not first:
                sp = sp_ref[0, plane, rows, :]
                be = LANES * (hp // ATT_STAT_PAIRS) + 16 * (hp % ATT_STAT_PAIRS)
                bo = be + HEAD_DIM
                m_prev = jnp.where(even, sp[:, be:be + 1], sp[:, bo:bo + 1])
                l_prev = jnp.where(even, sp[:, be + 8:be + 9], sp[:, bo + 8:bo + 9])
                acc_prev = op_ref[0, plane, rows, cols].astype(F32)
                m_new = jnp.maximum(m_prev, m)
                a_prev = jnp.exp2(m_prev - m_new)
                a_cur = jnp.exp2(m - m_new)
                acc = acc_prev * a_prev + acc * a_cur
                l = l_prev * a_prev + l * a_cur
                m = m_new
            if last:
                o_ref[0, plane, rows, cols] = (acc / l).astype(BF16)
                continue
            in_zone = ((lane & (HEAD_DIM - 1)) >> 4) == hp
            stats = jnp.where(in_zone, jnp.where(stat_is_max, m, l), stats)
            if interleaved:
                for j in range(CARRY_SPLIT):
                    o_ref[0, j, out_rows, cols] = acc[j * sub_rows:(j + 1) * sub_rows].astype(BF16)
            else:
                slab_ref[hp] = acc
                for j in range(CARRY_SPLIT):
                    o_ref[0, j, 0, out_rows, cols] = (
                        slab_ref[hp, pl.ds(j, sub_rows, stride=CARRY_SPLIT), :].astype(BF16))
        if last:
            continue
        if interleaved:
            for j in range(CARRY_SPLIT):
                st_ref[0, j, out_rows, :] = stats[j * sub_rows:(j + 1) * sub_rows]
        else:
            slab_ref[n_pairs] = stats
            for j in range(CARRY_SPLIT):
                st_ref[0, j, 0, out_rows, :] = (
                    slab_ref[n_pairs, pl.ds(j, sub_rows, stride=CARRY_SPLIT), :])


def _attention_pattern(q, k, v, o_prev, st_prev, first, last):
    cs = CARRY_SPLIT
    if first:
        b, _, length, width = q.shape
        dil, qs = 1, min(length, ATT_QUERIES_PER_STEP // cs)
        n_sub = qs * cs // ATT_TQ
    else:
        b, dil, length, width = q.shape
        qs = min(length, ATT_QUERIES_PER_STEP)
        n_sub = qs // ATT_TQ
    n_pairs = min(ATT_UNITS_PER_STEP // n_sub, width // LANES)
    hw = n_pairs * LANES
    n_hg = width // hw
    n_stat = n_pairs // ATT_STAT_PAIRS
    assert last or n_stat == 1
    n_planes = min(max(ATT_UNITS_PER_STEP // (n_sub * n_pairs), 1), dil) if last else 1
    scratch = []
    if first:
        blk = pl.BlockSpec((1, cs, qs, hw), lambda bi, r, g, qi: (bi, 0, qi, g))
        seq = pl.BlockSpec((1, cs, length, hw), lambda bi, r, g, qi: (bi, 0, 0, g))
        stat = pl.BlockSpec((1, cs, qs, LANES), lambda bi, r, g, qi: (bi, 0, qi, g))
    else:
        blk = pl.BlockSpec((1, n_planes, qs, hw), lambda bi, r, g, qi: (bi, r, qi, g))
        seq = pl.BlockSpec((1, n_planes, length, hw), lambda bi, r, g, qi: (bi, r, 0, g))
        stat = pl.BlockSpec((1, n_planes, qs, n_stat * LANES), lambda bi, r, g, qi: (bi, r, qi, g))
    in_specs, args = [blk, seq, seq], [q, k, v]
    if not first:
        in_specs += [blk, stat]
        args += [o_prev, st_prev]
    if first:
        out_specs = [blk, stat]
        out_shape = [jax.ShapeDtypeStruct((b, cs, length, width), BF16),
                     jax.ShapeDtypeStruct((b, cs, length, n_hg * LANES), F32)]
    elif last:
        out_specs = [blk]
        out_shape = [jax.ShapeDtypeStruct((b, dil, length, width), BF16)]
    else:
        carry = lambda w: pl.BlockSpec((1, cs, 1, qs // cs, w), lambda bi, r, g, qi: (bi, 0, r, qi, g))
        out_specs = [carry(hw), carry(LANES)]
        out_shape = [jax.ShapeDtypeStruct((b, cs, dil, length // cs, width), BF16),
                     jax.ShapeDtypeStruct((b, cs, dil, length // cs, n_hg * LANES), F32)]
        scratch.append(pltpu.VMEM((n_pairs + 1, ATT_TQ, LANES), F32))
    outs = pl.pallas_call(
        functools.partial(_attn_kernel, length=length, n_sub=n_sub, n_pairs=n_pairs,
                          n_planes=n_planes, first=first, last=last),
        grid=(b, dil // n_planes, n_hg, length // qs),
        in_specs=in_specs, out_specs=out_specs, out_shape=out_shape,
        scratch_shapes=scratch,
        compiler_params=_params("parallel", "parallel", "parallel", "arbitrary"),
        name=f"attention_dil{dil}",
    )(*args)
    if last:
        return outs[0], None
    if first:
        return outs[0], outs[1]
    nxt = lambda t: t.reshape(b, cs * dil, length // cs, t.shape[-1])
    return nxt(outs[0]), nxt(outs[1])


def _dilated_attention(qkv_planes):
    o = st = None
    for i, (q, k, v) in enumerate(qkv_planes):
        o, st = _attention_pattern(q, k, v, o, st, i == 0, i == len(qkv_planes) - 1)
    return o


def _split_dot(v, mat, passes):
    out = None
    r = v
    for i in range(passes):
        piece = r.astype(BF16)
        term = jnp.dot(piece, mat, preferred_element_type=F32)
        out = term if out is None else out + term
        if i + 1 < passes:
            r = r - piece.astype(F32)
    return out


def _split_dot_lhs_const(mat, v):
    out = None
    r = v
    for i in range(3):
        piece = r.astype(BF16)
        term = jnp.dot(mat, piece, preferred_element_type=F32)
        out = term if out is None else out + term
        if i < 2:
            r = r - piece.astype(F32)
    return out


def _ssm_kernel(xa_ref, dt_ref, ldt_ref, cs_ref, z_ref, alog_ref, dsk_ref, nw_ref, ef_ref, eb_ref,
                y_ref, hb_ref, bt_c, hrun_ref, *, n_steps):
    L = CHUNK
    cps = SSM_CHUNKS_PER_STEP
    step = pl.program_id(1)
    is_bwd = step < n_steps
    blk = jnp.where(is_bwd, n_steps - 1 - step, step - n_steps)

    @pl.when((step == 0) | (step == n_steps))
    def _():
        hrun_ref[...] = jnp.zeros_like(hrun_ref)

    n_bc = N_SSM_GROUPS * D_STATE
    gw = D_INNER // N_SSM_GROUPS
    li = lax.broadcasted_iota(jnp.int32, (L, L), 0)
    si = lax.broadcasted_iota(jnp.int32, (L, L), 1)
    a_log2 = -jnp.exp(alog_ref[...]) * LOG2E

    def scalars(rows):
        dt, ldt, cs = dt_ref[0, rows, :], ldt_ref[0, rows, :], cs_ref[0, rows, :]
        return dt, ldt, cs, cs - dt * a_log2, cs[L - 1:L, :]

    def state_update(xs, bts, weights, tot, e_ref):
        xw = (_split_dot(weights, e_ref[...], 1) * xs).astype(BF16)
        decay = _split_dot(jnp.broadcast_to(jnp.exp2(tot), (8, LANES)), e_ref[...], 3)[0:1]
        for g in range(N_SSM_GROUPS):
            cols = slice(g * gw, (g + 1) * gw)
            s_g = jnp.dot(bts[g], xw[:, cols], preferred_element_type=F32)
            hrun_ref[:, cols] = hrun_ref[:, cols] * decay[:, cols] + s_g

    def bwd_chunk(i, carry):
        ci = cps - 1 - i
        c = blk * cps + ci
        rows = pl.ds(pl.multiple_of(ci * L, L), L)
        dt, ldt, cs, ecs, tot = scalars(rows)
        hb_ref[c] = hrun_ref[...].astype(BF16)
        xs = xa_ref[0, rows, 0:D_INNER].astype(F32)
        bm = xa_ref[0, rows, D_INNER:D_INNER + n_bc].astype(F32)
        bts = [bm[:, g * D_STATE:(g + 1) * D_STATE].T.astype(BF16) for g in range(N_SSM_GROUPS)]
        bt_c[c] = jnp.concatenate(bts, axis=1)
        state_update(xs, bts, jnp.exp2(ecs + ldt), tot, eb_ref)
        return carry

    def fwd_chunk(ci, carry):
        c = blk * cps + ci
        rows = pl.ds(pl.multiple_of(ci * L, L), L)
        dt, ldt, cs, ecs, tot = scalars(rows)
        xs_b = xa_ref[0, rows, 0:D_INNER]
        xs = xs_b.astype(F32)
        bts = [bt_c[c, :, g * D_STATE:(g + 1) * D_STATE] for g in range(N_SSM_GROUPS)]
        hf_in = hrun_ref[...].astype(BF16)
        hb_in = hb_ref[c]
        scale_f = _split_dot(jnp.exp2(cs), ef_ref[...], 1)
        scale_b = _split_dot(jnp.exp2(tot - ecs), eb_ref[...], 1)
        row_f = (cs - ldt).T
        row_b = (ecs + ldt).T
        dt_t = dt.T
        lane = lax.broadcasted_iota(jnp.int32, (L, LANES), 1)
        even = lane < HEAD_DIM
        below = si < li
        diag = si == li
        hpg = N_SSM_HEADS // N_SSM_GROUPS
        y_groups = []
        for g in range(N_SSM_GROUPS):
            cols = slice(g * gw, (g + 1) * gw)
            cg = xa_ref[0, rows, D_INNER + n_bc + g * D_STATE:D_INNER + n_bc + (g + 1) * D_STATE]
            bg = xa_ref[0, rows, D_INNER + g * D_STATE:D_INNER + (g + 1) * D_STATE]
            gmat = lax.dot_general(cg, bg, (((1,), (1,)), ((), ())), preferred_element_type=F32)
            y_off = (scale_f[:, cols] * jnp.dot(cg, hf_in[:, cols], preferred_element_type=F32)
                     + scale_b[:, cols] * jnp.dot(cg, hb_in[:, cols], preferred_element_type=F32))
            pairs = []
            for pr in range(hpg // 2):
                h0 = g * hpg + 2 * pr
                xs_pair = xs_b[:, h0 * HEAD_DIM:(h0 + 2) * HEAD_DIM]
                res = []
                for h in (h0, h0 + 1):
                    hb = N_SSM_HEADS + h
                    sel = jnp.where(below, cs[:, h:h + 1] - row_f[h:h + 1, :],
                                    row_b[hb:hb + 1, :] - ecs[:, hb:hb + 1])
                    e = jnp.exp2(sel) + jnp.where(diag, dt_t[h:h + 1, :], 0.0)
                    w = (gmat * e).astype(BF16)
                    res.append(jnp.dot(w, xs_pair, preferred_element_type=F32))
                pairs.append(jnp.where(even, res[0], res[1]))
            y_groups.append(jnp.concatenate(pairs, axis=1) + y_off)
        y = jnp.concatenate(y_groups, axis=1) + dsk_ref[...] * xs
        zf = z_ref[0, rows, :].astype(F32)
        gated = y * _silu(zf)
        outs = []
        for g in range(N_SSM_GROUPS):
            gg = gated[:, g * gw:(g + 1) * gw]
            outs.append(gg * lax.rsqrt(jnp.mean(gg * gg, axis=-1, keepdims=True) + EPS))
        y_ref[0, rows, :] = (jnp.concatenate(outs, axis=1) * nw_ref[...]).astype(BF16)
        state_update(xs, bts, jnp.exp2(tot - cs + ldt), tot, ef_ref)
        return carry

    @pl.when(is_bwd)
    def _():
        lax.fori_loop(0, cps, bwd_chunk, 0, unroll=SSM_BWD_UNROLL)

    @pl.when(jnp.logical_not(is_bwd))
    def _():
        lax.fori_loop(0, cps, fwd_chunk, 0, unroll=SSM_FWD_UNROLL)


SSM_CHUNKS_PER_STEP = 16
SSM_FWD_UNROLL = 4
SSM_BWD_UNROLL = 8


def _head_expanders():
    col_head = np.arange(D_INNER) // HEAD_DIM
    rows = np.arange(LANES)[:, None]
    ef = (rows == col_head[None, :]).astype(np.float32)
    eb = (rows == col_head[None, :] + N_SSM_HEADS).astype(np.float32)
    return jnp.asarray(ef, BF16), jnp.asarray(eb, BF16)


def _ssm_mixer(z, xa, dt, ldt, cs, a_log_f, a_log_b, d_skip, norm_w):
    b, s, _ = z.shape
    n = s // CHUNK
    pad = LANES - 2 * N_SSM_HEADS
    alog = jnp.pad(jnp.concatenate([a_log_f, a_log_b]), (0, pad)).reshape(1, LANES)
    dsk = jnp.repeat(d_skip, HEAD_DIM).reshape(1, D_INNER)
    ef, eb = _head_expanders()

    rows = SSM_CHUNKS_PER_STEP * CHUNK
    n_steps = s // rows

    def block_of(st):
        return jnp.where(st < n_steps, n_steps - 1 - st, st - n_steps)

    per_chunk = pl.BlockSpec((1, rows, LANES), lambda bi, st: (bi, block_of(st), 0))
    fwd_only = pl.BlockSpec((1, rows, D_INNER), lambda bi, st: (bi, jnp.maximum(st - n_steps, 0), 0))
    return pl.pallas_call(
        functools.partial(_ssm_kernel, n_steps=n_steps),
        grid=(b, 2 * n_steps),
        in_specs=[
            pl.BlockSpec((1, rows, XBC_WIDTH), lambda bi, st: (bi, block_of(st), 0)),
            per_chunk, per_chunk, per_chunk, fwd_only,
            _const_spec((1, LANES)), _const_spec((1, D_INNER)), _const_spec((1, D_INNER)),
            _const_spec((LANES, D_INNER)), _const_spec((LANES, D_INNER)),
        ],
        out_specs=fwd_only,
        out_shape=jax.ShapeDtypeStruct((b, s, D_INNER), BF16),
        scratch_shapes=[pltpu.VMEM((n, D_STATE, D_INNER), BF16),
                        pltpu.VMEM((n, D_STATE, N_SSM_GROUPS * CHUNK), BF16),
                        pltpu.VMEM((D_STATE, D_INNER), F32)],
        compiler_params=_params("arbitrary", "arbitrary"),
        name="ssm_mixer",
    )(xa, dt, ldt, cs, z, alog, dsk, norm_w.reshape(1, D_INNER), ef, eb)


def _outproj_kernel(x_ref, a_ref, s_ref, wa_ref, ws_ref, nw_ref, x1_ref, h_ref, slab_ref):
    tm = x_ref.shape[0]
    dil = a_ref.shape[1]
    n_cb = ATTN_WIDTH // LANES
    for r in range(dil):
        blk = a_ref[0, r].astype(F32)
        for cb in range(n_cb):
            slab_ref[cb, pl.ds(r, tm // dil, stride=dil), :] = blk[:, cb * LANES:(cb + 1) * LANES]
    attn = jnp.concatenate([slab_ref[cb] for cb in range(n_cb)], axis=1).astype(BF16)
    x1 = (x_ref[...] + jnp.dot(attn, wa_ref[...], preferred_element_type=F32)
          + jnp.dot(s_ref[...], ws_ref[...], preferred_element_type=F32))
    x1_ref[...] = x1
    h_ref[...] = _rms(x1, nw_ref[...]).astype(BF16)


def _out_projection(x2d, attn_planes, ssm, w_out, norm_w, seq, tm=1024):
    t_rows = x2d.shape[0]
    dil = attn_planes.shape[1]
    nseq = seq // tm
    row = pl.BlockSpec((tm, D_MODEL), lambda i: (i, 0))
    planes = pl.BlockSpec((1, dil, tm // dil, ATTN_WIDTH), lambda i: (i // nseq, 0, i % nseq, 0))
    wa = w_out[:ATTN_WIDTH].astype(BF16)
    ws = w_out[ATTN_WIDTH:].astype(BF16)
    return pl.pallas_call(
        _outproj_kernel,
        grid=(t_rows // tm,),
        in_specs=[row, planes, row, _const_spec((ATTN_WIDTH, D_MODEL)), _const_spec((D_INNER, D_MODEL)),
                  _const_spec((1, D_MODEL))],
        out_specs=[row, row],
        out_shape=[jax.ShapeDtypeStruct((t_rows, D_MODEL), F32),
                   jax.ShapeDtypeStruct((t_rows, D_MODEL), BF16)],
        scratch_shapes=[pltpu.VMEM((ATTN_WIDTH // LANES, tm, LANES), F32)],
        compiler_params=_params("parallel"),
        name="out_projection",
    )(x2d, attn_planes, ssm, wa, ws, norm_w.reshape(1, D_MODEL))


FFN_COLS = 256
FFN_HALO = BF16_ROWS
FFN_SLOTS = 2


def _ffn_up_kernel(hc_ref, hp_ref, hn_ref, w_ref, cw_ref, cb_ref, act_ref, lhs_scr, u_scr, *,
                   blocks_per_seq):
    tm = hc_ref.shape[0]
    pos = pl.program_id(0) % blocks_per_seq
    rows = tm + 2 * FFN_HALO
    lhs_scr[0:FFN_HALO, :] = jnp.where(pos > 0, hp_ref[...], jnp.zeros_like(hp_ref))
    lhs_scr[FFN_HALO:FFN_HALO + tm, :] = hc_ref[...]
    lhs_scr[FFN_HALO + tm:rows, :] = jnp.where(pos < blocks_per_seq - 1, hn_ref[...],
                                               jnp.zeros_like(hn_ref))

    n_chunks = D_FF // FFN_COLS

    def chunk_cols(j, half):
        return pl.ds(pl.multiple_of(j * FFN_COLS + half * D_FF, LANES), FFN_COLS)

    def project(j, slot):
        for half in range(2):
            u_scr[2 * slot + half] = jnp.dot(lhs_scr[...], w_ref[:, chunk_cols(j, half)],
                                             preferred_element_type=F32)

    def conv(j, slot, half):
        return _conv3_rows(u_scr[2 * slot + half], cw_ref[:, chunk_cols(j, half)],
                           cb_ref[:, chunk_cols(j, half)], FFN_HALO, tm)

    def finish(j, slot):
        act_ref[:, chunk_cols(j, 0)] = (_silu(conv(j, slot, 0)) * conv(j, slot, 1)).astype(BF16)

    _software_pipeline(n_chunks, project, finish, depth=1, slots=FFN_SLOTS, rolled=True)


def _ffn_up(h, w_up, conv_w, conv_b, seq, tm=1024):
    t_rows = h.shape[0]
    hpb = tm // FFN_HALO
    last_halo = t_rows // FFN_HALO - 1
    width = 2 * D_FF
    return pl.pallas_call(
        functools.partial(_ffn_up_kernel, blocks_per_seq=seq // tm),
        grid=(t_rows // tm,),
        in_specs=[
            pl.BlockSpec((tm, D_MODEL), lambda i: (i, 0)),
            pl.BlockSpec((FFN_HALO, D_MODEL), lambda i: (jnp.maximum(i * hpb - 1, 0), 0)),
            pl.BlockSpec((FFN_HALO, D_MODEL), lambda i: (jnp.minimum((i + 1) * hpb, last_halo), 0)),
            _const_spec((D_MODEL, width)), _const_spec((3, width)), _const_spec((1, width)),
        ],
        out_specs=pl.BlockSpec((tm, D_FF), lambda i: (i, 0)),
        out_shape=jax.ShapeDtypeStruct((t_rows, D_FF), BF16),
        scratch_shapes=[pltpu.VMEM((tm + 2 * FFN_HALO, D_MODEL), BF16),
                        pltpu.VMEM((2 * FFN_SLOTS, tm + 2 * FFN_HALO, FFN_COLS), F32)],
        compiler_params=_params("parallel"),
        name="ffn_up",
    )(h, h, h, w_up.astype(BF16), conv_w.T, conv_b.reshape(1, width))


def _ffn_down_kernel(a_ref, x1_ref, wd_ref, nw_ref, o_ref):
    acc = x1_ref[...] + jnp.dot(a_ref[...], wd_ref[...], preferred_element_type=F32)
    o_ref[...] = _rms(acc, nw_ref[...])


def _ffn_down(act, x1, w_down, norm_w, tm=1024):
    t_rows = act.shape[0]
    return pl.pallas_call(
        _ffn_down_kernel,
        grid=(t_rows // tm,),
        in_specs=[pl.BlockSpec((tm, D_FF), lambda i: (i, 0)), pl.BlockSpec((tm, D_MODEL), lambda i: (i, 0)),
                  _const_spec((D_FF, D_MODEL)), _const_spec((1, D_MODEL))],
        out_specs=pl.BlockSpec((tm, D_MODEL), lambda i: (i, 0)),
        out_shape=jax.ShapeDtypeStruct((t_rows, D_MODEL), F32),
        compiler_params=_params("parallel"),
        name="ffn_down",
    )(act, x1, w_down.astype(BF16), norm_w.reshape(1, D_MODEL))


def kernel(x, norm1_w, w_in, ssm_conv_w, ssm_conv_b, a_log_f, a_log_b, dt_bias_f, dt_bias_b, d_skip,
           ssm_norm_w, w_out, norm2_w, w_up, ffn_conv_w, ffn_conv_b, w_down, final_norm_w):
    b, s, d = x.shape
    depth = w_in.shape[0]
    x2d = x.reshape(b * s, d)
    for layer in range(depth):
        (q4, k4, v4, q16, k16, v16, z) = _in_projection(x2d, norm1_w[layer], w_in[layer], b, s)
        xa, dt, ldt, cs = _ssm_projection(
            x2d, norm1_w[layer], w_in[layer], ssm_conv_w[layer], ssm_conv_b[layer],
            a_log_f[layer], a_log_b[layer], dt_bias_f[layer], dt_bias_b[layer], s)
        attn = _dilated_attention([(q4, k4, v4), (q4, k4, v4), (q16, k16, v16)])
        sh = lambda t: t.reshape(b, s, t.shape[-1])
        ssm = _ssm_mixer(sh(z), sh(xa), sh(dt), sh(ldt), sh(cs), a_log_f[layer], a_log_b[layer],
                         d_skip[layer], ssm_norm_w[layer])
        x1, h2 = _out_projection(x2d, attn, ssm.reshape(b * s, -1), w_out[layer], norm2_w[layer], s)
        act = _ffn_up(h2, w_up[layer], ffn_conv_w[layer], ffn_conv_b[layer], s)
        assert depth == 1
        x2d = _ffn_down(act, x1, w_down[layer], final_norm_w)
    return x2d.reshape(b, s, d)
```

```python
import functools

import numpy as np
import jax
import jax.numpy as jnp
from jax import lax
from jax.experimental import pallas as pl
from jax.experimental.pallas import tpu as pltpu

F32 = jnp.float32
BF16 = jnp.bfloat16

D_MODEL = 1024
HEAD_DIM = 64
N_ATTN_HEADS = 16
ATTN_WIDTH = N_ATTN_HEADS * HEAD_DIM
ROPE_DIM = HEAD_DIM // 4
ROPE_THETA = 500000.0
DILATIONS = (1, 4, 16)
BAND_HALF = 64
CARRY_SPLIT = 4

D_INNER = 1024
N_SSM_HEADS = 16
N_SSM_GROUPS = 4
D_STATE = 128
CHUNK = 128
XBC_WIDTH = D_INNER + 2 * N_SSM_GROUPS * D_STATE
D_FF = 2816
EPS = 1e-6

LANES = 128
BF16_ROWS = 16
VMEM_LIMIT = 56 * 1024 * 1024
NEG_BIG = -1e30
_NT = (((1,), (1,)), ((), ()))
LOG2E = 1.4426950408889634


def _params(*sem):
    return pltpu.CompilerParams(dimension_semantics=sem, vmem_limit_bytes=VMEM_LIMIT)


def _const_spec(shape):
    return pl.BlockSpec(shape, lambda *_: (0,) * len(shape))


def _rms(x, w):
    return x * lax.rsqrt(jnp.mean(x * x, axis=-1, keepdims=True) + EPS) * w


def _silu(y):
    h = 0.5 * y
    return h + h * jnp.tanh(h)


def _conv3_rows(u, w, b, halo, rows):
    r = u.shape[0]
    y = pltpu.roll(u, 1, 0) * w[0:1] + u * w[1:2] + pltpu.roll(u, r - 1, 0) * w[2:3] + b
    return y[halo:halo + rows]


def _software_pipeline(n_chunks, project, finish, depth, slots, rolled):
    for k in range(min(depth, n_chunks)):
        project(k, k % slots)
    n_rolled = max(n_chunks - depth, 0) // slots if rolled else 0

    def body(i, carry):
        k0 = i * slots
        for s in range(slots):
            project(k0 + s + depth, (s + depth) % slots)
            finish(k0 + s, s)
        return carry

    if n_rolled:
        lax.fori_loop(0, n_rolled, body, 0)
    for k in range(n_rolled * slots, n_chunks):
        if k + depth < n_chunks:
            project(k + depth, (k + depth) % slots)
        finish(k, k % slots)


def _inproj_kernel(x_ref, nw_ref, wq_ref, wk_ref, wv_ref, wz_ref, rc_ref, rs1_ref, rs2_ref,
                   q4_ref, k4_ref, v4_ref, q16_ref, k16_ref, v16_ref, z_ref, slab_ref, slab2_ref):
    tm = x_ref.shape[0]
    h = _rms(x_ref[...], nw_ref[...]).astype(BF16)
    rc, rs1, rs2 = rc_ref[...], rs1_ref[...], rs2_ref[...]
    n_cb = ATTN_WIDTH // LANES

    def emit(idx, w_ref, out_refs, rope):
        t = lax.dot_general(h, w_ref[...], _NT, preferred_element_type=F32)
        for cb in range(n_cb):
            cols = slice(cb * LANES, (cb + 1) * LANES)
            blk = t[:, cols]
            if rope:
                lo = pltpu.roll(blk, ROPE_DIM // 2, 1)
                hi = pltpu.roll(blk, LANES - ROPE_DIM // 2, 1)
                blk = blk * rc + lo * rs1 + hi * rs2
            slab = slab_ref.at[idx * n_cb + cb]
            slab2 = slab2_ref.at[idx * n_cb + cb]
            slab[...] = blk
            n4 = tm // CARRY_SPLIT
            n16 = n4 // CARRY_SPLIT
            for r4 in range(CARRY_SPLIT):
                p4 = slab[pl.ds(r4, n4, stride=CARRY_SPLIT), :]
                out_refs[0][0, r4, :, cols] = p4.astype(BF16)
                slab2[r4 * n4:(r4 + 1) * n4, :] = p4
                for j in range(CARRY_SPLIT):
                    out_refs[1][0, CARRY_SPLIT * j + r4, :, cols] = (
                        slab2[pl.ds(r4 * n4 + j, n16, stride=CARRY_SPLIT), :].astype(BF16))

    emit(0, wq_ref, (q4_ref, q16_ref), True)
    emit(1, wk_ref, (k4_ref, k16_ref), True)
    emit(2, wv_ref, (v4_ref, v16_ref), False)
    z_ref[...] = lax.dot_general(h, wz_ref[...], _NT, preferred_element_type=F32).astype(BF16)


def _rope_tables(seq):
    half = ROPE_DIM // 2
    inv_freq = jnp.power(ROPE_THETA, -jnp.arange(half, dtype=F32) * 2.0 / ROPE_DIM)
    ang = jnp.arange(seq, dtype=F32)[:, None] * inv_freq[None, :]
    cos, sin = jnp.cos(ang), jnp.sin(ang)
    one = jnp.ones((seq, HEAD_DIM - ROPE_DIM), F32)
    zero8 = jnp.zeros((seq, half), F32)
    zero = jnp.zeros((seq, HEAD_DIM - ROPE_DIM), F32)
    rc = jnp.concatenate([cos, cos, one], axis=1)
    rs1 = jnp.concatenate([zero8, sin, zero], axis=1)
    rs2 = jnp.concatenate([-sin, zero8, zero], axis=1)
    rep = LANES // HEAD_DIM
    return tuple(jnp.tile(t, (1, rep)) for t in (rc, rs1, rs2))


def _in_projection(x2d, norm_w, w_in, batch, seq, tm=512):
    t_rows = x2d.shape[0]
    a = ATTN_WIDTH
    w_t = w_in.T
    wq = (w_t[:a] * (HEAD_DIM ** -0.5 * LOG2E)).astype(BF16)
    wk = w_t[a:2 * a].astype(BF16)
    wv = w_t[2 * a:3 * a].astype(BF16)
    wz = w_t[3 * a:3 * a + D_INNER].astype(BF16)
    rc, rs1, rs2 = _rope_tables(seq)
    nseq = seq // tm
    row = lambda width: pl.BlockSpec((tm, width), lambda i: (i, 0))
    tab = pl.BlockSpec((tm, LANES), lambda i: (i % nseq, 0))
    plane = lambda dil: pl.BlockSpec((1, dil, tm // dil, a), lambda i: (i // nseq, 0, i % nseq, 0))
    plane_shape = lambda dil: jax.ShapeDtypeStruct((batch, dil, seq // dil, a), BF16)
    d4, d16 = DILATIONS[1:]
    return pl.pallas_call(
        _inproj_kernel,
        grid=(t_rows // tm,),
        in_specs=[row(D_MODEL), _const_spec((1, D_MODEL)),
                  _const_spec((D_MODEL, a)), _const_spec((D_MODEL, a)), _const_spec((D_MODEL, a)),
                  _const_spec((D_MODEL, D_INNER)), tab, tab, tab],
        out_specs=[plane(d4)] * 3 + [plane(d16)] * 3 + [row(D_INNER)],
        out_shape=[plane_shape(d4)] * 3 + [plane_shape(d16)] * 3
        + [jax.ShapeDtypeStruct((t_rows, D_INNER), BF16)],
        scratch_shapes=[pltpu.VMEM((3 * a // LANES, tm, LANES), F32)] * 2,
        compiler_params=_params("parallel"),
        name="in_projection",
    )(x2d, norm_w.reshape(1, D_MODEL), wq, wk, wv, wz, rc, rs1, rs2)


SSM_HALO = BF16_ROWS
SSM_COLS = 256
SSM_SLOTS = 4


def _ssm_proj_kernel(xc_ref, xp_ref, xn_ref, nw_ref, wx_ref, wdt_ref, cw_ref, cb_ref, alog_ref, dtb_ref,
                     xa_ref, dt_ref, ldt_ref, cs_ref, lhs_scr, u_scr, *, blocks_per_seq):
    tm = xc_ref.shape[0]
    pos = pl.program_id(0) % blocks_per_seq
    nw = nw_ref[...]
    rows = tm + 2 * SSM_HALO
    h = _rms(xc_ref[...], nw).astype(BF16)
    zero_halo = jnp.zeros((SSM_HALO, D_MODEL), BF16)
    lhs_scr[0:SSM_HALO, :] = jnp.where(pos > 0, _rms(xp_ref[...], nw).astype(BF16), zero_halo)
    lhs_scr[SSM_HALO:SSM_HALO + tm, :] = h
    lhs_scr[SSM_HALO + tm:rows, :] = jnp.where(pos < blocks_per_seq - 1,
                                               _rms(xn_ref[...], nw).astype(BF16), zero_halo)
    n_chunks = XBC_WIDTH // SSM_COLS

    def chunk_cols(j):
        return pl.ds(pl.multiple_of(j * SSM_COLS, LANES), SSM_COLS)

    def project(j, slot):
        u_scr[slot] = lax.dot_general(lhs_scr[...], wx_ref[chunk_cols(j), :], _NT,
                                      preferred_element_type=F32)

    def finish(j, slot):
        y = _conv3_rows(u_scr[slot], cw_ref[:, chunk_cols(j)], cb_ref[:, chunk_cols(j)], SSM_HALO, tm)
        xa_ref[:, chunk_cols(j)] = _silu(y).astype(BF16)

    _software_pipeline(n_chunks, project, finish, depth=2, slots=SSM_SLOTS, rolled=False)

    x_dt = lax.dot_general(h, wdt_ref[...], _NT, preferred_element_type=F32) + dtb_ref[...]
    dt = jnp.maximum(x_dt, 0.0) + jnp.log1p(jnp.exp(-jnp.abs(x_dt)))
    a = dt * (-jnp.exp(alog_ref[...]) * LOG2E)
    tri = (lax.broadcasted_iota(jnp.int32, (CHUNK, CHUNK), 1)
           <= lax.broadcasted_iota(jnp.int32, (CHUNK, CHUNK), 0)).astype(BF16)
    dt_ref[...] = dt
    ldt_ref[...] = jnp.log(dt) * LOG2E
    for ch in range(tm // CHUNK):
        rows_c = slice(ch * CHUNK, (ch + 1) * CHUNK)
        cs_ref[rows_c, :] = _split_dot_lhs_const(tri, a[rows_c])


def _ssm_projection(x2d, norm_w, w_in, conv_w, conv_b, a_log_f, a_log_b, dt_bias_f, dt_bias_b,
                    seq, tm=1024):
    t_rows = x2d.shape[0]
    pad = LANES - 2 * N_SSM_HEADS
    alog = jnp.pad(jnp.concatenate([a_log_f, a_log_b]), (0, pad)).reshape(1, LANES)
    dtb = jnp.pad(jnp.concatenate([dt_bias_f, dt_bias_b]), (0, pad)).reshape(1, LANES)
    o = 3 * ATTN_WIDTH + D_INNER
    w_t = w_in.T
    wx = w_t[o:o + XBC_WIDTH].astype(BF16)
    wdt = jnp.pad(w_t[o + XBC_WIDTH:], ((0, pad), (0, 0))).astype(BF16)
    hpb = tm // SSM_HALO
    last_halo = t_rows // SSM_HALO - 1
    row = lambda width: pl.BlockSpec((tm, width), lambda i: (i, 0))
    return pl.pallas_call(
        functools.partial(_ssm_proj_kernel, blocks_per_seq=seq // tm),
        grid=(t_rows // tm,),
        in_specs=[row(D_MODEL),
                  pl.BlockSpec((SSM_HALO, D_MODEL), lambda i: (jnp.maximum(i * hpb - 1, 0), 0)),
                  pl.BlockSpec((SSM_HALO, D_MODEL), lambda i: (jnp.minimum((i + 1) * hpb, last_halo), 0)),
                  _const_spec((1, D_MODEL)), _const_spec((XBC_WIDTH, D_MODEL)),
                  _const_spec((LANES, D_MODEL)),
                  _const_spec((3, XBC_WIDTH)), _const_spec((1, XBC_WIDTH)),
                  _const_spec((1, LANES)), _const_spec((1, LANES))],
        out_specs=[row(XBC_WIDTH), row(LANES), row(LANES), row(LANES)],
        out_shape=[jax.ShapeDtypeStruct((t_rows, XBC_WIDTH), BF16)]
        + [jax.ShapeDtypeStruct((t_rows, LANES), F32)] * 3,
        scratch_shapes=[pltpu.VMEM((tm + 2 * SSM_HALO, D_MODEL), BF16),
                        pltpu.VMEM((SSM_SLOTS, tm + 2 * SSM_HALO, SSM_COLS), F32)],
        compiler_params=_params("parallel"),
        name="ssm_projection",
    )(x2d, x2d, x2d, norm_w.reshape(1, D_MODEL), wx, wdt, conv_w.T, conv_b.reshape(1, XBC_WIDTH),
      alog, dtb)


ATT_TQ = 128
ATT_TK = ATT_TQ + 2 * BAND_HALF
ATT_UNITS_PER_STEP = 32
ATT_STAT_PAIRS = 4
ATT_QUERIES_PER_STEP = 1024


def _attn_kernel(*refs, length, n_sub, n_pairs, n_planes, first, last):
    interleaved = first
    q_ref, k_ref, v_ref = refs[:3]
    if first:
        o_ref, st_ref = refs[3:]
    elif last:
        op_ref, sp_ref, o_ref = refs[3:]
    else:
        op_ref, sp_ref, o_ref, st_ref, slab_ref = refs[3:]
    tq, tk = ATT_TQ, ATT_TK
    qi = pl.program_id(3)
    lane = lax.broadcasted_iota(jnp.int32, (tq, LANES), 1)
    even = lane < HEAD_DIM
    stat_is_max = (lane & 15) < 8
    row_i = lax.broadcasted_iota(jnp.int32, (tq, tk), 0)
    col_i = lax.broadcasted_iota(jnp.int32, (tq, tk), 1)
    nt = (((1,), (1,)), ((), ()))
    ones = jnp.ones((tk, LANES), BF16)
    sub_rows = tq // CARRY_SPLIT
    key_rows = tk // CARRY_SPLIT

    if interleaved:
        delta = (CARRY_SPLIT * ((col_i & (key_rows - 1)) - (row_i & (sub_rows - 1)))
                 + (col_i // key_rows - row_i // sub_rows))
    else:
        delta = col_i - row_i

    def window_start(sb):
        if interleaved:
            a0 = (qi * n_sub + sb) * sub_rows
            ws = pl.multiple_of(jnp.clip(a0 - BAND_HALF // CARRY_SPLIT, 0, length - key_rows),
                                BF16_ROWS)
            return ws, CARRY_SPLIT * (ws - a0)
        q0 = (qi * n_sub + sb % n_sub) * tq
        ws = pl.multiple_of(jnp.clip(q0 - BAND_HALF, 0, length - tk), BAND_HALF)
        return ws, ws - q0

    def load_q(sb, cols):
        if interleaved:
            return jnp.concatenate([q_ref[0, j, sb * sub_rows:(sb + 1) * sub_rows, cols]
                                    for j in range(CARRY_SPLIT)], axis=0)
        return q_ref[0, sb // n_sub, (sb % n_sub) * tq:(sb % n_sub + 1) * tq, cols]

    def load_keys(ref, sb, ws, cols):
        if interleaved:
            return jnp.concatenate([ref[0, j, pl.ds(ws, key_rows), cols]
                                    for j in range(CARRY_SPLIT)], axis=0)
        return ref[0, sb // n_sub, pl.ds(ws, tk), cols]

    scores, row_max = [], []
    for sb in range(n_planes * n_sub):
        ws, offset = window_start(sb)
        bias = jnp.where(jnp.abs(delta + offset) <= BAND_HALF, 0.0, NEG_BIG)
        bias2 = jnp.concatenate([bias, bias], axis=0)
        for hp in range(n_pairs):
            cols = slice(hp * LANES, (hp + 1) * LANES)
            q2 = load_q(sb, cols)
            zero = jnp.zeros_like(q2)
            qq = jnp.concatenate([jnp.where(even, q2, zero), jnp.where(even, zero, q2)], axis=0)
            k2 = load_keys(k_ref, sb, ws, cols)
            s = lax.dot_general(qq, k2, nt, preferred_element_type=F32) + bias2
            scores.append(s)
            row_max.append(jnp.max(s, axis=-1, keepdims=True))

    for sb in range(n_planes * n_sub):
        ws, _ = window_start(sb)
        plane = sb // n_sub
        rows = slice((sb % n_sub) * tq, (sb % n_sub + 1) * tq)
        out_rows = slice(sb * sub_rows, (sb + 1) * sub_rows)
        stats = jnp.zeros((tq, LANES), F32)
        for hp in range(n_pairs):
            u = sb * n_pairs + hp
            cols = slice(hp * LANES, (hp + 1) * LANES)
            v_ext = jnp.concatenate([load_keys(v_ref, sb, ws, cols), ones], axis=1)
            p = jnp.exp2(scores[u] - row_max[u]).astype(BF16)
            pv = jnp.dot(p, v_ext, preferred_element_type=F32)
            acc = jnp.where(even, pv[:tq, :LANES], pv[tq:, :LANES])
            l = jnp.where(even, pv[:tq, LANES:], pv[tq:, LANES:])
            m = jnp.where(even, row_max[u][:tq], row_max[u][tq:])
            if not first:
                sp = sp_ref[0, plane, rows, :]
                be = LANES * (hp // ATT_STAT_PAIRS) + 16 * (hp % ATT_STAT_PAIRS)
                bo = be + HEAD_DIM
                m_prev = jnp.where(even, sp[:, be:be + 1], sp[:, bo:bo + 1])
                l_prev = jnp.where(even, sp[:, be + 8:be + 9], sp[:, bo + 8:bo + 9])
                acc_prev = op_ref[0, plane, rows, cols].astype(F32)
                m_new = jnp.maximum(m_prev, m)
                a_prev = jnp.exp2(m_prev - m_new)
                a_cur = jnp.exp2(m - m_new)
                acc = acc_prev * a_prev + acc * a_cur
                l = l_prev * a_prev + l * a_cur
                m = m_new
            if last:
                o_ref[0, plane, rows, cols] = (acc / l).astype(BF16)
                continue
            in_zone = ((lane & (HEAD_DIM - 1)) >> 4) == hp
            stats = jnp.where(in_zone, jnp.where(stat_is_max, m, l), stats)
            if interleaved:
                for j in range(CARRY_SPLIT):
                    o_ref[0, j, out_rows, cols] = acc[j * sub_rows:(j + 1) * sub_rows].astype(BF16)
            else:
                slab_ref[hp] = acc
                for j in range(CARRY_SPLIT):
                    o_ref[0, j, 0, out_rows, cols] = (
                        slab_ref[hp, pl.ds(j, sub_rows, stride=CARRY_SPLIT), :].astype(BF16))
        if last:
            continue
        if interleaved:
            for j in range(CARRY_SPLIT):
                st_ref[0, j, out_rows, :] = stats[j * sub_rows:(j + 1) * sub_rows]
        else:
            slab_ref[n_pairs] = stats
            for j in range(CARRY_SPLIT):
                st_ref[0, j, 0, out_rows, :] = (
                    slab_ref[n_pairs, pl.ds(j, sub_rows, stride=CARRY_SPLIT), :])


def _attention_pattern(q, k, v, o_prev, st_prev, first, last):
    cs = CARRY_SPLIT
    if first:
        b, _, length, width = q.shape
        dil, qs = 1, min(length, ATT_QUERIES_PER_STEP // cs)
        n_sub = qs * cs // ATT_TQ
    else:
        b, dil, length, width = q.shape
        qs = min(length, ATT_QUERIES_PER_STEP)
        n_sub = qs // ATT_TQ
    n_pairs = min(ATT_UNITS_PER_STEP // n_sub, width // LANES)
    hw = n_pairs * LANES
    n_hg = width // hw
    n_stat = n_pairs // ATT_STAT_PAIRS
    assert last or n_stat == 1
    n_planes = min(max(ATT_UNITS_PER_STEP // (n_sub * n_pairs), 1), dil) if last else 1
    scratch = []
    if first:
        blk = pl.BlockSpec((1, cs, qs, hw), lambda bi, r, g, qi: (bi, 0, qi, g))
        seq = pl.BlockSpec((1, cs, length, hw), lambda bi, r, g, qi: (bi, 0, 0, g))
        stat = pl.BlockSpec((1, cs, qs, LANES), lambda bi, r, g, qi: (bi, 0, qi, g))
    else:
        blk = pl.BlockSpec((1, n_planes, qs, hw), lambda bi, r, g, qi: (bi, r, qi, g))
        seq = pl.BlockSpec((1, n_planes, length, hw), lambda bi, r, g, qi: (bi, r, 0, g))
        stat = pl.BlockSpec((1, n_planes, qs, n_stat * LANES), lambda bi, r, g, qi: (bi, r, qi, g))
    in_specs, args = [blk, seq, seq], [q, k, v]
    if not first:
        in_specs += [blk, stat]
        args += [o_prev, st_prev]
    if first:
        out_specs = [blk, stat]
        out_shape = [jax.ShapeDtypeStruct((b, cs, length, width), BF16),
                     jax.ShapeDtypeStruct((b, cs, length, n_hg * LANES), F32)]
    elif last:
        out_specs = [blk]
        out_shape = [jax.ShapeDtypeStruct((b, dil, length, width), BF16)]
    else:
        carry = lambda w: pl.BlockSpec((1, cs, 1, qs // cs, w), lambda bi, r, g, qi: (bi, 0, r, qi, g))
        out_specs = [carry(hw), carry(LANES)]
        out_shape = [jax.ShapeDtypeStruct((b, cs, dil, length // cs, width), BF16),
                     jax.ShapeDtypeStruct((b, cs, dil, length // cs, n_hg * LANES), F32)]
        scratch.append(pltpu.VMEM((n_pairs + 1, ATT_TQ, LANES), F32))
    outs = pl.pallas_call(
        functools.partial(_attn_kernel, length=length, n_sub=n_sub, n_pairs=n_pairs,
                          n_planes=n_planes, first=first, last=last),
        grid=(b, dil // n_planes, n_hg, length // qs),
        in_specs=in_specs, out_specs=out_specs, out_shape=out_shape,
        scratch_shapes=scratch,
        compiler_params=_params("parallel", "parallel", "parallel", "arbitrary"),
        name=f"attention_dil{dil}",
    )(*args)
    if last:
        return outs[0], None
    if first:
        return outs[0], outs[1]
    nxt = lambda t: t.reshape(b, cs * dil, length // cs, t.shape[-1])
    return nxt(outs[0]), nxt(outs[1])


def _dilated_attention(qkv_planes):
    o = st = None
    for i, (q, k, v) in enumerate(qkv_planes):
        o, st = _attention_pattern(q, k, v, o, st, i == 0, i == len(qkv_planes) - 1)
    return o


def _split_dot(v, mat, passes):
    out = None
    r = v
    for i in range(passes):
        piece = r.astype(BF16)
        term = jnp.dot(piece, mat, preferred_element_type=F32)
        out = term if out is None else out + term
        if i + 1 < passes:
            r = r - piece.astype(F32)
    return out


def _split_dot_lhs_const(mat, v):
    out = None
    r = v
    for i in range(3):
        piece = r.astype(BF16)
        term = jnp.dot(mat, piece, preferred_element_type=F32)
        out = term if out is None else out + term
        if i < 2:
            r = r - piece.astype(F32)
    return out


def _ssm_kernel(xa_ref, dt_ref, ldt_ref, cs_ref, z_ref, alog_ref, dsk_ref, nw_ref, ef_ref, eb_ref,
                y_ref, hb_ref, bt_c, hrun_ref, *, n_steps):
    L = CHUNK
    cps = SSM_CHUNKS_PER_STEP
    step = pl.program_id(1)
    is_bwd = step < n_steps
    blk = jnp.where(is_bwd, n_steps - 1 - step, step - n_steps)

    @pl.when((step == 0) | (step == n_steps))
    def _():
        hrun_ref[...] = jnp.zeros_like(hrun_ref)

    n_bc = N_SSM_GROUPS * D_STATE
    gw = D_INNER // N_SSM_GROUPS
    li = lax.broadcasted_iota(jnp.int32, (L, L), 0)
    si = lax.broadcasted_iota(jnp.int32, (L, L), 1)
    a_log2 = -jnp.exp(alog_ref[...]) * LOG2E

    def scalars(rows):
        dt, ldt, cs = dt_ref[0, rows, :], ldt_ref[0, rows, :], cs_ref[0, rows, :]
        return dt, ldt, cs, cs - dt * a_log2, cs[L - 1:L, :]

    def state_update(xs, bts, weights, tot, e_ref):
        xw = (_split_dot(weights, e_ref[...], 1) * xs).astype(BF16)
        decay = _split_dot(jnp.broadcast_to(jnp.exp2(tot), (8, LANES)), e_ref[...], 3)[0:1]
        for g in range(N_SSM_GROUPS):
            cols = slice(g * gw, (g + 1) * gw)
            s_g = jnp.dot(bts[g], xw[:, cols], preferred_element_type=F32)
            hrun_ref[:, cols] = hrun_ref[:, cols] * decay[:, cols] + s_g

    def bwd_chunk(i, carry):
        ci = cps - 1 - i
        c = blk * cps + ci
        rows = pl.ds(pl.multiple_of(ci * L, L), L)
        dt, ldt, cs, ecs, tot = scalars(rows)
        hb_ref[c] = hrun_ref[...].astype(BF16)
        xs = xa_ref[0, rows, 0:D_INNER].astype(F32)
        bm = xa_ref[0, rows, D_INNER:D_INNER + n_bc].astype(F32)
        bts = [bm[:, g * D_STATE:(g + 1) * D_STATE].T.astype(BF16) for g in range(N_SSM_GROUPS)]
        bt_c[c] = jnp.concatenate(bts, axis=1)
        state_update(xs, bts, jnp.exp2(ecs + ldt), tot, eb_ref)
        return carry

    def fwd_chunk(ci, carry):
        c = blk * cps + ci
        rows = pl.ds(pl.multiple_of(ci * L, L), L)
        dt, ldt, cs, ecs, tot = scalars(rows)
        xs_b = xa_ref[0, rows, 0:D_INNER]
        xs = xs_b.astype(F32)
        bts = [bt_c[c, :, g * D_STATE:(g + 1) * D_STATE] for g in range(N_SSM_GROUPS)]
        hf_in = hrun_ref[...].astype(BF16)
        hb_in = hb_ref[c]
        scale_f = _split_dot(jnp.exp2(cs), ef_ref[...], 1)
        scale_b = _split_dot(jnp.exp2(tot - ecs), eb_ref[...], 1)
        row_f = (cs - ldt).T
        row_b = (ecs + ldt).T
        dt_t = dt.T
        lane = lax.broadcasted_iota(jnp.int32, (L, LANES), 1)
        even = lane < HEAD_DIM
        below = si < li
        diag = si == li
        hpg = N_SSM_HEADS // N_SSM_GROUPS
        y_groups = []
        for g in range(N_SSM_GROUPS):
            cols = slice(g * gw, (g + 1) * gw)
            cg = xa_ref[0, rows, D_INNER + n_bc + g * D_STATE:D_INNER + n_bc + (g + 1) * D_STATE]
            bg = xa_ref[0, rows, D_INNER + g * D_STATE:D_INNER + (g + 1) * D_STATE]
            gmat = lax.dot_general(cg, bg, (((1,), (1,)), ((), ())), preferred_element_type=F32)
            y_off = (scale_f[:, cols] * jnp.dot(cg, hf_in[:, cols], preferred_element_type=F32)
                     + scale_b[:, cols] * jnp.dot(cg, hb_in[:, cols], preferred_element_type=F32))
            pairs = []
            for pr in range(hpg // 2):
                h0 = g * hpg + 2 * pr
                xs_pair = xs_b[:, h0 * HEAD_DIM:(h0 + 2) * HEAD_DIM]
                res = []
                for h in (h0, h0 + 1):
                    hb = N_SSM_HEADS + h
                    sel = jnp.where(below, cs[:, h:h + 1] - row_f[h:h + 1, :],
                                    row_b[hb:hb + 1, :] - ecs[:, hb:hb + 1])
                    e = jnp.exp2(sel) + jnp.where(diag, dt_t[h:h + 1, :], 0.0)
                    w = (gmat * e).astype(BF16)
                    res.append(jnp.dot(w, xs_pair, preferred_element_type=F32))
                pairs.append(jnp.where(even, res[0], res[1]))
            y_groups.append(jnp.concatenate(pairs, axis=1) + y_off)
        y = jnp.concatenate(y_groups, axis=1) + dsk_ref[...] * xs
        zf = z_ref[0, rows, :].astype(F32)
        gated = y * _silu(zf)
        outs = []
        for g in range(N_SSM_GROUPS):
            gg = gated[:, g * gw:(g + 1) * gw]
            outs.append(gg * lax.rsqrt(jnp.mean(gg * gg, axis=-1, keepdims=True) + EPS))
        y_ref[0, rows, :] = (jnp.concatenate(outs, axis=1) * nw_ref[...]).astype(BF16)
        state_update(xs, bts, jnp.exp2(tot - cs + ldt), tot, ef_ref)
        return carry

    @pl.when(is_bwd)
    def _():
        lax.fori_loop(0, cps, bwd_chunk, 0, unroll=SSM_BWD_UNROLL)

    @pl.when(jnp.logical_not(is_bwd))
    def _():
        lax.fori_loop(0, cps, fwd_chunk, 0, unroll=SSM_FWD_UNROLL)


SSM_CHUNKS_PER_STEP = 16
SSM_FWD_UNROLL = 4
SSM_BWD_UNROLL = 8


def _head_expanders():
    col_head = np.arange(D_INNER) // HEAD_DIM
    rows = np.arange(LANES)[:, None]
    ef = (rows == col_head[None, :]).astype(np.float32)
    eb = (rows == col_head[None, :] + N_SSM_HEADS).astype(np.float32)
    return jnp.asarray(ef, BF16), jnp.asarray(eb, BF16)


def _ssm_mixer(z, xa, dt, ldt, cs, a_log_f, a_log_b, d_skip, norm_w):
    b, s, _ = z.shape
    n = s // CHUNK
    pad = LANES - 2 * N_SSM_HEADS
    alog = jnp.pad(jnp.concatenate([a_log_f, a_log_b]), (0, pad)).reshape(1, LANES)
    dsk = jnp.repeat(d_skip, HEAD_DIM).reshape(1, D_INNER)
    ef, eb = _head_expanders()

    rows = SSM_CHUNKS_PER_STEP * CHUNK
    n_steps = s // rows

    def block_of(st):
        return jnp.where(st < n_steps, n_steps - 1 - st, st - n_steps)

    per_chunk = pl.BlockSpec((1, rows, LANES), lambda bi, st: (bi, block_of(st), 0))
    fwd_only = pl.BlockSpec((1, rows, D_INNER), lambda bi, st: (bi, jnp.maximum(st - n_steps, 0), 0))
    return pl.pallas_call(
        functools.partial(_ssm_kernel, n_steps=n_steps),
        grid=(b, 2 * n_steps),
        in_specs=[
            pl.BlockSpec((1, rows, XBC_WIDTH), lambda bi, st: (bi, block_of(st), 0)),
            per_chunk, per_chunk, per_chunk, fwd_only,
            _const_spec((1, LANES)), _const_spec((1, D_INNER)), _const_spec((1, D_INNER)),
            _const_spec((LANES, D_INNER)), _const_spec((LANES, D_INNER)),
        ],
        out_specs=fwd_only,
        out_shape=jax.ShapeDtypeStruct((b, s, D_INNER), BF16),
        scratch_shapes=[pltpu.VMEM((n, D_STATE, D_INNER), BF16),
                        pltpu.VMEM((n, D_STATE, N_SSM_GROUPS * CHUNK), BF16),
                        pltpu.VMEM((D_STATE, D_INNER), F32)],
        compiler_params=_params("arbitrary", "arbitrary"),
        name="ssm_mixer",
    )(xa, dt, ldt, cs, z, alog, dsk, norm_w.reshape(1, D_INNER), ef, eb)


def _outproj_kernel(x_ref, a_ref, s_ref, wa_ref, ws_ref, nw_ref, x1_ref, h_ref, slab_ref):
    tm = x_ref.shape[0]
    dil = a_ref.shape[1]
    n_cb = ATTN_WIDTH // LANES
    for r in range(dil):
        blk = a_ref[0, r].astype(F32)
        for cb in range(n_cb):
            slab_ref[cb, pl.ds(r, tm // dil, stride=dil), :] = blk[:, cb * LANES:(cb + 1) * LANES]
    attn = jnp.concatenate([slab_ref[cb] for cb in range(n_cb)], axis=1).astype(BF16)
    x1 = (x_ref[...] + jnp.dot(attn, wa_ref[...], preferred_element_type=F32)
          + jnp.dot(s_ref[...], ws_ref[...], preferred_element_type=F32))
    x1_ref[...] = x1
    h_ref[...] = _rms(x1, nw_ref[...]).astype(BF16)


def _out_projection(x2d, attn_planes, ssm, w_out, norm_w, seq, tm=1024):
    t_rows = x2d.shape[0]
    dil = attn_planes.shape[1]
    nseq = seq // tm
    row = pl.BlockSpec((tm, D_MODEL), lambda i: (i, 0))
    planes = pl.BlockSpec((1, dil, tm // dil, ATTN_WIDTH), lambda i: (i // nseq, 0, i % nseq, 0))
    wa = w_out[:ATTN_WIDTH].astype(BF16)
    ws = w_out[ATTN_WIDTH:].astype(BF16)
    return pl.pallas_call(
        _outproj_kernel,
        grid=(t_rows // tm,),
        in_specs=[row, planes, row, _const_spec((ATTN_WIDTH, D_MODEL)), _const_spec((D_INNER, D_MODEL)),
                  _const_spec((1, D_MODEL))],
        out_specs=[row, row],
        out_shape=[jax.ShapeDtypeStruct((t_rows, D_MODEL), F32),
                   jax.ShapeDtypeStruct((t_rows, D_MODEL), BF16)],
        scratch_shapes=[pltpu.VMEM((ATTN_WIDTH // LANES, tm, LANES), F32)],
        compiler_params=_params("parallel"),
        name="out_projection",
    )(x2d, attn_planes, ssm, wa, ws, norm_w.reshape(1, D_MODEL))


FFN_COLS = 256
FFN_HALO = BF16_ROWS
FFN_SLOTS = 2


def _ffn_up_kernel(hc_ref, hp_ref, hn_ref, w_ref, cw_ref, cb_ref, act_ref, lhs_scr, u_scr, *,
                   blocks_per_seq):
    tm = hc_ref.shape[0]
    pos = pl.program_id(0) % blocks_per_seq
    rows = tm + 2 * FFN_HALO
    lhs_scr[0:FFN_HALO, :] = jnp.where(pos > 0, hp_ref[...], jnp.zeros_like(hp_ref))
    lhs_scr[FFN_HALO:FFN_HALO + tm, :] = hc_ref[...]
    lhs_scr[FFN_HALO + tm:rows, :] = jnp.where(pos < blocks_per_seq - 1, hn_ref[...],
                                               jnp.zeros_like(hn_ref))

    n_chunks = D_FF // FFN_COLS

    def chunk_cols(j, half):
        return pl.ds(pl.multiple_of(j * FFN_COLS + half * D_FF, LANES), FFN_COLS)

    def project(j, slot):
        for half in range(2):
            u_scr[2 * slot + half] = jnp.dot(lhs_scr[...], w_ref[:, chunk_cols(j, half)],
                                             preferred_element_type=F32)

    def conv(j, slot, half):
        return _conv3_rows(u_scr[2 * slot + half], cw_ref[:, chunk_cols(j, half)],
                           cb_ref[:, chunk_cols(j, half)], FFN_HALO, tm)

    def finish(j, slot):
        act_ref[:, chunk_cols(j, 0)] = (_silu(conv(j, slot, 0)) * conv(j, slot, 1)).astype(BF16)

    _software_pipeline(n_chunks, project, finish, depth=1, slots=FFN_SLOTS, rolled=True)


def _ffn_up(h, w_up, conv_w, conv_b, seq, tm=1024):
    t_rows = h.shape[0]
    hpb = tm // FFN_HALO
    last_halo = t_rows // FFN_HALO - 1
    width = 2 * D_FF
    return pl.pallas_call(
        functools.partial(_ffn_up_kernel, blocks_per_seq=seq // tm),
        grid=(t_rows // tm,),
        in_specs=[
            pl.BlockSpec((tm, D_MODEL), lambda i: (i, 0)),
            pl.BlockSpec((FFN_HALO, D_MODEL), lambda i: (jnp.maximum(i * hpb - 1, 0), 0)),
            pl.BlockSpec((FFN_HALO, D_MODEL), lambda i: (jnp.minimum((i + 1) * hpb, last_halo), 0)),
            _const_spec((D_MODEL, width)), _const_spec((3, width)), _const_spec((1, width)),
        ],
        out_specs=pl.BlockSpec((tm, D_FF), lambda i: (i, 0)),
        out_shape=jax.ShapeDtypeStruct((t_rows, D_FF), BF16),
        scratch_shapes=[pltpu.VMEM((tm + 2 * FFN_HALO, D_MODEL), BF16),
                        pltpu.VMEM((2 * FFN_SLOTS, tm + 2 * FFN_HALO, FFN_COLS), F32)],
        compiler_params=_params("parallel"),
        name="ffn_up",
    )(h, h, h, w_up.astype(BF16), conv_w.T, conv_b.reshape(1, width))


def _ffn_down_kernel(a_ref, x1_ref, wd_ref, nw_ref, o_ref):
    acc = x1_ref[...] + jnp.dot(a_ref[...], wd_ref[...], preferred_element_type=F32)
    o_ref[...] = _rms(acc, nw_ref[...])


def _ffn_down(act, x1, w_down, norm_w, tm=1024):
    t_rows = act.shape[0]
    return pl.pallas_call(
        _ffn_down_kernel,
        grid=(t_rows // tm,),
        in_specs=[pl.BlockSpec((tm, D_FF), lambda i: (i, 0)), pl.BlockSpec((tm, D_MODEL), lambda i: (i, 0)),
                  _const_spec((D_FF, D_MODEL)), _const_spec((1, D_MODEL))],
        out_specs=pl.BlockSpec((tm, D_MODEL), lambda i: (i, 0)),
        out_shape=jax.ShapeDtypeStruct((t_rows, D_MODEL), F32),
        compiler_params=_params("parallel"),
        name="ffn_down",
    )(act, x1, w_down.astype(BF16), norm_w.reshape(1, D_MODEL))


def kernel(x, norm1_w, w_in, ssm_conv_w, ssm_conv_b, a_log_f, a_log_b, dt_bias_f, dt_bias_b, d_skip,
           ssm_norm_w, w_out, norm2_w, w_up, ffn_conv_w, ffn_conv_b, w_down, final_norm_w):
    b, s, d = x.shape
    depth = w_in.shape[0]
    x2d = x.reshape(b * s, d)
    for layer in range(depth):
        (q4, k4, v4, q16, k16, v16, z) = _in_projection(x2d, norm1_w[layer], w_in[layer], b, s)
        xa, dt, ldt, cs = _ssm_projection(
            x2d, norm1_w[layer], w_in[layer], ssm_conv_w[layer], ssm_conv_b[layer],
            a_log_f[layer], a_log_b[layer], dt_bias_f[layer], dt_bias_b[layer], s)
        attn = _dilated_attention([(q4, k4, v4), (q4, k4, v4), (q16, k16, v16)])
        sh = lambda t: t.reshape(b, s, t.shape[-1])
        ssm = _ssm_mixer(sh(z), sh(xa), sh(dt), sh(ldt), sh(cs), a_log_f[layer], a_log_b[layer],
                         d_skip[layer], ssm_norm_w[layer])
        x1, h2 = _out_projection(x2d, attn, ssm.reshape(b * s, -1), w_out[layer], norm2_w[layer], s)
        act = _ffn_up(h2, w_up[layer], ffn_conv_w[layer], ffn_conv_b[layer], s)
        assert depth == 1
        x2d = _ffn_down(act, x1, w_down[layer], final_norm_w)
    return x2d.reshape(b, s, d)
```

```python
import functools

import numpy as np
import jax
import jax.numpy as jnp
from jax import lax
from jax.experimental import pallas as pl
from jax.experimental.pallas import tpu as pltpu

F32 = jnp.float32
BF16 = jnp.bfloat16

D_MODEL = 1024
HEAD_DIM = 64
N_ATTN_HEADS = 16
ATTN_WIDTH = N_ATTN_HEADS * HEAD_DIM
ROPE_DIM = HEAD_DIM // 4
ROPE_THETA = 500000.0
DILATIONS = (1, 4, 16)
BAND_HALF = 64
CARRY_SPLIT = 4

D_INNER = 1024
N_SSM_HEADS = 16
N_SSM_GROUPS = 4
D_STATE = 128
CHUNK = 128
XBC_WIDTH = D_INNER + 2 * N_SSM_GROUPS * D_STATE
D_FF = 2816
EPS = 1e-6

LANES = 128
BF16_ROWS = 16
VMEM_LIMIT = 56 * 1024 * 1024
NEG_BIG = -1e30
_NT = (((1,), (1,)), ((), ()))
LOG2E = 1.4426950408889634


def _params(*sem):
    return pltpu.CompilerParams(dimension_semantics=sem, vmem_limit_bytes=VMEM_LIMIT)


def _const_spec(shape):
    return pl.BlockSpec(shape, lambda *_: (0,) * len(shape))


def _rms(x, w):
    return x * lax.rsqrt(jnp.mean(x * x, axis=-1, keepdims=True) + EPS) * w


def _silu(y):
    h = 0.5 * y
    return h + h * jnp.tanh(h)


def _conv3_rows(u, w, b, halo, rows):
    r = u.shape[0]
    y = pltpu.roll(u, 1, 0) * w[0:1] + u * w[1:2] + pltpu.roll(u, r - 1, 0) * w[2:3] + b
    return y[halo:halo + rows]


def _software_pipeline(n_chunks, project, finish, depth, slots, rolled):
    for k in range(min(depth, n_chunks)):
        project(k, k % slots)
    n_rolled = max(n_chunks - depth, 0) // slots if rolled else 0

    def body(i, carry):
        k0 = i * slots
        for s in range(slots):
            project(k0 + s + depth, (s + depth) % slots)
            finish(k0 + s, s)
        return carry

    if n_rolled:
        lax.fori_loop(0, n_rolled, body, 0)
    for k in range(n_rolled * slots, n_chunks):
        if k + depth < n_chunks:
            project(k + depth, (k + depth) % slots)
        finish(k, k % slots)


def _inproj_kernel(x_ref, nw_ref, wq_ref, wk_ref, wv_ref, wz_ref, rc_ref, rs1_ref, rs2_ref,
                   q4_ref, k4_ref, v4_ref, q16_ref, k16_ref, v16_ref, z_ref, slab_ref, slab2_ref):
    tm = x_ref.shape[0]
    h = _rms(x_ref[...], nw_ref[...]).astype(BF16)
    rc, rs1, rs2 = rc_ref[...], rs1_ref[...], rs2_ref[...]
    n_cb = ATTN_WIDTH // LANES

    def emit(idx, w_ref, out_refs, rope, scale=None):
        t = lax.dot_general(h, w_ref[...], _NT, preferred_element_type=F32)
        if scale is not None:
            t = t * scale
        for cb in range(n_cb):
            cols = slice(cb * LANES, (cb + 1) * LANES)
            blk = t[:, cols]
            if rope:
                lo = pltpu.roll(blk, ROPE_DIM // 2, 1)
                hi = pltpu.roll(blk, LANES - ROPE_DIM // 2, 1)
                blk = blk * rc + lo * rs1 + hi * rs2
            slab = slab_ref.at[idx * n_cb + cb]
            slab2 = slab2_ref.at[idx * n_cb + cb]
            slab[...] = blk
            n4 = tm // CARRY_SPLIT
            n16 = n4 // CARRY_SPLIT
            for r4 in range(CARRY_SPLIT):
                p4 = slab[pl.ds(r4, n4, stride=CARRY_SPLIT), :]
                out_refs[0][0, r4, :, cols] = p4.astype(BF16)
                slab2[r4 * n4:(r4 + 1) * n4, :] = p4
                for j in range(CARRY_SPLIT):
                    out_refs[1][0, CARRY_SPLIT * j + r4, :, cols] = (
                        slab2[pl.ds(r4 * n4 + j, n16, stride=CARRY_SPLIT), :].astype(BF16))

    emit(0, wq_ref, (q4_ref, q16_ref), True, scale=HEAD_DIM ** -0.5 * LOG2E)
    emit(1, wk_ref, (k4_ref, k16_ref), True)
    emit(2, wv_ref, (v4_ref, v16_ref), False)
    z_ref[...] = lax.dot_general(h, wz_ref[...], _NT, preferred_element_type=F32).astype(BF16)


def _rope_tables(seq):
    half = ROPE_DIM // 2
    inv_freq = jnp.power(ROPE_THETA, -jnp.arange(half, dtype=F32) * 2.0 / ROPE_DIM)
    ang = jnp.arange(seq, dtype=F32)[:, None] * inv_freq[None, :]
    cos, sin = jnp.cos(ang), jnp.sin(ang)
    one = jnp.ones((seq, HEAD_DIM - ROPE_DIM), F32)
    zero8 = jnp.zeros((seq, half), F32)
    zero = jnp.zeros((seq, HEAD_DIM - ROPE_DIM), F32)
    rc = jnp.concatenate([cos, cos, one], axis=1)
    rs1 = jnp.concatenate([zero8, sin, zero], axis=1)
    rs2 = jnp.concatenate([-sin, zero8, zero], axis=1)
    rep = LANES // HEAD_DIM
    return tuple(jnp.tile(t, (1, rep)) for t in (rc, rs1, rs2))


def _in_projection(x2d, norm_w, w_t, batch, seq, tm=512):
    t_rows = x2d.shape[0]
    a = ATTN_WIDTH
    assert a == D_INNER
    w_block = lambda k: pl.BlockSpec((a, D_MODEL), lambda i: (k, 0))
    rc, rs1, rs2 = _rope_tables(seq)
    nseq = seq // tm
    row = lambda width: pl.BlockSpec((tm, width), lambda i: (i, 0))
    tab = pl.BlockSpec((tm, LANES), lambda i: (i % nseq, 0))
    plane = lambda dil: pl.BlockSpec((1, dil, tm // dil, a), lambda i: (i // nseq, 0, i % nseq, 0))
    plane_shape = lambda dil: jax.ShapeDtypeStruct((batch, dil, seq // dil, a), BF16)
    d4, d16 = DILATIONS[1:]
    return pl.pallas_call(
        _inproj_kernel,
        grid=(t_rows // tm,),
        in_specs=[row(D_MODEL), _const_spec((1, D_MODEL)),
                  w_block(0), w_block(1), w_block(2), w_block(3), tab, tab, tab],
        out_specs=[plane(d4)] * 3 + [plane(d16)] * 3 + [row(D_INNER)],
        out_shape=[plane_shape(d4)] * 3 + [plane_shape(d16)] * 3
        + [jax.ShapeDtypeStruct((t_rows, D_INNER), BF16)],
        scratch_shapes=[pltpu.VMEM((3 * a // LANES, tm, LANES), F32)] * 2,
        compiler_params=_params("parallel"),
        name="in_projection",
    )(x2d, norm_w.reshape(1, D_MODEL), w_t, w_t, w_t, w_t, rc, rs1, rs2)


SSM_HALO = BF16_ROWS
SSM_COLS = 256
SSM_SLOTS = 4


def _ssm_proj_kernel(xc_ref, xp_ref, xn_ref, nw_ref, wx_ref, wdt_ref, cw_ref, cb_ref, alog_ref, dtb_ref,
                     xa_ref, dt_ref, ldt_ref, cs_ref, lhs_scr, u_scr, *, blocks_per_seq):
    tm = xc_ref.shape[0]
    pos = pl.program_id(0) % blocks_per_seq
    nw = nw_ref[...]
    rows = tm + 2 * SSM_HALO
    h = _rms(xc_ref[...], nw).astype(BF16)
    zero_halo = jnp.zeros((SSM_HALO, D_MODEL), BF16)
    lhs_scr[0:SSM_HALO, :] = jnp.where(pos > 0, _rms(xp_ref[...], nw).astype(BF16), zero_halo)
    lhs_scr[SSM_HALO:SSM_HALO + tm, :] = h
    lhs_scr[SSM_HALO + tm:rows, :] = jnp.where(pos < blocks_per_seq - 1,
                                               _rms(xn_ref[...], nw).astype(BF16), zero_halo)
    n_chunks = XBC_WIDTH // SSM_COLS

    def chunk_cols(j):
        return pl.ds(pl.multiple_of(j * SSM_COLS, LANES), SSM_COLS)

    def project(j, slot):
        u_scr[slot] = lax.dot_general(lhs_scr[...], wx_ref[chunk_cols(j), :], _NT,
                                      preferred_element_type=F32)

    def finish(j, slot):
        y = _conv3_rows(u_scr[slot], cw_ref[:, chunk_cols(j)], cb_ref[:, chunk_cols(j)], SSM_HALO, tm)
        xa_ref[:, chunk_cols(j)] = _silu(y).astype(BF16)

    _software_pipeline(n_chunks, project, finish, depth=2, slots=SSM_SLOTS, rolled=False)

    x_dt = lax.dot_general(h, wdt_ref[...], _NT, preferred_element_type=F32) + dtb_ref[...]
    dt = jnp.maximum(x_dt, 0.0) + jnp.log1p(jnp.exp(-jnp.abs(x_dt)))
    a = dt * (-jnp.exp(alog_ref[...]) * LOG2E)
    tri = (lax.broadcasted_iota(jnp.int32, (CHUNK, CHUNK), 1)
           <= lax.broadcasted_iota(jnp.int32, (CHUNK, CHUNK), 0)).astype(BF16)
    dt_ref[...] = dt
    ldt_ref[...] = jnp.log(dt) * LOG2E
    for ch in range(tm // CHUNK):
        rows_c = slice(ch * CHUNK, (ch + 1) * CHUNK)
        cs_ref[rows_c, :] = _split_dot_lhs_const(tri, a[rows_c])


def _ssm_projection(x2d, norm_w, w_t, conv_w, conv_b, a_log_f, a_log_b, dt_bias_f, dt_bias_b,
                    seq, tm=1024):
    t_rows = x2d.shape[0]
    pad = LANES - 2 * N_SSM_HEADS
    alog = jnp.pad(jnp.concatenate([a_log_f, a_log_b]), (0, pad)).reshape(1, LANES)
    dtb = jnp.pad(jnp.concatenate([dt_bias_f, dt_bias_b]), (0, pad)).reshape(1, LANES)
    o = 3 * ATTN_WIDTH + D_INNER
    assert o % XBC_WIDTH == 0
    wdt = jnp.pad(w_t[o + XBC_WIDTH:], ((0, pad), (0, 0)))
    hpb = tm // SSM_HALO
    last_halo = t_rows // SSM_HALO - 1
    row = lambda width: pl.BlockSpec((tm, width), lambda i: (i, 0))
    return pl.pallas_call(
        functools.partial(_ssm_proj_kernel, blocks_per_seq=seq // tm),
        grid=(t_rows // tm,),
        in_specs=[row(D_MODEL),
                  pl.BlockSpec((SSM_HALO, D_MODEL), lambda i: (jnp.maximum(i * hpb - 1, 0), 0)),
                  pl.BlockSpec((SSM_HALO, D_MODEL), lambda i: (jnp.minimum((i + 1) * hpb, last_halo), 0)),
                  _const_spec((1, D_MODEL)),
                  pl.BlockSpec((XBC_WIDTH, D_MODEL), lambda i: (o // XBC_WIDTH, 0)),
                  _const_spec((LANES, D_MODEL)),
                  _const_spec((3, XBC_WIDTH)), _const_spec((1, XBC_WIDTH)),
                  _const_spec((1, LANES)), _const_spec((1, LANES))],
        out_specs=[row(XBC_WIDTH), row(LANES), row(LANES), row(LANES)],
        out_shape=[jax.ShapeDtypeStruct((t_rows, XBC_WIDTH), BF16)]
        + [jax.ShapeDtypeStruct((t_rows, LANES), F32)] * 3,
        scratch_shapes=[pltpu.VMEM((tm + 2 * SSM_HALO, D_MODEL), BF16),
                        pltpu.VMEM((SSM_SLOTS, tm + 2 * SSM_HALO, SSM_COLS), F32)],
        compiler_params=_params("parallel"),
        name="ssm_projection",
    )(x2d, x2d, x2d, norm_w.reshape(1, D_MODEL), w_t, wdt, conv_w.T, conv_b.reshape(1, XBC_WIDTH),
      alog, dtb)


ATT_TQ = 128
ATT_TK = ATT_TQ + 2 * BAND_HALF
ATT_UNITS_PER_STEP = 32
ATT_STAT_PAIRS = 4
ATT_QUERIES_PER_STEP = 1024


def _attn_kernel(*refs, length, n_sub, n_pairs, n_planes, first, last):
    interleaved = first
    q_ref, k_ref, v_ref = refs[:3]
    if first:
        o_ref, st_ref = refs[3:]
    elif last:
        op_ref, sp_ref, o_ref = refs[3:]
    else:
        op_ref, sp_ref, o_ref, st_ref, slab_ref = refs[3:]
    tq, tk = ATT_TQ, ATT_TK
    qi = pl.program_id(3)
    lane = lax.broadcasted_iota(jnp.int32, (tq, LANES), 1)
    even = lane < HEAD_DIM
    stat_is_max = (lane & 15) < 8
    row_i = lax.broadcasted_iota(jnp.int32, (tq, tk), 0)
    col_i = lax.broadcasted_iota(jnp.int32, (tq, tk), 1)
    nt = (((1,), (1,)), ((), ()))
    ones = jnp.ones((tk, LANES), BF16)
    sub_rows = tq // CARRY_SPLIT
    key_rows = tk // CARRY_SPLIT

    if interleaved:
        delta = (CARRY_SPLIT * ((col_i & (key_rows - 1)) - (row_i & (sub_rows - 1)))
                 + (col_i // key_rows - row_i // sub_rows))
    else:
        delta = col_i - row_i

    def window_start(sb):
        if interleaved:
            a0 = (qi * n_sub + sb) * sub_rows
            ws = pl.multiple_of(jnp.clip(a0 - BAND_HALF // CARRY_SPLIT, 0, length - key_rows),
                                BF16_ROWS)
            return ws, CARRY_SPLIT * (ws - a0)
        q0 = (qi * n_sub + sb % n_sub) * tq
        ws = pl.multiple_of(jnp.clip(q0 - BAND_HALF, 0, length - tk), BAND_HALF)
        return ws, ws - q0

    def load_q(sb, cols):
        if interleaved:
            return jnp.concatenate([q_ref[0, j, sb * sub_rows:(sb + 1) * sub_rows, cols]
                                    for j in range(CARRY_SPLIT)], axis=0)
        return q_ref[0, sb // n_sub, (sb % n_sub) * tq:(sb % n_sub + 1) * tq, cols]

    def load_keys(ref, sb, ws, cols):
        if interleaved:
            return jnp.concatenate([ref[0, j, pl.ds(ws, key_rows), cols]
                                    for j in range(CARRY_SPLIT)], axis=0)
        return ref[0, sb // n_sub, pl.ds(ws, tk), cols]

    scores, row_max = [], []
    for sb in range(n_planes * n_sub):
        ws, offset = window_start(sb)
        bias = jnp.where(jnp.abs(delta + offset) <= BAND_HALF, 0.0, NEG_BIG)
        bias2 = jnp.concatenate([bias, bias], axis=0)
        for hp in range(n_pairs):
            cols = slice(hp * LANES, (hp + 1) * LANES)
            q2 = load_q(sb, cols)
            zero = jnp.zeros_like(q2)
            qq = jnp.concatenate([jnp.where(even, q2, zero), jnp.where(even, zero, q2)], axis=0)
            k2 = load_keys(k_ref, sb, ws, cols)
            s = lax.dot_general(qq, k2, nt, preferred_element_type=F32) + bias2
            scores.append(s)
            row_max.append(jnp.max(s, axis=-1, keepdims=True))

    for sb in range(n_planes * n_sub):
        ws, _ = window_start(sb)
        plane = sb // n_sub
        rows = slice((sb % n_sub) * tq, (sb % n_sub + 1) * tq)
        out_rows = slice(sb * sub_rows, (sb + 1) * sub_rows)
        stats = jnp.zeros((tq, LANES), F32)
        for hp in range(n_pairs):
            u = sb * n_pairs + hp
            cols = slice(hp * LANES, (hp + 1) * LANES)
            v_ext = jnp.concatenate([load_keys(v_ref, sb, ws, cols), ones], axis=1)
            p = jnp.exp2(scores[u] - row_max[u]).astype(BF16)
            pv = jnp.dot(p, v_ext, preferred_element_type=F32)
            acc = jnp.where(even, pv[:tq, :LANES], pv[tq:, :LANES])
            l = jnp.where(even, pv[:tq, LANES:], pv[tq:, LANES:])
            m = jnp.where(even, row_max[u][:tq], row_max[u][tq:])
            if not first:
                sp = sp_ref[0, plane, rows, :]
                be = LANES * (hp // ATT_STAT_PAIRS) + 16 * (hp % ATT_STAT_PAIRS)
                bo = be + HEAD_DIM
                m_prev = jnp.where(even, sp[:, be:be + 1], sp[:, bo:bo + 1])
                l_prev = jnp.where(even, sp[:, be + 8:be + 9], sp[:, bo + 8:bo + 9])
                acc_prev = op_ref[0, plane, rows, cols].astype(F32)
                m_new = jnp.maximum(m_prev, m)
                a_prev = jnp.exp2(m_prev - m_new)
                a_cur = jnp.exp2(m - m_new)
                acc = acc_prev * a_prev + acc * a_cur
                l = l_prev * a_prev + l * a_cur
                m = m_new
            if last:
                o_ref[0, plane, rows, cols] = (acc / l).astype(BF16)
                continue
            in_zone = ((lane & (HEAD_DIM - 1)) >> 4) == hp
            stats = jnp.where(in_zone, jnp.where(stat_is_max, m, l), stats)
            if interleaved:
                for j in range(CARRY_SPLIT):
                    o_ref[0, j, out_rows, cols] = acc[j * sub_rows:(j + 1) * sub_rows].astype(BF16)
            else:
                slab_ref[hp] = acc
                for j in range(CARRY_SPLIT):
                    o_ref[0, j, 0, out_rows, cols] = (
                        slab_ref[hp, pl.ds(j, sub_rows, stride=CARRY_SPLIT), :].astype(BF16))
        if last:
            continue
        if interleaved:
            for j in range(CARRY_SPLIT):
                st_ref[0, j, out_rows, :] = stats[j * sub_rows:(j + 1) * sub_rows]
        else:
            slab_ref[n_pairs] = stats
            for j in range(CARRY_SPLIT):
                st_ref[0, j, 0, out_rows, :] = (
                    slab_ref[n_pairs, pl.ds(j, sub_rows, stride=CARRY_SPLIT), :])


def _attention_pattern(q, k, v, o_prev, st_prev, first, last):
    cs = CARRY_SPLIT
    if first:
        b, _, length, width = q.shape
        dil, qs = 1, min(length, ATT_QUERIES_PER_STEP // cs)
        n_sub = qs * cs // ATT_TQ
    else:
        b, dil, length, width = q.shape
        qs = min(length, ATT_QUERIES_PER_STEP)
        n_sub = qs // ATT_TQ
    n_pairs = min(ATT_UNITS_PER_STEP // n_sub, width // LANES)
    hw = n_pairs * LANES
    n_hg = width // hw
    n_stat = n_pairs // ATT_STAT_PAIRS
    assert last or n_stat == 1
    n_planes = min(max(ATT_UNITS_PER_STEP // (n_sub * n_pairs), 1), dil) if last else 1
    scratch = []
    if first:
        blk = pl.BlockSpec((1, cs, qs, hw), lambda bi, r, g, qi: (bi, 0, qi, g))
        seq = pl.BlockSpec((1, cs, length, hw), lambda bi, r, g, qi: (bi, 0, 0, g))
        stat = pl.BlockSpec((1, cs, qs, LANES), lambda bi, r, g, qi: (bi, 0, qi, g))
    else:
        blk = pl.BlockSpec((1, n_planes, qs, hw), lambda bi, r, g, qi: (bi, r, qi, g))
        seq = pl.BlockSpec((1, n_planes, length, hw), lambda bi, r, g, qi: (bi, r, 0, g))
        stat = pl.BlockSpec((1, n_planes, qs, n_stat * LANES), lambda bi, r, g, qi: (bi, r, qi, g))
    in_specs, args = [blk, seq, seq], [q, k, v]
    if not first:
        in_specs += [blk, stat]
        args += [o_prev, st_prev]
    if first:
        out_specs = [blk, stat]
        out_shape = [jax.ShapeDtypeStruct((b, cs, length, width), BF16),
                     jax.ShapeDtypeStruct((b, cs, length, n_hg * LANES), F32)]
    elif last:
        out_specs = [blk]
        out_shape = [jax.ShapeDtypeStruct((b, dil, length, width), BF16)]
    else:
        carry = lambda w: pl.BlockSpec((1, cs, 1, qs // cs, w), lambda bi, r, g, qi: (bi, 0, r, qi, g))
        out_specs = [carry(hw), carry(LANES)]
        out_shape = [jax.ShapeDtypeStruct((b, cs, dil, length // cs, width), BF16),
                     jax.ShapeDtypeStruct((b, cs, dil, length // cs, n_hg * LANES), F32)]
        scratch.append(pltpu.VMEM((n_pairs + 1, ATT_TQ, LANES), F32))
    outs = pl.pallas_call(
        functools.partial(_attn_kernel, length=length, n_sub=n_sub, n_pairs=n_pairs,
                          n_planes=n_planes, first=first, last=last),
        grid=(b, dil // n_planes, n_hg, length // qs),
        in_specs=in_specs, out_specs=out_specs, out_shape=out_shape,
        scratch_shapes=scratch,
        compiler_params=_params("parallel", "parallel", "parallel", "arbitrary"),
        name=f"attention_dil{dil}",
    )(*args)
    if last:
        return outs[0], None
    if first:
        return outs[0], outs[1]
    nxt = lambda t: t.reshape(b, cs * dil, length // cs, t.shape[-1])
    return nxt(outs[0]), nxt(outs[1])


def _dilated_attention(qkv_planes):
    o = st = None
    for i, (q, k, v) in enumerate(qkv_planes):
        o, st = _attention_pattern(q, k, v, o, st, i == 0, i == len(qkv_planes) - 1)
    return o


def _split_dot(v, mat, passes):
    out = None
    r = v
    for i in range(passes):
        piece = r.astype(BF16)
        term = jnp.dot(piece, mat, preferred_element_type=F32)
        out = term if out is None else out + term
        if i + 1 < passes:
            r = r - piece.astype(F32)
    return out


def _split_dot_lhs_const(mat, v):
    out = None
    r = v
    for i in range(3):
        piece = r.astype(BF16)
        term = jnp.dot(mat, piece, preferred_element_type=F32)
        out = term if out is None else out + term
        if i < 2:
            r = r - piece.astype(F32)
    return out


def _ssm_kernel(xa_ref, dt_ref, ldt_ref, cs_ref, z_ref, alog_ref, dsk_ref, nw_ref, ef_ref, eb_ref,
                y_ref, hb_ref, bt_c, hrun_ref, *, n_steps):
    L = CHUNK
    cps = SSM_CHUNKS_PER_STEP
    step = pl.program_id(1)
    is_bwd = step < n_steps
    blk = jnp.where(is_bwd, n_steps - 1 - step, step - n_steps)

    @pl.when((step == 0) | (step == n_steps))
    def _():
        hrun_ref[...] = jnp.zeros_like(hrun_ref)

    n_bc = N_SSM_GROUPS * D_STATE
    gw = D_INNER // N_SSM_GROUPS
    li = lax.broadcasted_iota(jnp.int32, (L, L), 0)
    si = lax.broadcasted_iota(jnp.int32, (L, L), 1)
    a_log2 = -jnp.exp(alog_ref[...]) * LOG2E

    def scalars(rows):
        dt, ldt, cs = dt_ref[0, rows, :], ldt_ref[0, rows, :], cs_ref[0, rows, :]
        return dt, ldt, cs, cs - dt * a_log2, cs[L - 1:L, :]

    def state_update(xs, bts, weights, tot, e_ref):
        xw = (_split_dot(weights, e_ref[...], 1) * xs).astype(BF16)
        decay = _split_dot(jnp.broadcast_to(jnp.exp2(tot), (8, LANES)), e_ref[...], 3)[0:1]
        for g in range(N_SSM_GROUPS):
            cols = slice(g * gw, (g + 1) * gw)
            s_g = jnp.dot(bts[g], xw[:, cols], preferred_element_type=F32)
            hrun_ref[:, cols] = hrun_ref[:, cols] * decay[:, cols] + s_g

    def bwd_chunk(i, carry):
        ci = cps - 1 - i
        c = blk * cps + ci
        rows = pl.ds(pl.multiple_of(ci * L, L), L)
        dt, ldt, cs, ecs, tot = scalars(rows)
        hb_ref[c] = hrun_ref[...].astype(BF16)
        xs = xa_ref[0, rows, 0:D_INNER].astype(F32)
        bm = xa_ref[0, rows, D_INNER:D_INNER + n_bc].astype(F32)
        bts = [bm[:, g * D_STATE:(g + 1) * D_STATE].T.astype(BF16) for g in range(N_SSM_GROUPS)]
        bt_c[c] = jnp.concatenate(bts, axis=1)
        state_update(xs, bts, jnp.exp2(ecs + ldt), tot, eb_ref)
        return carry

    def fwd_chunk(ci, carry):
        c = blk * cps + ci
        rows = pl.ds(pl.multiple_of(ci * L, L), L)
        dt, ldt, cs, ecs, tot = scalars(rows)
        xs_b = xa_ref[0, rows, 0:D_INNER]
        xs = xs_b.astype(F32)
        bts = [bt_c[c, :, g * D_STATE:(g + 1) * D_STATE] for g in range(N_SSM_GROUPS)]
        hf_in = hrun_ref[...].astype(BF16)
        hb_in = hb_ref[c]
        scale_f = _split_dot(jnp.exp2(cs), ef_ref[...], 1)
        scale_b = _split_dot(jnp.exp2(tot - ecs), eb_ref[...], 1)
        row_f = (cs - ldt).T
        row_b = (ecs + ldt).T
        dt_t = dt.T
        lane = lax.broadcasted_iota(jnp.int32, (L, LANES), 1)
        even = lane < HEAD_DIM
        below = si < li
        diag = si == li
        hpg = N_SSM_HEADS // N_SSM_GROUPS
        y_groups = []
        for g in range(N_SSM_GROUPS):
            cols = slice(g * gw, (g + 1) * gw)
            cg = xa_ref[0, rows, D_INNER + n_bc + g * D_STATE:D_INNER + n_bc + (g + 1) * D_STATE]
            bg = xa_ref[0, rows, D_INNER + g * D_STATE:D_INNER + (g + 1) * D_STATE]
            gmat = lax.dot_general(cg, bg, (((1,), (1,)), ((), ())), preferred_element_type=F32)
            y_off = (scale_f[:, cols] * jnp.dot(cg, hf_in[:, cols], preferred_element_type=F32)
                     + scale_b[:, cols] * jnp.dot(cg, hb_in[:, cols], preferred_element_type=F32))
            pairs = []
            for pr in range(hpg // 2):
                h0 = g * hpg + 2 * pr
                xs_pair = xs_b[:, h0 * HEAD_DIM:(h0 + 2) * HEAD_DIM]
                res = []
                for h in (h0, h0 + 1):
                    hb = N_SSM_HEADS + h
                    sel = jnp.where(below, cs[:, h:h + 1] - row_f[h:h + 1, :],
                                    row_b[hb:hb + 1, :] - ecs[:, hb:hb + 1])
                    e = jnp.exp2(sel) + jnp.where(diag, dt_t[h:h + 1, :], 0.0)
                    w = (gmat * e).astype(BF16)
                    res.append(jnp.dot(w, xs_pair, preferred_element_type=F32))
                pairs.append(jnp.where(even, res[0], res[1]))
            y_groups.append(jnp.concatenate(pairs, axis=1) + y_off)
        y = jnp.concatenate(y_groups, axis=1) + dsk_ref[...] * xs
        zf = z_ref[0, rows, :].astype(F32)
        gated = y * _silu(zf)
        outs = []
        for g in range(N_SSM_GROUPS):
            gg = gated[:, g * gw:(g + 1) * gw]
            outs.append(gg * lax.rsqrt(jnp.mean(gg * gg, axis=-1, keepdims=True) + EPS))
        y_ref[0, rows, :] = (jnp.concatenate(outs, axis=1) * nw_ref[...]).astype(BF16)
        state_update(xs, bts, jnp.exp2(tot - cs + ldt), tot, ef_ref)
        return carry

    @pl.when(is_bwd)
    def _():
        lax.fori_loop(0, cps, bwd_chunk, 0, unroll=SSM_BWD_UNROLL)

    @pl.when(jnp.logical_not(is_bwd))
    def _():
        lax.fori_loop(0, cps, fwd_chunk, 0, unroll=SSM_FWD_UNROLL)


SSM_CHUNKS_PER_STEP = 16
SSM_FWD_UNROLL = 4
SSM_BWD_UNROLL = 8


def _head_expanders():
    col_head = np.arange(D_INNER) // HEAD_DIM
    rows = np.arange(LANES)[:, None]
    ef = (rows == col_head[None, :]).astype(np.float32)
    eb = (rows == col_head[None, :] + N_SSM_HEADS).astype(np.float32)
    return jnp.asarray(ef, BF16), jnp.asarray(eb, BF16)


def _ssm_mixer(z, xa, dt, ldt, cs, a_log_f, a_log_b, d_skip, norm_w):
    b, s, _ = z.shape
    n = s // CHUNK
    pad = LANES - 2 * N_SSM_HEADS
    alog = jnp.pad(jnp.concatenate([a_log_f, a_log_b]), (0, pad)).reshape(1, LANES)
    dsk = jnp.repeat(d_skip, HEAD_DIM).reshape(1, D_INNER)
    ef, eb = _head_expanders()

    rows = SSM_CHUNKS_PER_STEP * CHUNK
    n_steps = s // rows

    def block_of(st):
        return jnp.where(st < n_steps, n_steps - 1 - st, st - n_steps)

    per_chunk = pl.BlockSpec((1, rows, LANES), lambda bi, st: (bi, block_of(st), 0))
    fwd_only = pl.BlockSpec((1, rows, D_INNER), lambda bi, st: (bi, jnp.maximum(st - n_steps, 0), 0))
    return pl.pallas_call(
        functools.partial(_ssm_kernel, n_steps=n_steps),
        grid=(b, 2 * n_steps),
        in_specs=[
            pl.BlockSpec((1, rows, XBC_WIDTH), lambda bi, st: (bi, block_of(st), 0)),
            per_chunk, per_chunk, per_chunk, fwd_only,
            _const_spec((1, LANES)), _const_spec((1, D_INNER)), _const_spec((1, D_INNER)),
            _const_spec((LANES, D_INNER)), _const_spec((LANES, D_INNER)),
        ],
        out_specs=fwd_only,
        out_shape=jax.ShapeDtypeStruct((b, s, D_INNER), BF16),
        scratch_shapes=[pltpu.VMEM((n, D_STATE, D_INNER), BF16),
                        pltpu.VMEM((n, D_STATE, N_SSM_GROUPS * CHUNK), BF16),
                        pltpu.VMEM((D_STATE, D_INNER), F32)],
        compiler_params=_params("arbitrary", "arbitrary"),
        name="ssm_mixer",
    )(xa, dt, ldt, cs, z, alog, dsk, norm_w.reshape(1, D_INNER), ef, eb)


def _outproj_kernel(x_ref, a_ref, s_ref, wa_ref, ws_ref, nw_ref, x1_ref, h_ref, slab_ref):
    tm = x_ref.shape[0]
    dil = a_ref.shape[1]
    n_cb = ATTN_WIDTH // LANES
    for r in range(dil):
        blk = a_ref[0, r].astype(F32)
        for cb in range(n_cb):
            slab_ref[cb, pl.ds(r, tm // dil, stride=dil), :] = blk[:, cb * LANES:(cb + 1) * LANES]
    attn = jnp.concatenate([slab_ref[cb] for cb in range(n_cb)], axis=1).astype(BF16)
    x1 = (x_ref[...] + jnp.dot(attn, wa_ref[...], preferred_element_type=F32)
          + jnp.dot(s_ref[...], ws_ref[...], preferred_element_type=F32))
    x1_ref[...] = x1
    h_ref[...] = _rms(x1, nw_ref[...]).astype(BF16)


def _out_projection(x2d, attn_planes, ssm, w_out, norm_w, seq, tm=1024):
    t_rows = x2d.shape[0]
    dil = attn_planes.shape[1]
    nseq = seq // tm
    row = pl.BlockSpec((tm, D_MODEL), lambda i: (i, 0))
    planes = pl.BlockSpec((1, dil, tm // dil, ATTN_WIDTH), lambda i: (i // nseq, 0, i % nseq, 0))
    wa = w_out[:ATTN_WIDTH].astype(BF16)
    ws = w_out[ATTN_WIDTH:].astype(BF16)
    return pl.pallas_call(
        _outproj_kernel,
        grid=(t_rows // tm,),
        in_specs=[row, planes, row, _const_spec((ATTN_WIDTH, D_MODEL)), _const_spec((D_INNER, D_MODEL)),
                  _const_spec((1, D_MODEL))],
        out_specs=[row, row],
        out_shape=[jax.ShapeDtypeStruct((t_rows, D_MODEL), F32),
                   jax.ShapeDtypeStruct((t_rows, D_MODEL), BF16)],
        scratch_shapes=[pltpu.VMEM((ATTN_WIDTH // LANES, tm, LANES), F32)],
        compiler_params=_params("parallel"),
        name="out_projection",
    )(x2d, attn_planes, ssm, wa, ws, norm_w.reshape(1, D_MODEL))


FFN_COLS = 256
FFN_HALO = BF16_ROWS
FFN_SLOTS = 2


def _ffn_up_kernel(hc_ref, hp_ref, hn_ref, w_ref, cw_ref, cb_ref, act_ref, lhs_scr, u_scr, *,
                   blocks_per_seq):
    tm = hc_ref.shape[0]
    pos = pl.program_id(0) % blocks_per_seq
    rows = tm + 2 * FFN_HALO
    lhs_scr[0:FFN_HALO, :] = jnp.where(pos > 0, hp_ref[...], jnp.zeros_like(hp_ref))
    lhs_scr[FFN_HALO:FFN_HALO + tm, :] = hc_ref[...]
    lhs_scr[FFN_HALO + tm:rows, :] = jnp.where(pos < blocks_per_seq - 1, hn_ref[...],
                                               jnp.zeros_like(hn_ref))

    n_chunks = D_FF // FFN_COLS

    def chunk_cols(j, half):
        return pl.ds(pl.multiple_of(j * FFN_COLS + half * D_FF, LANES), FFN_COLS)

    def project(j, slot):
        for half in range(2):
            u_scr[2 * slot + half] = jnp.dot(lhs_scr[...], w_ref[:, chunk_cols(j, half)],
                                             preferred_element_type=F32)

    def conv(j, slot, half):
        return _conv3_rows(u_scr[2 * slot + half], cw_ref[:, chunk_cols(j, half)],
                           cb_ref[:, chunk_cols(j, half)], FFN_HALO, tm)

    def finish(j, slot):
        act_ref[:, chunk_cols(j, 0)] = (_silu(conv(j, slot, 0)) * conv(j, slot, 1)).astype(BF16)

    _software_pipeline(n_chunks, project, finish, depth=1, slots=FFN_SLOTS, rolled=True)


def _ffn_up(h, w_up, conv_w, conv_b, seq, tm=1024):
    t_rows = h.shape[0]
    hpb = tm // FFN_HALO
    last_halo = t_rows // FFN_HALO - 1
    width = 2 * D_FF
    return pl.pallas_call(
        functools.partial(_ffn_up_kernel, blocks_per_seq=seq // tm),
        grid=(t_rows // tm,),
        in_specs=[
            pl.BlockSpec((tm, D_MODEL), lambda i: (i, 0)),
            pl.BlockSpec((FFN_HALO, D_MODEL), lambda i: (jnp.maximum(i * hpb - 1, 0), 0)),
            pl.BlockSpec((FFN_HALO, D_MODEL), lambda i: (jnp.minimum((i + 1) * hpb, last_halo), 0)),
            _const_spec((D_MODEL, width)), _const_spec((3, width)), _const_spec((1, width)),
        ],
        out_specs=pl.BlockSpec((tm, D_FF), lambda i: (i, 0)),
        out_shape=jax.ShapeDtypeStruct((t_rows, D_FF), BF16),
        scratch_shapes=[pltpu.VMEM((tm + 2 * FFN_HALO, D_MODEL), BF16),
                        pltpu.VMEM((2 * FFN_SLOTS, tm + 2 * FFN_HALO, FFN_COLS), F32)],
        compiler_params=_params("parallel"),
        name="ffn_up",
    )(h, h, h, w_up.astype(BF16), conv_w.T, conv_b.reshape(1, width))


def _ffn_down_kernel(a_ref, x1_ref, wd_ref, nw_ref, o_ref):
    acc = x1_ref[...] + jnp.dot(a_ref[...], wd_ref[...], preferred_element_type=F32)
    o_ref[...] = _rms(acc, nw_ref[...])


def _ffn_down(act, x1, w_down, norm_w, tm=1024):
    t_rows = act.shape[0]
    return pl.pallas_call(
        _ffn_down_kernel,
        grid=(t_rows // tm,),
        in_specs=[pl.BlockSpec((tm, D_FF), lambda i: (i, 0)), pl.BlockSpec((tm, D_MODEL), lambda i: (i, 0)),
                  _const_spec((D_FF, D_MODEL)), _const_spec((1, D_MODEL))],
        out_specs=pl.BlockSpec((tm, D_MODEL), lambda i: (i, 0)),
        out_shape=jax.ShapeDtypeStruct((t_rows, D_MODEL), F32),
        compiler_params=_params("parallel"),
        name="ffn_down",
    )(act, x1, w_down.astype(BF16), norm_w.reshape(1, D_MODEL))


def kernel(x, norm1_w, w_in, ssm_conv_w, ssm_conv_b, a_log_f, a_log_b, dt_bias_f, dt_bias_b, d_skip,
           ssm_norm_w, w_out, norm2_w, w_up, ffn_conv_w, ffn_conv_b, w_down, final_norm_w):
    b, s, d = x.shape
    depth = w_in.shape[0]
    x2d = x.reshape(b * s, d)
    for layer in range(depth):
        w_t = w_in[layer].T.astype(BF16)
        (q4, k4, v4, q16, k16, v16, z) = _in_projection(x2d, norm1_w[layer], w_t, b, s)
        xa, dt, ldt, cs = _ssm_projection(
            x2d, norm1_w[layer], w_t, ssm_conv_w[layer], ssm_conv_b[layer],
            a_log_f[layer], a_log_b[layer], dt_bias_f[layer], dt_bias_b[layer], s)
        attn = _dilated_attention([(q4, k4, v4), (q4, k4, v4), (q16, k16, v16)])
        sh = lambda t: t.reshape(b, s, t.shape[-1])
        ssm = _ssm_mixer(sh(z), sh(xa), sh(dt), sh(ldt), sh(cs), a_log_f[layer], a_log_b[layer],
                         d_skip[layer], ssm_norm_w[layer])
        x1, h2 = _out_projection(x2d, attn, ssm.reshape(b * s, -1), w_out[layer], norm2_w[layer], s)
        act = _ffn_up(h2, w_up[layer], ffn_conv_w[layer], ffn_conv_b[layer], s)
        assert depth == 1
        x2d = _ffn_down(act, x1, w_down[layer], final_norm_w)
    return x2d.reshape(b, s, d)
```

```python
import functools

import numpy as np
import jax
import jax.numpy as jnp
from jax import lax
from jax.experimental import pallas as pl
from jax.experimental.pallas import tpu as pltpu

F32 = jnp.float32
BF16 = jnp.bfloat16

D_MODEL = 1024
HEAD_DIM = 64
N_ATTN_HEADS = 16
ATTN_WIDTH = N_ATTN_HEADS * HEAD_DIM
ROPE_DIM = HEAD_DIM // 4
ROPE_THETA = 500000.0
DILATIONS = (1, 4, 16)
BAND_HALF = 64
CARRY_SPLIT = 4

D_INNER = 1024
N_SSM_HEADS = 16
N_SSM_GROUPS = 4
D_STATE = 128
CHUNK = 128
XBC_WIDTH = D_INNER + 2 * N_SSM_GROUPS * D_STATE
D_FF = 2816
EPS = 1e-6

LANES = 128
BF16_ROWS = 16
VMEM_LIMIT = 56 * 1024 * 1024
NEG_BIG = -1e30
_NT = (((1,), (1,)), ((), ()))
LOG2E = 1.4426950408889634


def _params(*sem):
    return pltpu.CompilerParams(dimension_semantics=sem, vmem_limit_bytes=VMEM_LIMIT)


def _const_spec(shape):
    return pl.BlockSpec(shape, lambda *_: (0,) * len(shape))


def _resident_f32_weight(shape, block_index):
    return pl.BlockSpec(shape, lambda *_: block_index, pipeline_mode=pl.Buffered(1))


def _rms(x, w):
    return x * lax.rsqrt(jnp.mean(x * x, axis=-1, keepdims=True) + EPS) * w


def _silu(y):
    h = 0.5 * y
    return h + h * jnp.tanh(h)


def _conv3_rows(u, w, b, halo, rows):
    r = u.shape[0]
    y = pltpu.roll(u, 1, 0) * w[0:1] + u * w[1:2] + pltpu.roll(u, r - 1, 0) * w[2:3] + b
    return y[halo:halo + rows]


def _software_pipeline(n_chunks, project, finish, depth, slots, rolled):
    for k in range(min(depth, n_chunks)):
        project(k, k % slots)
    n_rolled = max(n_chunks - depth, 0) // slots if rolled else 0

    def body(i, carry):
        k0 = i * slots
        for s in range(slots):
            project(k0 + s + depth, (s + depth) % slots)
            finish(k0 + s, s)
        return carry

    if n_rolled:
        lax.fori_loop(0, n_rolled, body, 0)
    for k in range(n_rolled * slots, n_chunks):
        if k + depth < n_chunks:
            project(k + depth, (k + depth) % slots)
        finish(k, k % slots)


def _inproj_kernel(x_ref, nw_ref, wq_ref, wk_ref, wv_ref, wz_ref, rc_ref, rs1_ref, rs2_ref,
                   q4_ref, k4_ref, v4_ref, q16_ref, k16_ref, v16_ref, z_ref, slab_ref, slab2_ref):
    tm = x_ref.shape[0]
    h = _rms(x_ref[...], nw_ref[...]).astype(BF16)
    rc, rs1, rs2 = rc_ref[...], rs1_ref[...], rs2_ref[...]
    n_cb = ATTN_WIDTH // LANES

    def emit(idx, w_ref, out_refs, rope, scale=None):
        t = lax.dot_general(h, w_ref[...], _NT, preferred_element_type=F32)
        if scale is not None:
            t = t * scale
        for cb in range(n_cb):
            cols = slice(cb * LANES, (cb + 1) * LANES)
            blk = t[:, cols]
            if rope:
                lo = pltpu.roll(blk, ROPE_DIM // 2, 1)
                hi = pltpu.roll(blk, LANES - ROPE_DIM // 2, 1)
                blk = blk * rc + lo * rs1 + hi * rs2
            slab = slab_ref.at[idx * n_cb + cb]
            slab2 = slab2_ref.at[idx * n_cb + cb]
            slab[...] = blk
            n4 = tm // CARRY_SPLIT
            n16 = n4 // CARRY_SPLIT
            for r4 in range(CARRY_SPLIT):
                p4 = slab[pl.ds(r4, n4, stride=CARRY_SPLIT), :]
                out_refs[0][0, r4, :, cols] = p4.astype(BF16)
                slab2[r4 * n4:(r4 + 1) * n4, :] = p4
                for j in range(CARRY_SPLIT):
                    out_refs[1][0, CARRY_SPLIT * j + r4, :, cols] = (
                        slab2[pl.ds(r4 * n4 + j, n16, stride=CARRY_SPLIT), :].astype(BF16))

    emit(0, wq_ref, (q4_ref, q16_ref), True, scale=HEAD_DIM ** -0.5 * LOG2E)
    emit(1, wk_ref, (k4_ref, k16_ref), True)
    emit(2, wv_ref, (v4_ref, v16_ref), False)
    z_ref[...] = lax.dot_general(h, wz_ref[...], _NT, preferred_element_type=F32).astype(BF16)


def _rope_tables(seq):
    half = ROPE_DIM // 2
    inv_freq = jnp.power(ROPE_THETA, -jnp.arange(half, dtype=F32) * 2.0 / ROPE_DIM)
    ang = jnp.arange(seq, dtype=F32)[:, None] * inv_freq[None, :]
    cos, sin = jnp.cos(ang), jnp.sin(ang)
    one = jnp.ones((seq, HEAD_DIM - ROPE_DIM), F32)
    zero8 = jnp.zeros((seq, half), F32)
    zero = jnp.zeros((seq, HEAD_DIM - ROPE_DIM), F32)
    rc = jnp.concatenate([cos, cos, one], axis=1)
    rs1 = jnp.concatenate([zero8, sin, zero], axis=1)
    rs2 = jnp.concatenate([-sin, zero8, zero], axis=1)
    rep = LANES // HEAD_DIM
    return tuple(jnp.tile(t, (1, rep)) for t in (rc, rs1, rs2))


def _in_projection(x2d, norm_w, w_t, batch, seq, tm=512):
    t_rows = x2d.shape[0]
    a = ATTN_WIDTH
    assert a == D_INNER
    w_block = lambda k: pl.BlockSpec((a, D_MODEL), lambda i: (k, 0))
    rc, rs1, rs2 = _rope_tables(seq)
    nseq = seq // tm
    row = lambda width: pl.BlockSpec((tm, width), lambda i: (i, 0))
    tab = pl.BlockSpec((tm, LANES), lambda i: (i % nseq, 0))
    plane = lambda dil: pl.BlockSpec((1, dil, tm // dil, a), lambda i: (i // nseq, 0, i % nseq, 0))
    plane_shape = lambda dil: jax.ShapeDtypeStruct((batch, dil, seq // dil, a), BF16)
    d4, d16 = DILATIONS[1:]
    return pl.pallas_call(
        _inproj_kernel,
        grid=(t_rows // tm,),
        in_specs=[row(D_MODEL), _const_spec((1, D_MODEL)),
                  w_block(0), w_block(1), w_block(2), w_block(3), tab, tab, tab],
        out_specs=[plane(d4)] * 3 + [plane(d16)] * 3 + [row(D_INNER)],
        out_shape=[plane_shape(d4)] * 3 + [plane_shape(d16)] * 3
        + [jax.ShapeDtypeStruct((t_rows, D_INNER), BF16)],
        scratch_shapes=[pltpu.VMEM((3 * a // LANES, tm, LANES), F32)] * 2,
        compiler_params=_params("parallel"),
        name="in_projection",
    )(x2d, norm_w.reshape(1, D_MODEL), w_t, w_t, w_t, w_t, rc, rs1, rs2)


SSM_HALO = BF16_ROWS
SSM_COLS = 256
SSM_SLOTS = 4


def _ssm_proj_kernel(xc_ref, xp_ref, xn_ref, nw_ref, wx_ref, wdt_ref, cw_ref, cb_ref, alog_ref, dtb_ref,
                     xa_ref, dt_ref, ldt_ref, cs_ref, lhs_scr, u_scr, *, blocks_per_seq):
    tm = xc_ref.shape[0]
    pos = pl.program_id(0) % blocks_per_seq
    nw = nw_ref[...]
    rows = tm + 2 * SSM_HALO
    h = _rms(xc_ref[...], nw).astype(BF16)
    zero_halo = jnp.zeros((SSM_HALO, D_MODEL), BF16)
    lhs_scr[0:SSM_HALO, :] = jnp.where(pos > 0, _rms(xp_ref[...], nw).astype(BF16), zero_halo)
    lhs_scr[SSM_HALO:SSM_HALO + tm, :] = h
    lhs_scr[SSM_HALO + tm:rows, :] = jnp.where(pos < blocks_per_seq - 1,
                                               _rms(xn_ref[...], nw).astype(BF16), zero_halo)
    n_chunks = XBC_WIDTH // SSM_COLS

    def chunk_cols(j):
        return pl.ds(pl.multiple_of(j * SSM_COLS, LANES), SSM_COLS)

    def project(j, slot):
        u_scr[slot] = lax.dot_general(lhs_scr[...], wx_ref[chunk_cols(j), :], _NT,
                                      preferred_element_type=F32)

    def finish(j, slot):
        y = _conv3_rows(u_scr[slot], cw_ref[:, chunk_cols(j)], cb_ref[:, chunk_cols(j)], SSM_HALO, tm)
        xa_ref[:, chunk_cols(j)] = _silu(y).astype(BF16)

    _software_pipeline(n_chunks, project, finish, depth=2, slots=SSM_SLOTS, rolled=False)

    x_dt = lax.dot_general(h, wdt_ref[...], _NT, preferred_element_type=F32) + dtb_ref[...]
    dt = jnp.maximum(x_dt, 0.0) + jnp.log1p(jnp.exp(-jnp.abs(x_dt)))
    a = dt * (-jnp.exp(alog_ref[...]) * LOG2E)
    tri = (lax.broadcasted_iota(jnp.int32, (CHUNK, CHUNK), 1)
           <= lax.broadcasted_iota(jnp.int32, (CHUNK, CHUNK), 0)).astype(BF16)
    dt_ref[...] = dt
    ldt_ref[...] = jnp.log(dt) * LOG2E
    for ch in range(tm // CHUNK):
        rows_c = slice(ch * CHUNK, (ch + 1) * CHUNK)
        cs_ref[rows_c, :] = _split_dot_lhs_const(tri, a[rows_c])


def _ssm_projection(x2d, norm_w, w_t, conv_w, conv_b, a_log_f, a_log_b, dt_bias_f, dt_bias_b,
                    seq, tm=1024):
    t_rows = x2d.shape[0]
    pad = LANES - 2 * N_SSM_HEADS
    alog = jnp.pad(jnp.concatenate([a_log_f, a_log_b]), (0, pad)).reshape(1, LANES)
    dtb = jnp.pad(jnp.concatenate([dt_bias_f, dt_bias_b]), (0, pad)).reshape(1, LANES)
    o = 3 * ATTN_WIDTH + D_INNER
    assert o % XBC_WIDTH == 0
    wdt = jnp.pad(w_t[o + XBC_WIDTH:], ((0, pad), (0, 0)))
    hpb = tm // SSM_HALO
    last_halo = t_rows // SSM_HALO - 1
    row = lambda width: pl.BlockSpec((tm, width), lambda i: (i, 0))
    return pl.pallas_call(
        functools.partial(_ssm_proj_kernel, blocks_per_seq=seq // tm),
        grid=(t_rows // tm,),
        in_specs=[row(D_MODEL),
                  pl.BlockSpec((SSM_HALO, D_MODEL), lambda i: (jnp.maximum(i * hpb - 1, 0), 0)),
                  pl.BlockSpec((SSM_HALO, D_MODEL), lambda i: (jnp.minimum((i + 1) * hpb, last_halo), 0)),
                  _const_spec((1, D_MODEL)),
                  pl.BlockSpec((XBC_WIDTH, D_MODEL), lambda i: (o // XBC_WIDTH, 0)),
                  _const_spec((LANES, D_MODEL)),
                  _const_spec((3, XBC_WIDTH)), _const_spec((1, XBC_WIDTH)),
                  _const_spec((1, LANES)), _const_spec((1, LANES))],
        out_specs=[row(XBC_WIDTH), row(LANES), row(LANES), row(LANES)],
        out_shape=[jax.ShapeDtypeStruct((t_rows, XBC_WIDTH), BF16)]
        + [jax.ShapeDtypeStruct((t_rows, LANES), F32)] * 3,
        scratch_shapes=[pltpu.VMEM((tm + 2 * SSM_HALO, D_MODEL), BF16),
                        pltpu.VMEM((SSM_SLOTS, tm + 2 * SSM_HALO, SSM_COLS), F32)],
        compiler_params=_params("parallel"),
        name="ssm_projection",
    )(x2d, x2d, x2d, norm_w.reshape(1, D_MODEL), w_t, wdt, conv_w.T, conv_b.reshape(1, XBC_WIDTH),
      alog, dtb)


ATT_TQ = 128
ATT_TK = ATT_TQ + 2 * BAND_HALF
ATT_UNITS_PER_STEP = 32
ATT_STAT_PAIRS = 4
ATT_QUERIES_PER_STEP = 1024


def _attn_kernel(*refs, length, n_sub, n_pairs, n_planes, first, last):
    interleaved = first
    q_ref, k_ref, v_ref = refs[:3]
    if first:
        o_ref, st_ref = refs[3:]
    elif last:
        op_ref, sp_ref, o_ref = refs[3:]
    else:
        op_ref, sp_ref, o_ref, st_ref, slab_ref = refs[3:]
    tq, tk = ATT_TQ, ATT_TK
    qi = pl.program_id(3)
    lane = lax.broadcasted_iota(jnp.int32, (tq, LANES), 1)
    even = lane < HEAD_DIM
    stat_is_max = (lane & 15) < 8
    row_i = lax.broadcasted_iota(jnp.int32, (tq, tk), 0)
    col_i = lax.broadcasted_iota(jnp.int32, (tq, tk), 1)
    nt = (((1,), (1,)), ((), ()))
    ones = jnp.ones((tk, LANES), BF16)
    sub_rows = tq // CARRY_SPLIT
    key_rows = tk // CARRY_SPLIT

    if interleaved:
        delta = (CARRY_SPLIT * ((col_i & (key_rows - 1)) - (row_i & (sub_rows - 1)))
                 + (col_i // key_rows - row_i // sub_rows))
    else:
        delta = col_i - row_i

    def window_start(sb):
        if interleaved:
            a0 = (qi * n_sub + sb) * sub_rows
            ws = pl.multiple_of(jnp.clip(a0 - BAND_HALF // CARRY_SPLIT, 0, length - key_rows),
                                BF16_ROWS)
            return ws, CARRY_SPLIT * (ws - a0)
        q0 = (qi * n_sub + sb % n_sub) * tq
        ws = pl.multiple_of(jnp.clip(q0 - BAND_HALF, 0, length - tk), BAND_HALF)
        return ws, ws - q0

    def load_q(sb, cols):
        if interleaved:
            return jnp.concatenate([q_ref[0, j, sb * sub_rows:(sb + 1) * sub_rows, cols]
                                    for j in range(CARRY_SPLIT)], axis=0)
        return q_ref[0, sb // n_sub, (sb % n_sub) * tq:(sb % n_sub + 1) * tq, cols]

    def load_keys(ref, sb, ws, cols):
        if interleaved:
            return jnp.concatenate([ref[0, j, pl.ds(ws, key_rows), cols]
                                    for j in range(CARRY_SPLIT)], axis=0)
        return ref[0, sb // n_sub, pl.ds(ws, tk), cols]

    scores, row_max = [], []
    for sb in range(n_planes * n_sub):
        ws, offset = window_start(sb)
        bias = jnp.where(jnp.abs(delta + offset) <= BAND_HALF, 0.0, NEG_BIG)
        bias2 = jnp.concatenate([bias, bias], axis=0)
        for hp in range(n_pairs):
            cols = slice(hp * LANES, (hp + 1) * LANES)
            q2 = load_q(sb, cols)
            zero = jnp.zeros_like(q2)
            qq = jnp.concatenate([jnp.where(even, q2, zero), jnp.where(even, zero, q2)], axis=0)
            k2 = load_keys(k_ref, sb, ws, cols)
            s = lax.dot_general(qq, k2, nt, preferred_element_type=F32) + bias2
            scores.append(s)
            row_max.append(jnp.max(s, axis=-1, keepdims=True))

    for sb in range(n_planes * n_sub):
        ws, _ = window_start(sb)
        plane = sb // n_sub
        rows = slice((sb % n_sub) * tq, (sb % n_sub + 1) * tq)
        out_rows = slice(sb * sub_rows, (sb + 1) * sub_rows)
        stats = jnp.zeros((tq, LANES), F32)
        for hp in range(n_pairs):
            u = sb * n_pairs + hp
            cols = slice(hp * LANES, (hp + 1) * LANES)
            v_ext = jnp.concatenate([load_keys(v_ref, sb, ws, cols), ones], axis=1)
            p = jnp.exp2(scores[u] - row_max[u]).astype(BF16)
            pv = jnp.dot(p, v_ext, preferred_element_type=F32)
            acc = jnp.where(even, pv[:tq, :LANES], pv[tq:, :LANES])
            l = jnp.where(even, pv[:tq, LANES:], pv[tq:, LANES:])
            m = jnp.where(even, row_max[u][:tq], row_max[u][tq:])
            if not first:
                sp = sp_ref[0, plane, rows, :]
                be = LANES * (hp // ATT_STAT_PAIRS) + 16 * (hp % ATT_STAT_PAIRS)
                bo = be + HEAD_DIM
                m_prev = jnp.where(even, sp[:, be:be + 1], sp[:, bo:bo + 1])
                l_prev = jnp.where(even, sp[:, be + 8:be + 9], sp[:, bo + 8:bo + 9])
                acc_prev = op_ref[0, plane, rows, cols].astype(F32)
                m_new = jnp.maximum(m_prev, m)
                a_prev = jnp.exp2(m_prev - m_new)
                a_cur = jnp.exp2(m - m_new)
                acc = acc_prev * a_prev + acc * a_cur
                l = l_prev * a_prev + l * a_cur
                m = m_new
            if last:
                o_ref[0, plane, rows, cols] = (acc / l).astype(BF16)
                continue
            in_zone = ((lane & (HEAD_DIM - 1)) >> 4) == hp
            stats = jnp.where(in_zone, jnp.where(stat_is_max, m, l), stats)
            if interleaved:
                for j in range(CARRY_SPLIT):
                    o_ref[0, j, out_rows, cols] = acc[j * sub_rows:(j + 1) * sub_rows].astype(BF16)
            else:
                slab_ref[hp] = acc
                for j in range(CARRY_SPLIT):
                    o_ref[0, j, 0, out_rows, cols] = (
                        slab_ref[hp, pl.ds(j, sub_rows, stride=CARRY_SPLIT), :].astype(BF16))
        if last:
            continue
        if interleaved:
            for j in range(CARRY_SPLIT):
                st_ref[0, j, out_rows, :] = stats[j * sub_rows:(j + 1) * sub_rows]
        else:
            slab_ref[n_pairs] = stats
            for j in range(CARRY_SPLIT):
                st_ref[0, j, 0, out_rows, :] = (
                    slab_ref[n_pairs, pl.ds(j, sub_rows, stride=CARRY_SPLIT), :])


def _attention_pattern(q, k, v, o_prev, st_prev, first, last):
    cs = CARRY_SPLIT
    if first:
        b, _, length, width = q.shape
        dil, qs = 1, min(length, ATT_QUERIES_PER_STEP // cs)
        n_sub = qs * cs // ATT_TQ
    else:
        b, dil, length, width = q.shape
        qs = min(length, ATT_QUERIES_PER_STEP)
        n_sub = qs // ATT_TQ
    n_pairs = min(ATT_UNITS_PER_STEP // n_sub, width // LANES)
    hw = n_pairs * LANES
    n_hg = width // hw
    n_stat = n_pairs // ATT_STAT_PAIRS
    assert last or n_stat == 1
    n_planes = min(max(ATT_UNITS_PER_STEP // (n_sub * n_pairs), 1), dil) if last else 1
    scratch = []
    if first:
        blk = pl.BlockSpec((1, cs, qs, hw), lambda bi, r, g, qi: (bi, 0, qi, g))
        seq = pl.BlockSpec((1, cs, length, hw), lambda bi, r, g, qi: (bi, 0, 0, g))
        stat = pl.BlockSpec((1, cs, qs, LANES), lambda bi, r, g, qi: (bi, 0, qi, g))
    else:
        blk = pl.BlockSpec((1, n_planes, qs, hw), lambda bi, r, g, qi: (bi, r, qi, g))
        seq = pl.BlockSpec((1, n_planes, length, hw), lambda bi, r, g, qi: (bi, r, 0, g))
        stat = pl.BlockSpec((1, n_planes, qs, n_stat * LANES), lambda bi, r, g, qi: (bi, r, qi, g))
    in_specs, args = [blk, seq, seq], [q, k, v]
    if not first:
        in_specs += [blk, stat]
        args += [o_prev, st_prev]
    if first:
        out_specs = [blk, stat]
        out_shape = [jax.ShapeDtypeStruct((b, cs, length, width), BF16),
                     jax.ShapeDtypeStruct((b, cs, length, n_hg * LANES), F32)]
    elif last:
        out_specs = [blk]
        out_shape = [jax.ShapeDtypeStruct((b, dil, length, width), BF16)]
    else:
        carry = lambda w: pl.BlockSpec((1, cs, 1, qs // cs, w), lambda bi, r, g, qi: (bi, 0, r, qi, g))
        out_specs = [carry(hw), carry(LANES)]
        out_shape = [jax.ShapeDtypeStruct((b, cs, dil, length // cs, width), BF16),
                     jax.ShapeDtypeStruct((b, cs, dil, length // cs, n_hg * LANES), F32)]
        scratch.append(pltpu.VMEM((n_pairs + 1, ATT_TQ, LANES), F32))
    outs = pl.pallas_call(
        functools.partial(_attn_kernel, length=length, n_sub=n_sub, n_pairs=n_pairs,
                          n_planes=n_planes, first=first, last=last),
        grid=(b, dil // n_planes, n_hg, length // qs),
        in_specs=in_specs, out_specs=out_specs, out_shape=out_shape,
        scratch_shapes=scratch,
        compiler_params=_params("parallel", "parallel", "parallel", "arbitrary"),
        name=f"attention_dil{dil}",
    )(*args)
    if last:
        return outs[0], None
    if first:
        return outs[0], outs[1]
    nxt = lambda t: t.reshape(b, cs * dil, length // cs, t.shape[-1])
    return nxt(outs[0]), nxt(outs[1])


def _dilated_attention(qkv_planes):
    o = st = None
    for i, (q, k, v) in enumerate(qkv_planes):
        o, st = _attention_pattern(q, k, v, o, st, i == 0, i == len(qkv_planes) - 1)
    return o


def _split_dot(v, mat, passes):
    out = None
    r = v
    for i in range(passes):
        piece = r.astype(BF16)
        term = jnp.dot(piece, mat, preferred_element_type=F32)
        out = term if out is None else out + term
        if i + 1 < passes:
            r = r - piece.astype(F32)
    return out


def _split_dot_lhs_const(mat, v):
    out = None
    r = v
    for i in range(3):
        piece = r.astype(BF16)
        term = jnp.dot(mat, piece, preferred_element_type=F32)
        out = term if out is None else out + term
        if i < 2:
            r = r - piece.astype(F32)
    return out


def _ssm_kernel(xa_ref, dt_ref, ldt_ref, cs_ref, z_ref, alog_ref, dsk_ref, nw_ref, ef_ref, eb_ref,
                y_ref, hb_ref, bt_c, hrun_ref, *, n_steps):
    L = CHUNK
    cps = SSM_CHUNKS_PER_STEP
    step = pl.program_id(1)
    is_bwd = step < n_steps
    blk = jnp.where(is_bwd, n_steps - 1 - step, step - n_steps)

    @pl.when((step == 0) | (step == n_steps))
    def _():
        hrun_ref[...] = jnp.zeros_like(hrun_ref)

    n_bc = N_SSM_GROUPS * D_STATE
    gw = D_INNER // N_SSM_GROUPS
    li = lax.broadcasted_iota(jnp.int32, (L, L), 0)
    si = lax.broadcasted_iota(jnp.int32, (L, L), 1)
    a_log2 = -jnp.exp(alog_ref[...]) * LOG2E

    def scalars(rows):
        dt, ldt, cs = dt_ref[0, rows, :], ldt_ref[0, rows, :], cs_ref[0, rows, :]
        return dt, ldt, cs, cs - dt * a_log2, cs[L - 1:L, :]

    def state_update(xs, bts, weights, tot, e_ref):
        xw = (_split_dot(weights, e_ref[...], 1) * xs).astype(BF16)
        decay = _split_dot(jnp.broadcast_to(jnp.exp2(tot), (8, LANES)), e_ref[...], 3)[0:1]
        for g in range(N_SSM_GROUPS):
            cols = slice(g * gw, (g + 1) * gw)
            s_g = jnp.dot(bts[g], xw[:, cols], preferred_element_type=F32)
            hrun_ref[:, cols] = hrun_ref[:, cols] * decay[:, cols] + s_g

    def bwd_chunk(i, carry):
        ci = cps - 1 - i
        c = blk * cps + ci
        rows = pl.ds(pl.multiple_of(ci * L, L), L)
        dt, ldt, cs, ecs, tot = scalars(rows)
        hb_ref[c] = hrun_ref[...].astype(BF16)
        xs = xa_ref[0, rows, 0:D_INNER].astype(F32)
        bm = xa_ref[0, rows, D_INNER:D_INNER + n_bc].astype(F32)
        bts = [bm[:, g * D_STATE:(g + 1) * D_STATE].T.astype(BF16) for g in range(N_SSM_GROUPS)]
        bt_c[c] = jnp.concatenate(bts, axis=1)
        state_update(xs, bts, jnp.exp2(ecs + ldt), tot, eb_ref)
        return carry

    def fwd_chunk(ci, carry):
        c = blk * cps + ci
        rows = pl.ds(pl.multiple_of(ci * L, L), L)
        dt, ldt, cs, ecs, tot = scalars(rows)
        xs_b = xa_ref[0, rows, 0:D_INNER]
        xs = xs_b.astype(F32)
        bts = [bt_c[c, :, g * D_STATE:(g + 1) * D_STATE] for g in range(N_SSM_GROUPS)]
        hf_in = hrun_ref[...].astype(BF16)
        hb_in = hb_ref[c]
        scale_f = _split_dot(jnp.exp2(cs), ef_ref[...], 1)
        scale_b = _split_dot(jnp.exp2(tot - ecs), eb_ref[...], 1)
        row_f = (cs - ldt).T
        row_b = (ecs + ldt).T
        dt_t = dt.T
        lane = lax.broadcasted_iota(jnp.int32, (L, LANES), 1)
        even = lane < HEAD_DIM
        below = si < li
        diag = si == li
        hpg = N_SSM_HEADS // N_SSM_GROUPS
        y_groups = []
        for g in range(N_SSM_GROUPS):
            cols = slice(g * gw, (g + 1) * gw)
            cg = xa_ref[0, rows, D_INNER + n_bc + g * D_STATE:D_INNER + n_bc + (g + 1) * D_STATE]
            bg = xa_ref[0, rows, D_INNER + g * D_STATE:D_INNER + (g + 1) * D_STATE]
            gmat = lax.dot_general(cg, bg, (((1,), (1,)), ((), ())), preferred_element_type=F32)
            y_off = (scale_f[:, cols] * jnp.dot(cg, hf_in[:, cols], preferred_element_type=F32)
                     + scale_b[:, cols] * jnp.dot(cg, hb_in[:, cols], preferred_element_type=F32))
            pairs = []
            for pr in range(hpg // 2):
                h0 = g * hpg + 2 * pr
                xs_pair = xs_b[:, h0 * HEAD_DIM:(h0 + 2) * HEAD_DIM]
                res = []
                for h in (h0, h0 + 1):
                    hb = N_SSM_HEADS + h
                    sel = jnp.where(below, cs[:, h:h + 1] - row_f[h:h + 1, :],
                                    row_b[hb:hb + 1, :] - ecs[:, hb:hb + 1])
                    e = jnp.exp2(sel) + jnp.where(diag, dt_t[h:h + 1, :], 0.0)
                    w = (gmat * e).astype(BF16)
                    res.append(jnp.dot(w, xs_pair, preferred_element_type=F32))
                pairs.append(jnp.where(even, res[0], res[1]))
            y_groups.append(jnp.concatenate(pairs, axis=1) + y_off)
        y = jnp.concatenate(y_groups, axis=1) + dsk_ref[...] * xs
        zf = z_ref[0, rows, :].astype(F32)
        gated = y * _silu(zf)
        outs = []
        for g in range(N_SSM_GROUPS):
            gg = gated[:, g * gw:(g + 1) * gw]
            outs.append(gg * lax.rsqrt(jnp.mean(gg * gg, axis=-1, keepdims=True) + EPS))
        y_ref[0, rows, :] = (jnp.concatenate(outs, axis=1) * nw_ref[...]).astype(BF16)
        state_update(xs, bts, jnp.exp2(tot - cs + ldt), tot, ef_ref)
        return carry

    @pl.when(is_bwd)
    def _():
        lax.fori_loop(0, cps, bwd_chunk, 0, unroll=SSM_BWD_UNROLL)

    @pl.when(jnp.logical_not(is_bwd))
    def _():
        lax.fori_loop(0, cps, fwd_chunk, 0, unroll=SSM_FWD_UNROLL)


SSM_CHUNKS_PER_STEP = 16
SSM_FWD_UNROLL = 4
SSM_BWD_UNROLL = 8


def _head_expanders():
    col_head = np.arange(D_INNER) // HEAD_DIM
    rows = np.arange(LANES)[:, None]
    ef = (rows == col_head[None, :]).astype(np.float32)
    eb = (rows == col_head[None, :] + N_SSM_HEADS).astype(np.float32)
    return jnp.asarray(ef, BF16), jnp.asarray(eb, BF16)


def _ssm_mixer(z, xa, dt, ldt, cs, a_log_f, a_log_b, d_skip, norm_w):
    b, s, _ = z.shape
    n = s // CHUNK
    pad = LANES - 2 * N_SSM_HEADS
    alog = jnp.pad(jnp.concatenate([a_log_f, a_log_b]), (0, pad)).reshape(1, LANES)
    dsk = jnp.repeat(d_skip, HEAD_DIM).reshape(1, D_INNER)
    ef, eb = _head_expanders()

    rows = SSM_CHUNKS_PER_STEP * CHUNK
    n_steps = s // rows

    def block_of(st):
        return jnp.where(st < n_steps, n_steps - 1 - st, st - n_steps)

    per_chunk = pl.BlockSpec((1, rows, LANES), lambda bi, st: (bi, block_of(st), 0))
    fwd_only = pl.BlockSpec((1, rows, D_INNER), lambda bi, st: (bi, jnp.maximum(st - n_steps, 0), 0))
    return pl.pallas_call(
        functools.partial(_ssm_kernel, n_steps=n_steps),
        grid=(b, 2 * n_steps),
        in_specs=[
            pl.BlockSpec((1, rows, XBC_WIDTH), lambda bi, st: (bi, block_of(st), 0)),
            per_chunk, per_chunk, per_chunk, fwd_only,
            _const_spec((1, LANES)), _const_spec((1, D_INNER)), _const_spec((1, D_INNER)),
            _const_spec((LANES, D_INNER)), _const_spec((LANES, D_INNER)),
        ],
        out_specs=fwd_only,
        out_shape=jax.ShapeDtypeStruct((b, s, D_INNER), BF16),
        scratch_shapes=[pltpu.VMEM((n, D_STATE, D_INNER), BF16),
                        pltpu.VMEM((n, D_STATE, N_SSM_GROUPS * CHUNK), BF16),
                        pltpu.VMEM((D_STATE, D_INNER), F32)],
        compiler_params=_params("arbitrary", "arbitrary"),
        name="ssm_mixer",
    )(xa, dt, ldt, cs, z, alog, dsk, norm_w.reshape(1, D_INNER), ef, eb)


def _outproj_kernel(x_ref, a_ref, s_ref, wa_ref, ws_ref, nw_ref, x1_ref, h_ref, slab_ref):
    tm = x_ref.shape[0]
    dil = a_ref.shape[1]
    n_cb = ATTN_WIDTH // LANES
    for r in range(dil):
        blk = a_ref[0, r].astype(F32)
        for cb in range(n_cb):
            slab_ref[cb, pl.ds(r, tm // dil, stride=dil), :] = blk[:, cb * LANES:(cb + 1) * LANES]
    attn = jnp.concatenate([slab_ref[cb] for cb in range(n_cb)], axis=1).astype(BF16)
    x1 = (x_ref[...] + jnp.dot(attn, wa_ref[...].astype(BF16), preferred_element_type=F32)
          + jnp.dot(s_ref[...], ws_ref[...].astype(BF16), preferred_element_type=F32))
    x1_ref[...] = x1
    h_ref[...] = _rms(x1, nw_ref[...]).astype(BF16)


def _out_projection(x2d, attn_planes, ssm, w_out, norm_w, seq, tm=1024):
    t_rows = x2d.shape[0]
    dil = attn_planes.shape[1]
    nseq = seq // tm
    row = pl.BlockSpec((tm, D_MODEL), lambda i: (i, 0))
    planes = pl.BlockSpec((1, dil, tm // dil, ATTN_WIDTH), lambda i: (i // nseq, 0, i % nseq, 0))
    return pl.pallas_call(
        _outproj_kernel,
        grid=(t_rows // tm,),
        in_specs=[row, planes, row, _resident_f32_weight((ATTN_WIDTH, D_MODEL), (0, 0)),
                  _resident_f32_weight((D_INNER, D_MODEL), (ATTN_WIDTH // D_INNER, 0)),
                  _const_spec((1, D_MODEL))],
        out_specs=[row, row],
        out_shape=[jax.ShapeDtypeStruct((t_rows, D_MODEL), F32),
                   jax.ShapeDtypeStruct((t_rows, D_MODEL), BF16)],
        scratch_shapes=[pltpu.VMEM((ATTN_WIDTH // LANES, tm, LANES), F32)],
        compiler_params=_params("parallel"),
        name="out_projection",
    )(x2d, attn_planes, ssm, w_out, w_out, norm_w.reshape(1, D_MODEL))


FFN_COLS = 256
FFN_HALO = BF16_ROWS
FFN_SLOTS = 2


def _ffn_up_kernel(hc_ref, hp_ref, hn_ref, w_ref, cw_ref, cb_ref, act_ref, lhs_scr, u_scr, *,
                   blocks_per_seq):
    tm = hc_ref.shape[0]
    pos = pl.program_id(0) % blocks_per_seq
    rows = tm + 2 * FFN_HALO
    lhs_scr[0:FFN_HALO, :] = jnp.where(pos > 0, hp_ref[...], jnp.zeros_like(hp_ref))
    lhs_scr[FFN_HALO:FFN_HALO + tm, :] = hc_ref[...]
    lhs_scr[FFN_HALO + tm:rows, :] = jnp.where(pos < blocks_per_seq - 1, hn_ref[...],
                                               jnp.zeros_like(hn_ref))

    n_chunks = D_FF // FFN_COLS

    def chunk_cols(j, half):
        return pl.ds(pl.multiple_of(j * FFN_COLS + half * D_FF, LANES), FFN_COLS)

    def project(j, slot):
        for half in range(2):
            u_scr[2 * slot + half] = jnp.dot(lhs_scr[...], w_ref[:, chunk_cols(j, half)],
                                             preferred_element_type=F32)

    def conv(j, slot, half):
        return _conv3_rows(u_scr[2 * slot + half], cw_ref[:, chunk_cols(j, half)],
                           cb_ref[:, chunk_cols(j, half)], FFN_HALO, tm)

    def finish(j, slot):
        act_ref[:, chunk_cols(j, 0)] = (_silu(conv(j, slot, 0)) * conv(j, slot, 1)).astype(BF16)

    _software_pipeline(n_chunks, project, finish, depth=1, slots=FFN_SLOTS, rolled=True)


def _ffn_up(h, w_up, conv_w, conv_b, seq, tm=1024):
    t_rows = h.shape[0]
    hpb = tm // FFN_HALO
    last_halo = t_rows // FFN_HALO - 1
    width = 2 * D_FF
    return pl.pallas_call(
        functools.partial(_ffn_up_kernel, blocks_per_seq=seq // tm),
        grid=(t_rows // tm,),
        in_specs=[
            pl.BlockSpec((tm, D_MODEL), lambda i: (i, 0)),
            pl.BlockSpec((FFN_HALO, D_MODEL), lambda i: (jnp.maximum(i * hpb - 1, 0), 0)),
            pl.BlockSpec((FFN_HALO, D_MODEL), lambda i: (jnp.minimum((i + 1) * hpb, last_halo), 0)),
            _const_spec((D_MODEL, width)), _const_spec((3, width)), _const_spec((1, width)),
        ],
        out_specs=pl.BlockSpec((tm, D_FF), lambda i: (i, 0)),
        out_shape=jax.ShapeDtypeStruct((t_rows, D_FF), BF16),
        scratch_shapes=[pltpu.VMEM((tm + 2 * FFN_HALO, D_MODEL), BF16),
                        pltpu.VMEM((2 * FFN_SLOTS, tm + 2 * FFN_HALO, FFN_COLS), F32)],
        compiler_params=_params("parallel"),
        name="ffn_up",
    )(h, h, h, w_up.astype(BF16), conv_w.T, conv_b.reshape(1, width))


def _ffn_down_kernel(a_ref, x1_ref, wd_ref, nw_ref, o_ref):
    acc = x1_ref[...] + jnp.dot(a_ref[...], wd_ref[...].astype(BF16), preferred_element_type=F32)
    o_ref[...] = _rms(acc, nw_ref[...])


def _ffn_down(act, x1, w_down, norm_w, tm=1024):
    t_rows = act.shape[0]
    return pl.pallas_call(
        _ffn_down_kernel,
        grid=(t_rows // tm,),
        in_specs=[pl.BlockSpec((tm, D_FF), lambda i: (i, 0)), pl.BlockSpec((tm, D_MODEL), lambda i: (i, 0)),
                  _resident_f32_weight((D_FF, D_MODEL), (0, 0)), _const_spec((1, D_MODEL))],
        out_specs=pl.BlockSpec((tm, D_MODEL), lambda i: (i, 0)),
        out_shape=jax.ShapeDtypeStruct((t_rows, D_MODEL), F32),
        compiler_params=_params("parallel"),
        name="ffn_down",
    )(act, x1, w_down, norm_w.reshape(1, D_MODEL))


def kernel(x, norm1_w, w_in, ssm_conv_w, ssm_conv_b, a_log_f, a_log_b, dt_bias_f, dt_bias_b, d_skip,
           ssm_norm_w, w_out, norm2_w, w_up, ffn_conv_w, ffn_conv_b, w_down, final_norm_w):
    b, s, d = x.shape
    depth = w_in.shape[0]
    x2d = x.reshape(b * s, d)
    for layer in range(depth):
        w_t = w_in[layer].T.astype(BF16)
        (q4, k4, v4, q16, k16, v16, z) = _in_projection(x2d, norm1_w[layer], w_t, b, s)
        xa, dt, ldt, cs = _ssm_projection(
            x2d, norm1_w[layer], w_t, ssm_conv_w[layer], ssm_conv_b[layer],
            a_log_f[layer], a_log_b[layer], dt_bias_f[layer], dt_bias_b[layer], s)
        attn = _dilated_attention([(q4, k4, v4), (q4, k4, v4), (q16, k16, v16)])
        sh = lambda t: t.reshape(b, s, t.shape[-1])
        ssm = _ssm_mixer(sh(z), sh(xa), sh(dt), sh(ldt), sh(cs), a_log_f[layer], a_log_b[layer],
                         d_skip[layer], ssm_norm_w[layer])
        x1, h2 = _out_projection(x2d, attn, ssm.reshape(b * s, -1), w_out[layer], norm2_w[layer], s)
        act = _ffn_up(h2, w_up[layer], ffn_conv_w[layer], ffn_conv_b[layer], s)
        assert depth == 1
        x2d = _ffn_down(act, x1, w_down[layer], final_norm_w)
    return x2d.reshape(b, s, d)
```
